```python
import functools
import jax, jax.numpy as jnp
from jax import lax
import numpy as np

D_MODEL = 1024
BATCH = 16
SEQ = 2048
DEPTH = 1
DEC_BATCH = 128
DEC_SEQ = 1
PAST_LEN = 16384
PAGE_SIZE = 128

MIX_WIDTH = D_MODEL
LRU_WIDTH = MIX_WIDTH // 2
LRU_HEADS = 8
LRU_HEAD_DIM = LRU_WIDTH // LRU_HEADS
CONV_WIDTH = 4
LRU_C = 8.0
MLA_WIDTH = MIX_WIDTH - LRU_WIDTH
MLA_HEADS = 8
QK_NOPE_DIM = 64
QK_ROPE_DIM = 32
QK_HEAD_DIM = QK_NOPE_DIM + QK_ROPE_DIM
V_HEAD_DIM = MLA_WIDTH // MLA_HEADS
Q_LORA_RANK = 256
KV_LORA_RANK = 128
ROPE_THETA = 10000.0
ATTN_SCALE = QK_HEAD_DIM ** -0.5
D_FF = 2816
FFN_CONV_WIDTH = 3
Q_BLOCK = 128
EPS = 1e-6
IN_COLS = 2 * LRU_WIDTH + Q_LORA_RANK + KV_LORA_RANK + QK_ROPE_DIM
IN_SPLITS = (LRU_WIDTH, 2 * LRU_WIDTH, 2 * LRU_WIDTH + Q_LORA_RANK,
             2 * LRU_WIDTH + Q_LORA_RANK + KV_LORA_RANK)

kernel_name = 'hymba_rglru_mla_convffn_adaln_step'


def rms_norm(x, g):
    xf = x.astype(jnp.float32)
    y = xf * lax.rsqrt(jnp.mean(xf * xf, axis=-1, keepdims=True) + EPS)
    return (y * g.astype(jnp.float32)).astype(x.dtype)


def rope_cos_sin(pos):
    inv = ROPE_THETA ** (-jnp.arange(0, QK_ROPE_DIM, 2, dtype=jnp.float32) / QK_ROPE_DIM)
    ang = pos.astype(jnp.float32)[:, None] * inv[None, :]
    return jnp.cos(ang), jnp.sin(ang)


def apply_rope(x, cos, sin):
    xf = x.astype(jnp.float32)
    half = QK_ROPE_DIM // 2
    x1, x2 = xf[..., :half], xf[..., half:]
    return jnp.concatenate([x1 * cos - x2 * sin, x2 * cos + x1 * sin], axis=-1).astype(x.dtype)


def causal_dwconv(x, buf, w, b):
    k = w.shape[0]
    s = x.shape[1]
    xx = jnp.concatenate([buf.astype(x.dtype), x], axis=1)
    out = b + xx[:, 0:s] * w[0]
    for j in range(1, k):
        out = out + xx[:, j:j + s] * w[j]
    return out, xx[:, xx.shape[1] - (k - 1):]


def rg_lru(x, pos, h0, w_a, b_a, w_x, b_x, lam):
    bsz, s, w = x.shape
    xh = x.reshape(bsz, s, LRU_HEADS, LRU_HEAD_DIM)
    r = jax.nn.sigmoid((jnp.einsum('bshi,hij->bshj', xh, w_a).reshape(bsz, s, w) + b_a).astype(jnp.float32))
    i = jax.nn.sigmoid((jnp.einsum('bshi,hij->bshj', xh, w_x).reshape(bsz, s, w) + b_x).astype(jnp.float32))
    log_a = -LRU_C * r * jax.nn.softplus(-lam.astype(jnp.float32))
    a = jnp.exp(log_a)
    mult = jnp.sqrt(-jnp.expm1(2.0 * log_a))
    mult = jnp.where((pos == 0)[None, :, None], 1.0, mult)
    u = mult * i * x.astype(jnp.float32)

    def step(h, au):
        a_t, u_t = au
        h = a_t * h + u_t
        return h, h

    h_last, hs = lax.scan(step, h0.astype(jnp.float32), (jnp.swapaxes(a, 0, 1), jnp.swapaxes(u, 0, 1)))
    return jnp.swapaxes(hs, 0, 1).astype(x.dtype), h_last.astype(h0.dtype)


def mla_prompt_attention(q_nope, q_rope, kv_lat, k_rope, w_ukv, k_nope_g):
    bsz, s = q_nope.shape[:2]
    kv = (kv_lat @ w_ukv).reshape(bsz, s, MLA_HEADS, QK_NOPE_DIM + V_HEAD_DIM)
    k_nope = rms_norm(kv[..., :QK_NOPE_DIM], k_nope_g)
    v = kv[..., QK_NOPE_DIM:]
    nb = s // Q_BLOCK
    key_pos = jnp.arange(s)
    neg = jnp.finfo(jnp.float32).min

    def to_blocks(t):
        return jnp.moveaxis(t.reshape(bsz, nb, Q_BLOCK, *t.shape[2:]), 1, 0)

    def one_block(args):
        qn, qr, blk = args
        q_pos = blk * Q_BLOCK + jnp.arange(Q_BLOCK)
        sc = (jnp.einsum('bqhd,bkhd->bhqk', qn, k_nope)
              + jnp.einsum('bqhd,bkd->bhqk', qr, k_rope)).astype(jnp.float32) * ATTN_SCALE
        sc = jnp.where(key_pos[None, :] <= q_pos[:, None], sc, neg)
        p = jax.nn.softmax(sc, axis=-1).astype(v.dtype)
        return jnp.einsum('bhqk,bkhd->bqhd', p, v)

    out = lax.map(one_block, (to_blocks(q_nope), to_blocks(q_rope), jnp.arange(nb)))
    return jnp.moveaxis(out, 0, 1).reshape(bsz, s, MLA_HEADS, V_HEAD_DIM)


def mla_sample_attention(q_nope, q_rope, kv_lat, k_rope, cache_lat, cache_kr, page_table, w_ukv, k_nope_g):
    n_past = page_table.shape[1] * PAGE_SIZE
    ds = q_nope.shape[1]
    key_pos = jnp.arange(n_past + ds)
    q_pos = n_past + jnp.arange(ds)
    mask = key_pos[None, :] <= q_pos[:, None]
    neg = jnp.finfo(jnp.float32).min

    def one_seq(args):
        qn, qr, lat_new, kr_new, pages = args
        lat = jnp.concatenate([cache_lat[pages].reshape(n_past, KV_LORA_RANK).astype(lat_new.dtype), lat_new], axis=0)
        kr = jnp.concatenate([cache_kr[pages].reshape(n_past, QK_ROPE_DIM).astype(kr_new.dtype), kr_new], axis=0)
        kv = (lat @ w_ukv).reshape(n_past + ds, MLA_HEADS, QK_NOPE_DIM + V_HEAD_DIM)
        k_nope = rms_norm(kv[..., :QK_NOPE_DIM], k_nope_g)
        v = kv[..., QK_NOPE_DIM:]
        sc = (jnp.einsum('qhd,khd->hqk', qn, k_nope)
              + jnp.einsum('qhd,kd->hqk', qr, kr)).astype(jnp.float32) * ATTN_SCALE
        sc = jnp.where(mask, sc, neg)
        p = jax.nn.softmax(sc, axis=-1).astype(v.dtype)
        return jnp.einsum('hqk,khd->qhd', p, v)

    return lax.map(one_seq, (q_nope, q_rope, kv_lat, k_rope, page_table))


def hybrid_layer(x, c, pos, lru_h0, lru_buf, ffn_buf, attend, lp):
    bsz, s, _ = x.shape
    mod = jnp.dot(jax.nn.silu(c), lp['w_ada']) + lp['b_ada']
    shift1, scale1, gate1, shift2, scale2, gate2 = jnp.split(mod[:, None, :], 6, axis=-1)

    h = rms_norm(x, lp['norm1_g']) * (1.0 + scale1) + shift1
    z = h @ lp['w_in']
    x_lru, g_lru, q_lat, kv_lat, k_rope = jnp.split(z, IN_SPLITS, axis=-1)

    x_conv, lru_buf_new = causal_dwconv(x_lru, lru_buf, lp['lru_conv_w'], lp['lru_conv_b'])
    hs, h_last = rg_lru(x_conv, pos, lru_h0, lp['lru_w_a'], lp['lru_b_a'], lp['lru_w_x'], lp['lru_b_x'], lp['lru_lambda'])
    lru_out = hs * jax.nn.gelu(g_lru)

    cos, sin = rope_cos_sin(pos)
    q = (rms_norm(q_lat, lp['q_lora_norm_g']) @ lp['w_uq']).reshape(bsz, s, MLA_HEADS, QK_HEAD_DIM)
    q_nope = rms_norm(q[..., :QK_NOPE_DIM], lp['q_nope_norm_g'])
    q_rope = apply_rope(rms_norm(q[..., QK_NOPE_DIM:], lp['q_rope_norm_g']), cos[:, None, :], sin[:, None, :])
    kv_lat = rms_norm(kv_lat, lp['kv_lora_norm_g'])
    k_rope = apply_rope(rms_norm(k_rope, lp['k_rope_norm_g']), cos, sin)
    attn = attend(q_nope, q_rope, kv_lat, k_rope).reshape(bsz, s, MLA_WIDTH)

    mixed = jnp.concatenate([rms_norm(lru_out, lp['lru_out_norm_g']),
                             rms_norm(attn, lp['mla_out_norm_g'])], axis=-1) @ lp['w_o']
    x = x + gate1 * mixed

    h2 = rms_norm(x, lp['norm2_g']) * (1.0 + scale2) + shift2
    up, ffn_buf_new = causal_dwconv(h2 @ lp['w_up'], ffn_buf, lp['ffn_conv_w'], lp['ffn_conv_b'])
    val, gt = jnp.split(up, 2, axis=-1)
    x = x + gate2 * ((jax.nn.gelu(gt) * val) @ lp['w_down'])
    return x, kv_lat, k_rope, h_last, lru_buf_new, ffn_buf_new


def setup_inputs(seed: int = 0) -> dict:
    key = jax.random.key(seed)
    ks = iter(jax.random.split(key, 48))
    f32 = jnp.float32

    def nrm(shape, scale=1.0):
        return jax.random.normal(next(ks), shape, f32) * scale

    def gain(shape):
        return 1.0 + 0.02 * nrm(shape)

    L = DEPTH
    n_pages = PAST_LEN // PAGE_SIZE
    n_pool = (DEC_BATCH * n_pages * 5) // 4

    x_prompt = nrm((BATCH, SEQ, D_MODEL))
    x_sample = nrm((DEC_BATCH, DEC_SEQ, D_MODEL))
    cache_kv_latent = nrm((L, n_pool, PAGE_SIZE, KV_LORA_RANK))
    cache_k_rope = nrm((L, n_pool, PAGE_SIZE, QK_ROPE_DIM))
    state_lru_h = nrm((L, DEC_BATCH, LRU_WIDTH), 0.5)
    state_lru_conv = nrm((L, DEC_BATCH, CONV_WIDTH - 1, LRU_WIDTH))
    state_ffn_conv = nrm((L, DEC_BATCH, FFN_CONV_WIDTH - 1, 2 * D_FF))
    perm = jax.random.permutation(next(ks), n_pool)
    page_table = perm[:DEC_BATCH * n_pages].reshape(DEC_BATCH, n_pages).astype(jnp.int32)
    c_prompt = nrm((BATCH, D_MODEL))
    c_sample = nrm((DEC_BATCH, D_MODEL))

    a0 = jax.random.uniform(next(ks), (L, LRU_WIDTH), f32, 0.9, 0.999)
    sg = a0 ** (1.0 / LRU_C)
    lru_lambda = jnp.log(sg) - jnp.log1p(-sg)

    return {
        'x_prompt': x_prompt,
        'x_sample': x_sample,
        'cache_kv_latent': cache_kv_latent,
        'cache_k_rope': cache_k_rope,
        'state_lru_h': state_lru_h,
        'state_lru_conv': state_lru_conv,
        'state_ffn_conv': state_ffn_conv,
        'page_table': page_table,
        'c_prompt': c_prompt,
        'c_sample': c_sample,
        'w_ada': nrm((L, D_MODEL, 6 * D_MODEL), 0.5 * D_MODEL ** -0.5),
        'b_ada': nrm((L, 6 * D_MODEL), 0.01),
        'norm1_g': gain((L, D_MODEL)),
        'w_in': nrm((L, D_MODEL, IN_COLS), D_MODEL ** -0.5),
        'lru_conv_w': nrm((L, CONV_WIDTH, LRU_WIDTH), CONV_WIDTH ** -0.5),
        'lru_conv_b': nrm((L, LRU_WIDTH), 0.01),
        'lru_w_a': nrm((L, LRU_HEADS, LRU_HEAD_DIM, LRU_HEAD_DIM), LRU_HEAD_DIM ** -0.5),
        'lru_b_a': nrm((L, LRU_WIDTH), 0.01),
        'lru_w_x': nrm((L, LRU_HEADS, LRU_HEAD_DIM, LRU_HEAD_DIM), LRU_HEAD_DIM ** -0.5),
        'lru_b_x': nrm((L, LRU_WIDTH), 0.01),
        'lru_lambda': lru_lambda,
        'q_lora_norm_g': gain((L, Q_LORA_RANK)),
        'w_uq': nrm((L, Q_LORA_RANK, MLA_HEADS * QK_HEAD_DIM), Q_LORA_RANK ** -0.5),
        'q_nope_norm_g': gain((L, QK_NOPE_DIM)),
        'q_rope_norm_g': gain((L, QK_ROPE_DIM)),
        'kv_lora_norm_g': gain((L, KV_LORA_RANK)),
        'k_rope_norm_g': gain((L, QK_ROPE_DIM)),
        'w_ukv': nrm((L, KV_LORA_RANK, MLA_HEADS * (QK_NOPE_DIM + V_HEAD_DIM)), KV_LORA_RANK ** -0.5),
        'k_nope_norm_g': gain((L, QK_NOPE_DIM)),
        'lru_out_norm_g': gain((L, LRU_WIDTH)),
        'mla_out_norm_g': gain((L, MLA_WIDTH)),
        'w_o': nrm((L, MIX_WIDTH, D_MODEL), MIX_WIDTH ** -0.5),
        'norm2_g': gain((L, D_MODEL)),
        'w_up': nrm((L, D_MODEL, 2 * D_FF), D_MODEL ** -0.5),
        'ffn_conv_w': nrm((L, FFN_CONV_WIDTH, 2 * D_FF), FFN_CONV_WIDTH ** -0.5),
        'ffn_conv_b': nrm((L, 2 * D_FF), 0.01),
        'w_down': nrm((L, D_FF, D_MODEL), D_FF ** -0.5),
    }


def reference(x_prompt, x_sample, cache_kv_latent, cache_k_rope, state_lru_h, state_lru_conv,
              state_ffn_conv, page_table, c_prompt, c_sample, w_ada, b_ada, norm1_g, w_in,
              lru_conv_w, lru_conv_b, lru_w_a, lru_b_a, lru_w_x, lru_b_x, lru_lambda,
              q_lora_norm_g, w_uq, q_nope_norm_g, q_rope_norm_g, kv_lora_norm_g, k_rope_norm_g,
              w_ukv, k_nope_norm_g, lru_out_norm_g, mla_out_norm_g, w_o, norm2_g, w_up,
              ffn_conv_w, ffn_conv_b, w_down):
    bsz, s = x_prompt.shape[:2]
    dbsz, ds = x_sample.shape[:2]
    pos_prompt = jnp.arange(s)
    pos_sample = PAST_LEN + jnp.arange(ds)
    dt = x_prompt.dtype

    y_p, y_s = x_prompt, x_sample
    outs_p = ([], [], [], [], [])
    outs_s = ([], [], [], [], [])
    for l in range(DEPTH):
        lp = dict(w_ada=w_ada[l], b_ada=b_ada[l], norm1_g=norm1_g[l], w_in=w_in[l],
                  lru_conv_w=lru_conv_w[l], lru_conv_b=lru_conv_b[l], lru_w_a=lru_w_a[l],
                  lru_b_a=lru_b_a[l], lru_w_x=lru_w_x[l], lru_b_x=lru_b_x[l], lru_lambda=lru_lambda[l],
                  q_lora_norm_g=q_lora_norm_g[l], w_uq=w_uq[l], q_nope_norm_g=q_nope_norm_g[l],
                  q_rope_norm_g=q_rope_norm_g[l], kv_lora_norm_g=kv_lora_norm_g[l],
                  k_rope_norm_g=k_rope_norm_g[l], lru_out_norm_g=lru_out_norm_g[l],
                  mla_out_norm_g=mla_out_norm_g[l], w_o=w_o[l], norm2_g=norm2_g[l], w_up=w_up[l],
                  ffn_conv_w=ffn_conv_w[l], ffn_conv_b=ffn_conv_b[l], w_down=w_down[l])
        attend_prompt = functools.partial(mla_prompt_attention, w_ukv=w_ukv[l], k_nope_g=k_nope_norm_g[l])
        attend_sample = functools.partial(mla_sample_attention, cache_lat=cache_kv_latent[l],
                                          cache_kr=cache_k_rope[l], page_table=page_table,
                                          w_ukv=w_ukv[l], k_nope_g=k_nope_norm_g[l])
        res_p = hybrid_layer(y_p, c_prompt, pos_prompt,
                             jnp.zeros((bsz, LRU_WIDTH), dt),
                             jnp.zeros((bsz, CONV_WIDTH - 1, LRU_WIDTH), dt),
                             jnp.zeros((bsz, FFN_CONV_WIDTH - 1, 2 * D_FF), dt),
                             attend_prompt, lp)
        res_s = hybrid_layer(y_s, c_sample, pos_sample, state_lru_h[l], state_lru_conv[l],
                             state_ffn_conv[l], attend_sample, lp)
        y_p, y_s = res_p[0], res_s[0]
        for j in range(5):
            outs_p[j].append(res_p[j + 1])
            outs_s[j].append(res_s[j + 1])

    kv_latent_prompt = jnp.stack(outs_p[0])
    k_rope_prompt = jnp.stack(outs_p[1])
    lru_h_prompt = jnp.stack(outs_p[2])
    lru_conv_prompt = jnp.stack(outs_p[3])
    ffn_conv_prompt = jnp.stack(outs_p[4])
    kv_latent_sample = jnp.stack(outs_s[0])
    k_rope_sample = jnp.stack(outs_s[1])
    lru_h_sample = jnp.stack(outs_s[2])
    lru_conv_sample = jnp.stack(outs_s[3])
    ffn_conv_sample = jnp.stack(outs_s[4])
    return (y_p, y_s, kv_latent_prompt, k_rope_prompt, lru_h_prompt, lru_conv_prompt, ffn_conv_prompt,
            kv_latent_sample, k_rope_sample, lru_h_sample, lru_conv_sample, ffn_conv_sample)
```

```python
import functools

import numpy as np
import jax
import jax.numpy as jnp
from jax import lax
from jax.experimental import pallas as pl
from jax.experimental.pallas import tpu as pltpu

f32 = jnp.float32
bf16 = jnp.bfloat16

EPS = 1e-6
LRU_C = 8.0
ROPE_THETA = 10000.0
LANES = 128
MXU_DIM = 256
VMEM_LIMIT = 56 * 1024 * 1024

_NOPE, _ROPE = 64, 32
_R1_LO, _R2_LO = 48, 112


def _h128_src():
    src = np.full((LANES,), -1, np.int32)
    src[0:48] = np.arange(0, 48)
    src[48:64] = _NOPE + np.arange(0, 16)
    src[64:80] = np.arange(48, 64)
    src[112:128] = _NOPE + 16 + np.arange(0, 16)
    return src


def _dot(a, b):
    return jnp.dot(a, b, preferred_element_type=f32)


def _dot_nt(a, b):
    return lax.dot_general(a, b, (((1,), (1,)), ((), ())), preferred_element_type=f32)


def _rms(x, g):
    ms = jnp.mean(x * x, axis=-1, keepdims=True)
    return x * lax.rsqrt(ms + EPS) * g


def _neg_expm1_2x(y):
    t = jnp.tanh(y)
    return -2.0 * t / (1.0 - t)


def _lru_gates(x_conv, wa_ref, ba, wx_ref, bx, lam):
    xb = x_conv.astype(bf16)
    ng = wa_ref.shape[0]
    ra = jnp.concatenate([_dot(xb[:, g * MXU_DIM:(g + 1) * MXU_DIM], wa_ref[g]) for g in range(ng)], axis=-1) + ba
    ia = jnp.concatenate([_dot(xb[:, g * MXU_DIM:(g + 1) * MXU_DIM], wx_ref[g]) for g in range(ng)], axis=-1) + bx
    r = jax.nn.sigmoid(ra)
    ig = jax.nn.sigmoid(ia)
    log_a = (-LRU_C) * r * jax.nn.softplus(-lam)
    a = jnp.exp(log_a)
    mult = jnp.sqrt(_neg_expm1_2x(log_a))
    return a, mult, ig * x_conv


def _q_heads(q, gq, mn, mr, cc, ss, n_heads, to3d):
    outs = []
    for h in range(n_heads):
        xh = q[:, h * LANES:(h + 1) * LANES]
        x2 = xh * xh
        ssn = jnp.sum(x2 * mn, axis=-1, keepdims=True)
        ssr = jnp.sum(x2 * mr, axis=-1, keepdims=True)
        inv = lax.rsqrt(ssn * (1.0 / _NOPE) + EPS) * mn + lax.rsqrt(ssr * (1.0 / _ROPE) + EPS) * mr
        xn = xh * inv * gq
        outs.append(to3d(xn) * cc + to3d(pltpu.roll(xn, LANES // 2, 1)) * ss)
    return outs


def _krope(x, gkr, cc, ss, to3d):
    ssq = jnp.sum(x * x, axis=-1, keepdims=True)
    xn = x * lax.rsqrt(ssq * (1.0 / _ROPE) + EPS) * gkr
    return to3d(xn) * cc + to3d(pltpu.roll(xn, LANES // 2, 1)) * ss


def _mod_kernel(c_ref, w_ref, b_ref, o_ref):
    c = c_ref[...]
    sc = (c * jax.nn.sigmoid(c)).astype(bf16)
    o_ref[...] = _dot(sc, w_ref[...].astype(bf16)) + b_ref[...]


def _mod_call(c_all, w_ada, b_ada):
    m, d = c_all.shape
    n = w_ada.shape[1]
    tn = 1536 if n % 1536 == 0 else n
    return pl.pallas_call(
        _mod_kernel,
        grid=(n // tn,),
        in_specs=[pl.BlockSpec((m, d), lambda j: (0, 0)),
                  pl.BlockSpec((d, tn), lambda j: (0, j)),
                  pl.BlockSpec((1, tn), lambda j: (0, j))],
        out_specs=pl.BlockSpec((m, tn), lambda j: (0, j)),
        out_shape=jax.ShapeDtypeStruct((m, n), f32),
        compiler_params=pltpu.CompilerParams(dimension_semantics=("arbitrary",), vmem_limit_bytes=VMEM_LIMIT),
        name="adaln_mod",
    )(c_all, w_ada, b_ada.reshape(1, n))


def _inproj_kernel(x_ref, sh_ref, sc_ref, cc_ref, ss_ref, n1g_ref, win_ref, cw_ref, cb_ref,
                   wa_ref, ba_ref, wx_ref, bx_ref, lam_ref, qlg_ref, wuq_ref, gq_ref, mn_ref, mr_ref,
                   kvg_ref, gkr_ref, wk_ref, gk_ref, wv_ref, log_ref,
                   lru_o, q_o, k_o, v_o, kvlat_o, krope_o, hlast_o, xtail_o,
                   xprev_s, a_s, u_s, hs_s, h_s, *, n_heads):
    i = pl.program_id(0)
    nb, ts, d = x_ref.shape
    m = nb * ts
    w = lam_ref.shape[-1]
    ql = qlg_ref.shape[-1]
    kl = kvg_ref.shape[-1]
    kconv = cw_ref.shape[0]
    nlc = w // LANES

    @pl.when(i == 0)
    def _():
        xprev_s[...] = jnp.zeros_like(xprev_s)
        h_s[...] = jnp.zeros_like(h_s)

    def to3d(v):
        return v.reshape(nb, ts, v.shape[-1])

    x = x_ref[...]
    h = _rms(x, n1g_ref[...]) * (1.0 + sc_ref[...]) + sh_ref[...]
    z = _dot(h.reshape(m, d).astype(bf16), win_ref[...])
    x_lru = z[:, 0:w]
    g_lru = z[:, w:2 * w]
    q_lat = z[:, 2 * w:2 * w + ql]
    kv_lat = z[:, 2 * w + ql:2 * w + ql + kl]
    kr_pre = z[:, 2 * w + ql + kl:2 * w + ql + kl + LANES]

    t_idx = lax.broadcasted_iota(jnp.int32, (m, 1), 0) & (ts - 1)
    xp = xprev_s[...]
    x_conv = cb_ref[...]
    for j in range(kconv - 1):
        k = kconv - 1 - j
        sh = jnp.where(t_idx >= k, pltpu.roll(x_lru, k, 0), pltpu.roll(xp, (m + k - ts) % m, 0))
        x_conv = x_conv + sh * cw_ref[j:j + 1, :]
    x_conv = x_conv + x_lru * cw_ref[kconv - 1:kconv, :]
    xprev_s[...] = x_lru
    xtail_o[...] = to3d(x_lru)[:, ts - 8:, :]

    a, mult, ux = _lru_gates(x_conv, wa_ref, ba_ref[...], wx_ref, bx_ref[...], lam_ref[...])
    mult = jnp.where(jnp.logical_and(t_idx == 0, i == 0), 1.0, mult)
    u = mult * ux
    for j in range(nlc):
        a_s[j] = a[:, j * LANES:(j + 1) * LANES]
        u_s[j] = u[:, j * LANES:(j + 1) * LANES]

    def scan_step(t, hc):
        out = []
        for j in range(nlc):
            hj = a_s[j, pl.ds(t, nb, stride=ts), :] * hc[j] + u_s[j, pl.ds(t, nb, stride=ts), :]
            hs_s[j, pl.ds(t, nb, stride=ts), :] = hj
            out.append(hj)
        return tuple(out)

    hc = lax.fori_loop(0, ts, scan_step, tuple(h_s[j] for j in range(nlc)), unroll=8)
    for j in range(nlc):
        h_s[j] = hc[j]
    hlast_o[...] = jnp.concatenate(list(hc), axis=-1)
    hs = jnp.concatenate([hs_s[j] for j in range(nlc)], axis=-1)
    lru_out = hs * jax.nn.gelu(g_lru)
    lru_o[...] = to3d(_rms(lru_out, log_ref[...]).astype(bf16))

    cc = cc_ref[...][None]
    ss = ss_ref[...][None]
    qn = _rms(q_lat, qlg_ref[...]).astype(bf16)
    q = _dot(qn, wuq_ref[...])
    for hd, o in enumerate(_q_heads(q, gq_ref[...], mn_ref[...], mr_ref[...], cc, ss, n_heads, to3d)):
        q_o[:, :, hd * LANES:(hd + 1) * LANES] = o.astype(bf16)

    kvn = _rms(kv_lat, kvg_ref[...])
    kvlat_o[...] = to3d(kvn)
    kr = _krope(kr_pre, gkr_ref[...], cc, ss, to3d)
    krope_o[...] = jnp.concatenate([kr[:, :, _R1_LO:_R1_LO + 16], kr[:, :, _R2_LO:_R2_LO + 16]], axis=-1)
    kvb = kvn.astype(bf16)
    kk = _dot(kvb, wk_ref[...])
    v_o[...] = to3d(_dot(kvb, wv_ref[...]).astype(bf16))
    gk = gk_ref[...]
    for hd in range(n_heads):
        kh = kk[:, hd * LANES:(hd + 1) * LANES]
        ssq = jnp.sum(kh * kh, axis=-1, keepdims=True)
        khn = kh * lax.rsqrt(ssq * (1.0 / _NOPE) + EPS) * gk
        k_o[:, :, hd * LANES:(hd + 1) * LANES] = (to3d(khn) + kr).astype(bf16)


def _const_spec(shape):
    nd = len(shape)
    return pl.BlockSpec(shape, lambda *_: (0,) * nd, pipeline_mode=pl.Buffered(1))


def _inproj_call(x, sh1, sc1, cc, ss, p, n_heads, ts):
    b, s, d = x.shape
    w = p["lam"].shape[-1]
    kl = p["kvg"].shape[-1]
    hw = n_heads * LANES
    m = b * ts
    consts = [p["n1g"], p["win"], p["cw"], p["cb"], p["wa"], p["ba"], p["wx"], p["bx"], p["lam"], p["qlg"],
              p["wuq"], p["gq"], p["mn"], p["mr"], p["kvg"], p["gkr"], p["wk"], p["gk"], p["wv"], p["log"]]
    in_specs = [pl.BlockSpec((b, ts, d), lambda i: (0, i, 0)),
                _const_spec(sh1.shape), _const_spec(sc1.shape),
                pl.BlockSpec((ts, LANES), lambda i: (i, 0)),
                pl.BlockSpec((ts, LANES), lambda i: (i, 0))] + [_const_spec(c.shape) for c in consts]

    def tile(n, dt):
        return pl.BlockSpec((b, ts, n), lambda i: (0, i, 0)), jax.ShapeDtypeStruct((b, s, n), dt)

    outs = [tile(w, bf16), tile(hw, bf16), tile(hw, bf16), tile(hw, bf16), tile(kl, f32), tile(_ROPE, f32),
            (pl.BlockSpec((b, w), lambda i: (0, 0)), jax.ShapeDtypeStruct((b, w), f32)),
            (pl.BlockSpec((b, 8, w), lambda i: (0, 0, 0)), jax.ShapeDtypeStruct((b, 8, w), f32))]
    nlc = w // LANES
    return pl.pallas_call(
        functools.partial(_inproj_kernel, n_heads=n_heads),
        grid=(s // ts,),
        in_specs=in_specs,
        out_specs=[o[0] for o in outs],
        out_shape=[o[1] for o in outs],
        scratch_shapes=[pltpu.VMEM((m, w), f32), pltpu.VMEM((nlc, m, LANES), f32), pltpu.VMEM((nlc, m, LANES), f32),
                        pltpu.VMEM((nlc, m, LANES), f32), pltpu.VMEM((nlc, b, LANES), f32)],
        compiler_params=pltpu.CompilerParams(dimension_semantics=("arbitrary",), vmem_limit_bytes=VMEM_LIMIT),
        name="prompt_inproj",
    )(x, sh1, sc1, cc, ss, *consts)


def _attn_kernel(q_ref, k_ref, v_ref, g_ref, o_ref, *, n_heads, tk):
    qi = pl.program_id(1)
    tq = q_ref.shape[1]
    nmask = tq // tk
    n_full = qi * nmask
    row = lax.broadcasted_iota(jnp.int32, (tq, tk), 0)
    col = lax.broadcasted_iota(jnp.int32, (tq, tk), 1)
    pair_out = []
    for hd in range(n_heads):
        hs = slice(hd * LANES, (hd + 1) * LANES)
        q = q_ref[0, :, hs]

        def step(j, carry, masked_off=None):
            mx, l, acc = carry
            start = pl.multiple_of(j * tk, tk)
            k = k_ref[0, pl.ds(start, tk), hs]
            v = v_ref[0, pl.ds(start, tk), hs]
            s = _dot_nt(q, k)
            if masked_off is not None:
                s = jnp.where(col + masked_off * tk <= row, s, -1e30)
            m_new = jnp.maximum(mx, jnp.max(s, axis=-1, keepdims=True))
            alpha = jnp.exp(mx - m_new)
            pm = jnp.exp(s - m_new)
            l = alpha * l + jnp.sum(pm, axis=-1, keepdims=True)
            acc = alpha * acc + _dot(pm.astype(bf16), v)
            return m_new, l, acc

        carry = (jnp.full((tq, 1), -1e30, f32), jnp.zeros((tq, 1), f32), jnp.zeros((tq, LANES), f32))
        carry = lax.fori_loop(0, n_full, step, carry)
        for jm in range(nmask):
            carry = step(n_full + jm, carry, masked_off=jm)
        _, l, acc = carry
        pair_out.append(acc / l)
    o = jnp.concatenate([pair_out[2 * pp] + pair_out[2 * pp + 1] for pp in range(n_heads // 2)], axis=-1)
    o_ref[0] = _rms(o, g_ref[...]).astype(bf16)


def _attn_call(q, k, v, g, n_heads, tq, tk):
    b, s, hw = q.shape
    wout = g.shape[-1]
    return pl.pallas_call(
        functools.partial(_attn_kernel, n_heads=n_heads, tk=tk),
        grid=(b, s // tq),
        in_specs=[pl.BlockSpec((1, tq, hw), lambda bi, qi: (bi, qi, 0)),
                  pl.BlockSpec((1, s, hw), lambda bi, qi: (bi, 0, 0)),
                  pl.BlockSpec((1, s, hw), lambda bi, qi: (bi, 0, 0)),
                  pl.BlockSpec((1, wout), lambda bi, qi: (0, 0))],
        out_specs=pl.BlockSpec((1, tq, wout), lambda bi, qi: (bi, qi, 0)),
        out_shape=jax.ShapeDtypeStruct((b, s, wout), bf16),
        compiler_params=pltpu.CompilerParams(dimension_semantics=("arbitrary", "arbitrary"),
                                             vmem_limit_bytes=VMEM_LIMIT),
        name="prompt_attn",
    )(q, k, v, g)


def _mix_and_norm2(x, lru, attn, wo_ref, g1, n2g, sc2, sh2):
    w = lru.shape[-1]
    mixed = _dot(lru, wo_ref[pl.ds(0, w), :]) + _dot(attn, wo_ref[pl.ds(w, attn.shape[-1]), :])
    x1 = x + g1 * mixed
    h2 = (_rms(x1, n2g) * (1.0 + sc2) + sh2).astype(bf16)
    return x1, h2


def _ffn_kernel(x_ref, lru_ref, attn_ref, g1_ref, sh2_ref, sc2_ref, g2_ref, n2g_ref, wo_ref, wup_ref,
                fcw_ref, fcb_ref, wdn_ref, y_o, tail_o, tail_s):
    si = pl.program_id(1)
    tm = x_ref.shape[1]
    nc = wdn_ref.shape[0]
    ck = wdn_ref.shape[1]
    kconv = fcw_ref.shape[1]

    @pl.when(si == 0)
    def _():
        tail_s[...] = jnp.zeros_like(tail_s)

    x1, h2 = _mix_and_norm2(x_ref[0], lru_ref[0], attn_ref[0], wo_ref, g1_ref[0], n2g_ref[...], sc2_ref[0], sh2_ref[0])
    r8 = lax.broadcasted_iota(jnp.int32, (8, 1), 0)

    def conv(up, c):
        tail = tail_s[c]
        first = up[0:8]
        out = fcb_ref[c]
        cw = fcw_ref[c]
        for j in range(kconv - 1):
            k = kconv - 1 - j
            head = jnp.where(r8 < k, pltpu.roll(tail, k, 0), pltpu.roll(first, k, 0))
            sh = jnp.concatenate([head, pltpu.roll(up, k, 0)[8:]], axis=0)
            out = out + sh * cw[j:j + 1, :]
        out = out + up * cw[kconv - 1:kconv, :]
        tail_s[c] = up[tm - 8:]
        tail_o[0, :, c * ck:(c + 1) * ck] = up[tm - 8:]
        return out

    acc = jnp.zeros((tm, x1.shape[-1]), f32)
    for c in range(nc):
        val = conv(_dot(h2, wup_ref[c]), c)
        gt = conv(_dot(h2, wup_ref[nc + c]), nc + c)
        act = (jax.nn.gelu(gt) * val).astype(bf16)
        acc = acc + _dot(act, wdn_ref[c])
    y_o[0] = x1 + g2_ref[0] * acc


def _ffn_call(x, lru, attn, g1, sh2, sc2, g2, p, tm):
    b, s, d = x.shape
    w = lru.shape[-1]
    nc, ck, _ = p["wdn"].shape
    row = lambda n: pl.BlockSpec((1, tm, n), lambda bi, si: (bi, si, 0))
    modspec = pl.BlockSpec((1, 1, d), lambda bi, si: (bi, 0, 0))
    consts = [p["n2g"], p["wo"], p["wup"], p["fcw"], p["fcb"], p["wdn"]]
    return pl.pallas_call(
        _ffn_kernel,
        grid=(b, s // tm),
        in_specs=[row(d), row(w), row(attn.shape[-1]), modspec, modspec, modspec, modspec]
        + [_const_spec(c.shape) for c in consts],
        out_specs=[row(d), pl.BlockSpec((1, 8, 2 * nc * ck), lambda bi, si: (bi, 0, 0))],
        out_shape=[jax.ShapeDtypeStruct((b, s, d), f32), jax.ShapeDtypeStruct((b, 8, 2 * nc * ck), f32)],
        scratch_shapes=[pltpu.VMEM((2 * nc, 8, ck), f32)],
        compiler_params=pltpu.CompilerParams(dimension_semantics=("arbitrary", "arbitrary"),
                                             vmem_limit_bytes=VMEM_LIMIT),
        name="prompt_ffn",
    )(x, lru, attn, g1, sh2, sc2, g2, *consts)


def _spre_kernel(x_ref, sh_ref, sc_ref, cc_ref, ss_ref, n1g_ref, win_ref, cbuf_ref, cw_ref, cb_ref,
                 wa_ref, ba_ref, wx_ref, bx_ref, lam_ref, h0_ref, qlg_ref, wuq_ref, gq_ref, mn_ref, mr_ref,
                 kvg_ref, gkr_ref, wk_ref, gk_ref, log_ref,
                 lru_o, xlru_o, hnew_o, qabs_o, qr_o, kvn_o, kr_o, *, n_heads, first_pos):
    w = lam_ref.shape[-1]
    ql = qlg_ref.shape[-1]
    kl = kvg_ref.shape[-1]
    kconv = cw_ref.shape[0]
    ident = lambda v: v

    h = _rms(x_ref[...], n1g_ref[...]) * (1.0 + sc_ref[...]) + sh_ref[...]
    z = _dot(h.astype(bf16), win_ref[...])
    x_lru = z[:, 0:w]
    g_lru = z[:, w:2 * w]
    q_lat = z[:, 2 * w:2 * w + ql]
    kv_lat = z[:, 2 * w + ql:2 * w + ql + kl]
    kr_pre = z[:, 2 * w + ql + kl:2 * w + ql + kl + LANES]

    x_conv = cb_ref[...]
    for j in range(kconv - 1):
        x_conv = x_conv + cbuf_ref[j] * cw_ref[j:j + 1, :]
    x_conv = x_conv + x_lru * cw_ref[kconv - 1:kconv, :]
    xlru_o[...] = x_lru
    a, mult, ux = _lru_gates(x_conv, wa_ref, ba_ref[...], wx_ref, bx_ref[...], lam_ref[...])
    if first_pos:
        mult = jnp.ones_like(mult)
    hn = a * h0_ref[...] + mult * ux
    hnew_o[...] = hn
    lru_o[...] = _rms(hn * jax.nn.gelu(g_lru), log_ref[...]).astype(bf16)

    cc = cc_ref[...]
    ss = ss_ref[...]
    qn = _rms(q_lat, qlg_ref[...]).astype(bf16)
    q = _dot(qn, wuq_ref[...])
    gk = gk_ref[...]
    for hd, o in enumerate(_q_heads(q, gq_ref[...], mn_ref[...], mr_ref[...], cc, ss, n_heads, ident)):
        wkh = wk_ref[:, hd * LANES:(hd + 1) * LANES]
        qabs_o[hd] = _dot_nt((o * gk).astype(bf16), wkh).astype(bf16)
        qr_o[hd] = jnp.concatenate([o[:, _R1_LO:_R1_LO + 16], o[:, _R2_LO:_R2_LO + 16]], axis=-1).astype(bf16)

    kvn_o[...] = _rms(kv_lat, kvg_ref[...])
    kr = _krope(kr_pre, gkr_ref[...], cc, ss, ident)
    kr_o[...] = jnp.concatenate([kr[:, _R1_LO:_R1_LO + 16], kr[:, _R2_LO:_R2_LO + 16]], axis=-1)


def _spre_call(x, sh1, sc1, cc, ss, cbuf, h0, p, n_heads, first_pos):
    nb, d = x.shape
    w = p["lam"].shape[-1]
    kl = p["kvg"].shape[-1]
    out_shape = [jax.ShapeDtypeStruct((nb, w), bf16), jax.ShapeDtypeStruct((nb, w), f32),
                 jax.ShapeDtypeStruct((nb, w), f32), jax.ShapeDtypeStruct((n_heads, nb, kl), bf16),
                 jax.ShapeDtypeStruct((n_heads, nb, _ROPE), bf16), jax.ShapeDtypeStruct((nb, kl), f32),
                 jax.ShapeDtypeStruct((nb, _ROPE), f32)]
    return pl.pallas_call(
        functools.partial(_spre_kernel, n_heads=n_heads, first_pos=first_pos),
        out_shape=out_shape,
        compiler_params=pltpu.CompilerParams(vmem_limit_bytes=VMEM_LIMIT),
        name="sample_inproj",
    )(x, sh1, sc1, cc, ss, p["n1g"], p["win"], cbuf, p["cw"], p["cb"], p["wa"], p["ba"], p["wx"], p["bx"],
      p["lam"], h0, p["qlg"], p["wuq"], p["gq"], p["mn"], p["mr"], p["kvg"], p["gkr"], p["wk"], p["gk"], p["log"])


def _sattn_kernel(pt_ref, *refs, ppc, n_heads):
    lat_refs = refs[:ppc]
    kr_refs = refs[ppc:2 * ppc]
    wkt_ref, qabs_ref, qr_ref, latn_ref, krn_ref, o_ref, lhs_s, m_s, l_s, acc_s = refs[2 * ppc:]
    c = pl.program_id(1)
    nk = wkt_ref.shape[0]
    hp = qabs_ref.shape[0]
    page = lat_refs[0].shape[0]

    @pl.when(c == 0)
    def _():
        m_s[...] = jnp.full_like(m_s, -1e30)
        l_s[...] = jnp.zeros_like(l_s)
        acc_s[...] = jnp.zeros_like(acc_s)
        lhs_s[0:nk, :] = wkt_ref[...]
        lhs_s[nk:nk + hp, :] = qabs_ref[...]

    def update(latb, krb, first_only):
        tk = latb.shape[0]
        big = _dot_nt(lhs_s[...], latb)
        knt = big[0:nk]
        ssq = jnp.sum((knt * knt).reshape(n_heads, _NOPE, tk), axis=1)
        rs = lax.rsqrt(ssq * (1.0 / _NOPE) + EPS)
        sr = _dot_nt(qr_ref[...], krb)
        s8 = big[nk:nk + n_heads] * rs + sr[0:n_heads]
        s = jnp.concatenate([s8, jnp.zeros((hp - n_heads, tk), f32)], axis=0)
        if first_only:
            s = jnp.where(lax.broadcasted_iota(jnp.int32, s.shape, 1) == 0, s, -1e30)
        m_old = m_s[...]
        m_new = jnp.maximum(m_old, jnp.max(s, axis=-1, keepdims=True))
        alpha = jnp.exp(m_old - m_new)
        pm = jnp.exp(s - m_new)
        l_s[...] = alpha * l_s[...] + jnp.sum(pm, axis=-1, keepdims=True)
        acc_s[...] = alpha * acc_s[...] + _dot(pm.astype(bf16), latb)
        m_s[...] = m_new

    latb = jnp.concatenate([r[...] for r in lat_refs], axis=0).astype(bf16)
    krb = jnp.concatenate([r[...] for r in kr_refs], axis=0).astype(bf16)
    update(latb, krb, False)

    @pl.when(c == pl.num_programs(1) - 1)
    def _():
        update(jnp.broadcast_to(latn_ref[...], (page, latn_ref.shape[-1])).astype(bf16),
               jnp.broadcast_to(krn_ref[...], (page, krn_ref.shape[-1])).astype(bf16), True)
        o_ref[...] = acc_s[...] / l_s[...]


def _sattn_call(page_table, cache_lat, cache_kr, layer, wkt, qabs, qr, latn, krn, n_heads, ppc):
    nb, npg = page_table.shape
    _, _, page, kl = cache_lat.shape
    rd = cache_kr.shape[-1]
    hp = qabs.shape[1]
    nk = wkt.shape[0]

    def page_map(bi, ci, pt, *, i):
        return (layer, pt[bi * npg + ci * ppc + i], 0, 0)

    in_specs = ([pl.BlockSpec((None, None, page, kl), functools.partial(page_map, i=i)) for i in range(ppc)]
                + [pl.BlockSpec((None, None, page, rd), functools.partial(page_map, i=i)) for i in range(ppc)]
                + [pl.BlockSpec((nk, kl), lambda bi, ci, pt: (0, 0)),
                   pl.BlockSpec((None, hp, kl), lambda bi, ci, pt: (bi, 0, 0)),
                   pl.BlockSpec((None, hp, rd), lambda bi, ci, pt: (bi, 0, 0)),
                   pl.BlockSpec((None, 1, kl), lambda bi, ci, pt: (bi, 0, 0)),
                   pl.BlockSpec((None, 1, rd), lambda bi, ci, pt: (bi, 0, 0))])
    grid_spec = pltpu.PrefetchScalarGridSpec(
        num_scalar_prefetch=1,
        grid=(nb, npg // ppc),
        in_specs=in_specs,
        out_specs=pl.BlockSpec((None, hp, kl), lambda bi, ci, pt: (bi, 0, 0)),
        scratch_shapes=[pltpu.VMEM((nk + hp, kl), bf16), pltpu.VMEM((hp, 1), f32), pltpu.VMEM((hp, 1), f32),
                        pltpu.VMEM((hp, kl), f32)])
    return pl.pallas_call(
        functools.partial(_sattn_kernel, ppc=ppc, n_heads=n_heads),
        grid_spec=grid_spec,
        out_shape=jax.ShapeDtypeStruct((nb, hp, kl), f32),
        compiler_params=pltpu.CompilerParams(dimension_semantics=("arbitrary", "arbitrary"),
                                             vmem_limit_bytes=VMEM_LIMIT),
        name="sample_attn",
    )(page_table.reshape(-1), *([cache_lat] * ppc), *([cache_kr] * ppc), wkt, qabs, qr, latn, krn)


def _spost_kernel(x_ref, lru_ref, olat_ref, g1_ref, sh2_ref, sc2_ref, g2_ref, mog_ref, n2g_ref, wv_ref, wo_ref,
                  wup_ref, fcw_ref, fcb_ref, wdn_ref, fbuf_ref, y_o, up_o, *, n_heads):
    nc = wdn_ref.shape[0]
    ck = wdn_ref.shape[1]
    kconv = fcw_ref.shape[1]
    heads = [_dot(olat_ref[hd].astype(bf16), wv_ref[:, hd * LANES:(hd + 1) * LANES]) for hd in range(n_heads)]
    attn = jnp.concatenate([heads[2 * pp] + heads[2 * pp + 1] for pp in range(n_heads // 2)], axis=-1)
    attn = _rms(attn, mog_ref[...]).astype(bf16)
    x1, h2 = _mix_and_norm2(x_ref[...], lru_ref[...], attn, wo_ref, g1_ref[...], n2g_ref[...], sc2_ref[...], sh2_ref[...])

    def conv(up, c):
        out = fcb_ref[c]
        cw = fcw_ref[c]
        for j in range(kconv - 1):
            out = out + fbuf_ref[j, :, c * ck:(c + 1) * ck] * cw[j:j + 1, :]
        up_o[:, c * ck:(c + 1) * ck] = up
        return out + up * cw[kconv - 1:kconv, :]

    acc = jnp.zeros(x1.shape, f32)
    for c in range(nc):
        val = conv(_dot(h2, wup_ref[c]), c)
        gt = conv(_dot(h2, wup_ref[nc + c]), nc + c)
        acc = acc + _dot((jax.nn.gelu(gt) * val).astype(bf16), wdn_ref[c])
    y_o[...] = x1 + g2_ref[...] * acc


def _spost_call(x, lru, olat, g1, sh2, sc2, g2, fbuf, p, n_heads):
    nb, d = x.shape
    nc, ck, _ = p["wdn"].shape
    return pl.pallas_call(
        functools.partial(_spost_kernel, n_heads=n_heads),
        out_shape=[jax.ShapeDtypeStruct((nb, d), f32), jax.ShapeDtypeStruct((nb, 2 * nc * ck), f32)],
        compiler_params=pltpu.CompilerParams(vmem_limit_bytes=VMEM_LIMIT),
        name="sample_ffn",
    )(x, lru, olat, g1, sh2, sc2, g2, p["mog"], p["n2g"], p["wv"], p["wo"], p["wup"], p["fcw"], p["fcb"], p["wdn"], fbuf)


def _take_cols(wmat, idx):
    padded = jnp.concatenate([wmat, jnp.zeros(wmat.shape[:-1] + (1,), wmat.dtype)], axis=-1)
    return jnp.take(padded, jnp.asarray(np.where(idx < 0, wmat.shape[-1], idx)), axis=-1)


def _block_diag_groups(wh):
    nh, hd, _ = wh.shape
    per = MXU_DIM // hd
    groups = []
    for g in range(nh // per):
        blk = jnp.zeros((MXU_DIM, MXU_DIM), wh.dtype)
        for j in range(per):
            blk = lax.dynamic_update_slice(blk, wh[g * per + j], (j * hd, j * hd))
        groups.append(blk)
    return jnp.stack(groups).astype(bf16)


def _prep_layer(l, n_heads, scale, w_in, lru_conv_w, lru_conv_b, lru_w_a, lru_b_a, lru_w_x, lru_b_x, lru_lambda,
                norm1_g, q_lora_norm_g, w_uq, q_nope_norm_g, q_rope_norm_g, kv_lora_norm_g, k_rope_norm_g, w_ukv,
                k_nope_norm_g, lru_out_norm_g, mla_out_norm_g, w_o, norm2_g, w_up, ffn_conv_w, ffn_conv_b, w_down):
    src = _h128_src()
    w = lru_lambda.shape[-1]
    ql = q_lora_norm_g.shape[-1]
    kl = kv_lora_norm_g.shape[-1]
    qk = _NOPE + _ROPE
    vd = w_ukv.shape[-1] // n_heads - _NOPE
    row = lambda v: v.reshape(1, -1).astype(f32)

    kr_src = np.where(src >= _NOPE, src - _NOPE, -1)
    win = w_in[l]
    base = 2 * w + ql + kl
    win_ext = jnp.concatenate([win[:, :base], _take_cols(win[:, base:], kr_src)], axis=-1).astype(bf16)

    q_idx = np.concatenate([np.where(src >= 0, src + h * qk, -1) for h in range(n_heads)])
    wuq = _take_cols(w_uq[l], q_idx).astype(bf16)
    k_src = np.where((src >= 0) & (src < _NOPE), src, -1)
    k_idx = np.concatenate([np.where(k_src >= 0, k_src + h * (_NOPE + vd), -1) for h in range(n_heads)])
    wk = _take_cols(w_ukv[l], k_idx).astype(bf16)
    v_idx = []
    for h in range(n_heads):
        slab = np.full((LANES,), -1, np.int64)
        off = (h % 2) * vd
        slab[off:off + vd] = h * (_NOPE + vd) + _NOPE + np.arange(vd)
        v_idx.append(slab)
    wv = _take_cols(w_ukv[l], np.concatenate(v_idx)).astype(bf16)
    wkt_idx = np.concatenate([h * (_NOPE + vd) + np.arange(_NOPE) for h in range(n_heads)])
    wkt = jnp.take(w_ukv[l], jnp.asarray(wkt_idx), axis=-1).T.astype(bf16)

    nope_tab = lambda g: _take_cols(g.reshape(1, -1), k_src)
    rope_tab = lambda g: _take_cols(g.reshape(1, -1), kr_src)
    mn = jnp.asarray((k_src >= 0).astype(np.float32)).reshape(1, LANES)
    mr = jnp.asarray((kr_src >= 0).astype(np.float32)).reshape(1, LANES)
    gq = (nope_tab(q_nope_norm_g[l]) + rope_tab(q_rope_norm_g[l])) * scale

    dff = w_down.shape[1]
    ck = MXU_DIM
    nc = dff // ck
    wup = w_up[l].reshape(w_up.shape[1], 2 * nc, ck).transpose(1, 0, 2).astype(bf16)
    fcw = ffn_conv_w[l].reshape(-1, 2 * nc, ck).transpose(1, 0, 2).astype(f32)
    fcb = ffn_conv_b[l].reshape(2 * nc, 1, ck).astype(f32)
    wdn = w_down[l].reshape(nc, ck, -1).astype(bf16)
    return dict(
        n1g=row(norm1_g[l]), win=win_ext, cw=lru_conv_w[l].astype(f32), cb=row(lru_conv_b[l]),
        wa=_block_diag_groups(lru_w_a[l]), ba=row(lru_b_a[l]), wx=_block_diag_groups(lru_w_x[l]), bx=row(lru_b_x[l]),
        lam=row(lru_lambda[l]), qlg=row(q_lora_norm_g[l]), wuq=wuq, gq=gq.astype(f32), mn=mn, mr=mr,
        kvg=row(kv_lora_norm_g[l]), gkr=rope_tab(k_rope_norm_g[l]).astype(f32), wk=wk,
        gk=nope_tab(k_nope_norm_g[l]).astype(f32), wv=wv, wkt=wkt, log=row(lru_out_norm_g[l]),
        mog=row(mla_out_norm_g[l]), wo=w_o[l].astype(bf16), n2g=row(norm2_g[l]), wup=wup, fcw=fcw, fcb=fcb, wdn=wdn)


def _rope_tables(pos):
    half = _ROPE // 2
    inv = ROPE_THETA ** (-jnp.arange(0, _ROPE, 2, dtype=f32) / _ROPE)
    ang = pos.astype(f32)[:, None] * inv[None, :]
    cos, sin = jnp.cos(ang), jnp.sin(ang)
    n = pos.shape[0]
    cc = jnp.zeros((n, LANES), f32)
    cc = cc.at[:, 0:_R1_LO].set(1.0).at[:, _R1_LO + half:_R1_LO + half + 16].set(1.0)
    cc = cc.at[:, _R1_LO:_R1_LO + half].set(cos).at[:, _R2_LO:_R2_LO + half].set(cos)
    ss = jnp.zeros((n, LANES), f32)
    ss = ss.at[:, _R1_LO:_R1_LO + half].set(-sin).at[:, _R2_LO:_R2_LO + half].set(sin)
    return cc, ss


def kernel(x_prompt, x_sample, cache_kv_latent, cache_k_rope, state_lru_h, state_lru_conv, state_ffn_conv,
           page_table, c_prompt, c_sample, w_ada, b_ada, norm1_g, w_in, lru_conv_w, lru_conv_b, lru_w_a, lru_b_a,
           lru_w_x, lru_b_x, lru_lambda, q_lora_norm_g, w_uq, q_nope_norm_g, q_rope_norm_g, kv_lora_norm_g,
           k_rope_norm_g, w_ukv, k_nope_norm_g, lru_out_norm_g, mla_out_norm_g, w_o, norm2_g, w_up, ffn_conv_w,
           ffn_conv_b, w_down):
    b, s, d = x_prompt.shape
    nb, ds, _ = x_sample.shape
    depth = w_in.shape[0]
    assert ds == 1 and q_nope_norm_g.shape[-1] == _NOPE and q_rope_norm_g.shape[-1] == _ROPE
    n_heads = w_uq.shape[-1] // (_NOPE + _ROPE)
    scale = float(_NOPE + _ROPE) ** -0.5
    npg = page_table.shape[1]
    n_past = npg * cache_kv_latent.shape[2]
    ts = 32
    tq = min(512, s)
    tk = min(256, s)
    tm = min(512, s)
    ppc = 16 if npg % 16 == 0 else 1
    hp = 16

    cc_p, ss_p = _rope_tables(jnp.arange(s))
    cc_s, ss_s = _rope_tables(n_past + jnp.arange(1))

    y_p = x_prompt
    y_s = x_sample.reshape(nb, d)
    c_all = jnp.concatenate([c_prompt, c_sample], axis=0)
    outs_p = [[] for _ in range(5)]
    outs_s = [[] for _ in range(5)]
    for l in range(depth):
        p = _prep_layer(l, n_heads, scale, w_in, lru_conv_w, lru_conv_b, lru_w_a, lru_b_a, lru_w_x, lru_b_x,
                        lru_lambda, norm1_g, q_lora_norm_g, w_uq, q_nope_norm_g, q_rope_norm_g, kv_lora_norm_g,
                        k_rope_norm_g, w_ukv, k_nope_norm_g, lru_out_norm_g, mla_out_norm_g, w_o, norm2_g, w_up,
                        ffn_conv_w, ffn_conv_b, w_down)
        mod = _mod_call(c_all, w_ada[l], b_ada[l])
        mp = [m_[:, None, :] for m_ in jnp.split(mod[:b], 6, axis=-1)]
        ms = jnp.split(mod[b:], 6, axis=-1)

        lru_p, q_p, k_p, v_p, kvlat_p, krope_p, hlast_p, xtail_p = _inproj_call(
            y_p, mp[0], mp[1], cc_p, ss_p, p, n_heads, ts)
        attn_p = _attn_call(q_p, k_p, v_p, p["mog"], n_heads, tq, tk)
        y_p, ftail_p = _ffn_call(y_p, lru_p, attn_p, mp[2], mp[3], mp[4], mp[5], p, tm)
        kc = lru_conv_w.shape[1]
        fk = ffn_conv_w.shape[1]
        for j, o in enumerate((kvlat_p, krope_p, hlast_p, xtail_p[:, 8 - (kc - 1):], ftail_p[:, 8 - (fk - 1):])):
            outs_p[j].append(o)

        cbuf = jnp.swapaxes(state_lru_conv[l], 0, 1)
        fbuf = jnp.swapaxes(state_ffn_conv[l], 0, 1)
        lru_s, xlru_s, hnew_s, qabs, qr, kvn_s, kr_s = _spre_call(
            y_s, ms[0], ms[1], cc_s, ss_s, cbuf, state_lru_h[l], p, n_heads, n_past == 0)
        pad_heads = lambda t: jnp.pad(jnp.swapaxes(t, 0, 1), ((0, 0), (0, hp - n_heads), (0, 0)))
        olat = _sattn_call(page_table, cache_kv_latent, cache_k_rope, l, p["wkt"], pad_heads(qabs), pad_heads(qr),
                           kvn_s[:, None, :], kr_s[:, None, :], n_heads, ppc)
        olat = jnp.swapaxes(olat[:, :n_heads], 0, 1)
        y_s, up_s = _spost_call(y_s, lru_s, olat, ms[2], ms[3], ms[4], ms[5], fbuf, p, n_heads)
        lru_conv_new = jnp.concatenate([state_lru_conv[l][:, 1:], xlru_s[:, None, :]], axis=1)
        ffn_conv_new = jnp.concatenate([state_ffn_conv[l][:, 1:], up_s[:, None, :]], axis=1)
        for j, o in enumerate((kvn_s[:, None, :], kr_s[:, None, :], hnew_s, lru_conv_new, ffn_conv_new)):
            outs_s[j].append(o)

    return (y_p, y_s.reshape(nb, 1, d), *[jnp.stack(o) for o in outs_p], *[jnp.stack(o) for o in outs_s])
```

```python
import functools

import numpy as np
import jax
import jax.numpy as jnp
from jax import lax
from jax.experimental import pallas as pl
from jax.experimental.pallas import tpu as pltpu

f32 = jnp.float32
bf16 = jnp.bfloat16

EPS = 1e-6
LRU_C = 8.0
ROPE_THETA = 10000.0
LANES = 128
MXU_DIM = 256
VMEM_LIMIT = 56 * 1024 * 1024

_NOPE, _ROPE = 64, 32
_R1_LO, _R2_LO = 48, 112


def _h128_src():
    src = np.full((LANES,), -1, np.int32)
    src[0:48] = np.arange(0, 48)
    src[48:64] = _NOPE + np.arange(0, 16)
    src[64:80] = np.arange(48, 64)
    src[112:128] = _NOPE + 16 + np.arange(0, 16)
    return src


def _dot(a, b):
    return jnp.dot(a, b, preferred_element_type=f32)


def _dot_nt(a, b):
    return lax.dot_general(a, b, (((1,), (1,)), ((), ())), preferred_element_type=f32)


def _rms(x, g):
    ms = jnp.mean(x * x, axis=-1, keepdims=True)
    return x * lax.rsqrt(ms + EPS) * g


def _neg_expm1_2x(y):
    t = jnp.tanh(y)
    return -2.0 * t / (1.0 - t)


def _lru_gates(x_conv, wa_ref, ba, wx_ref, bx, lam):
    xb = x_conv.astype(bf16)
    ng = wa_ref.shape[0]
    ra = jnp.concatenate([_dot(xb[:, g * MXU_DIM:(g + 1) * MXU_DIM], wa_ref[g]) for g in range(ng)], axis=-1) + ba
    ia = jnp.concatenate([_dot(xb[:, g * MXU_DIM:(g + 1) * MXU_DIM], wx_ref[g]) for g in range(ng)], axis=-1) + bx
    r = jax.nn.sigmoid(ra)
    ig = jax.nn.sigmoid(ia)
    log_a = (-LRU_C) * r * jax.nn.softplus(-lam)
    a = jnp.exp(log_a)
    mult = jnp.sqrt(_neg_expm1_2x(log_a))
    return a, mult, ig * x_conv


def _q_heads(q, gq, mn, mr, cc, ss, n_heads, to3d):
    outs = []
    for h in range(n_heads):
        xh = q[:, h * LANES:(h + 1) * LANES]
        x2 = xh * xh
        ssn = jnp.sum(x2 * mn, axis=-1, keepdims=True)
        ssr = jnp.sum(x2 * mr, axis=-1, keepdims=True)
        inv = lax.rsqrt(ssn * (1.0 / _NOPE) + EPS) * mn + lax.rsqrt(ssr * (1.0 / _ROPE) + EPS) * mr
        xn = xh * inv * gq
        outs.append(to3d(xn) * cc + to3d(pltpu.roll(xn, LANES // 2, 1)) * ss)
    return outs


def _krope(x, gkr, cc, ss, to3d):
    ssq = jnp.sum(x * x, axis=-1, keepdims=True)
    xn = x * lax.rsqrt(ssq * (1.0 / _ROPE) + EPS) * gkr
    return to3d(xn) * cc + to3d(pltpu.roll(xn, LANES // 2, 1)) * ss


def _mod_kernel(c_ref, w_ref, b_ref, o_ref):
    c = c_ref[...]
    sc = (c * jax.nn.sigmoid(c)).astype(bf16)
    o_ref[...] = _dot(sc, w_ref[...].astype(bf16)) + b_ref[...]


def _mod_call(c_all, w_ada, b_ada):
    m, d = c_all.shape
    n = w_ada.shape[1]
    tn = 1536 if n % 1536 == 0 else n
    return pl.pallas_call(
        _mod_kernel,
        grid=(n // tn,),
        in_specs=[pl.BlockSpec((m, d), lambda j: (0, 0)),
                  pl.BlockSpec((d, tn), lambda j: (0, j)),
                  pl.BlockSpec((1, tn), lambda j: (0, j))],
        out_specs=pl.BlockSpec((m, tn), lambda j: (0, j)),
        out_shape=jax.ShapeDtypeStruct((m, n), f32),
        compiler_params=pltpu.CompilerParams(dimension_semantics=("arbitrary",), vmem_limit_bytes=VMEM_LIMIT),
        name="adaln_mod",
    )(c_all, w_ada, b_ada.reshape(1, n))


def _inproj_kernel(x_ref, sh_ref, sc_ref, cc_ref, ss_ref, n1g_ref, win_ref, cw_ref, cb_ref,
                   wa_ref, ba_ref, wx_ref, bx_ref, lam_ref, qlg_ref, wuq_ref, gq_ref, mn_ref, mr_ref,
                   kvg_ref, gkr_ref, wk_ref, gk_ref, wv_ref, log_ref,
                   lru_o, q_o, k_o, v_o, kvlat_o, krope_o, hlast_o, xtail_o,
                   xprev_s, a_s, u_s, hs_s, h_s, *, n_heads):
    i = pl.program_id(0)
    nb, ts, d = x_ref.shape
    m = nb * ts
    w = lam_ref.shape[-1]
    ql = qlg_ref.shape[-1]
    kl = kvg_ref.shape[-1]
    kconv = cw_ref.shape[0]
    nlc = w // LANES

    @pl.when(i == 0)
    def _():
        xprev_s[...] = jnp.zeros_like(xprev_s)
        h_s[...] = jnp.zeros_like(h_s)

    def to3d(v):
        return v.reshape(nb, ts, v.shape[-1])

    x = x_ref[...]
    h = _rms(x, n1g_ref[...]) * (1.0 + sc_ref[...]) + sh_ref[...]
    z = _dot(h.reshape(m, d).astype(bf16), win_ref[...])
    x_lru = z[:, 0:w]
    g_lru = z[:, w:2 * w]
    q_lat = z[:, 2 * w:2 * w + ql]
    kv_lat = z[:, 2 * w + ql:2 * w + ql + kl]
    kr_pre = z[:, 2 * w + ql + kl:2 * w + ql + kl + LANES]

    t_idx = lax.broadcasted_iota(jnp.int32, (m, 1), 0) & (ts - 1)
    xp = xprev_s[...]
    x_conv = cb_ref[...]
    for j in range(kconv - 1):
        k = kconv - 1 - j
        sh = jnp.where(t_idx >= k, pltpu.roll(x_lru, k, 0), pltpu.roll(xp, (m + k - ts) % m, 0))
        x_conv = x_conv + sh * cw_ref[j:j + 1, :]
    x_conv = x_conv + x_lru * cw_ref[kconv - 1:kconv, :]
    xprev_s[...] = x_lru
    xtail_o[...] = to3d(x_lru)[:, ts - 8:, :]

    a, mult, ux = _lru_gates(x_conv, wa_ref, ba_ref[...], wx_ref, bx_ref[...], lam_ref[...])
    mult = jnp.where(jnp.logical_and(t_idx == 0, i == 0), 1.0, mult)
    u = mult * ux
    for j in range(nlc):
        a_s[j] = a[:, j * LANES:(j + 1) * LANES]
        u_s[j] = u[:, j * LANES:(j + 1) * LANES]

    def scan_step(t, hc):
        out = []
        for j in range(nlc):
            hj = a_s[j, pl.ds(t, nb, stride=ts), :] * hc[j] + u_s[j, pl.ds(t, nb, stride=ts), :]
            hs_s[j, pl.ds(t, nb, stride=ts), :] = hj
            out.append(hj)
        return tuple(out)

    hc = lax.fori_loop(0, ts, scan_step, tuple(h_s[j] for j in range(nlc)), unroll=8)
    for j in range(nlc):
        h_s[j] = hc[j]
    hlast_o[...] = jnp.concatenate(list(hc), axis=-1)
    hs = jnp.concatenate([hs_s[j] for j in range(nlc)], axis=-1)
    lru_out = hs * jax.nn.gelu(g_lru)
    lru_o[...] = to3d(_rms(lru_out, log_ref[...]).astype(bf16))

    cc = cc_ref[...][None]
    ss = ss_ref[...][None]
    qn = _rms(q_lat, qlg_ref[...]).astype(bf16)
    q = _dot(qn, wuq_ref[...])
    for hd, o in enumerate(_q_heads(q, gq_ref[...], mn_ref[...], mr_ref[...], cc, ss, n_heads, to3d)):
        q_o[:, :, hd * LANES:(hd + 1) * LANES] = o.astype(bf16)

    kvn = _rms(kv_lat, kvg_ref[...])
    kvlat_o[...] = to3d(kvn)
    kr = _krope(kr_pre, gkr_ref[...], cc, ss, to3d)
    krope_o[...] = jnp.concatenate([kr[:, :, _R1_LO:_R1_LO + 16], kr[:, :, _R2_LO:_R2_LO + 16]], axis=-1)
    kvb = kvn.astype(bf16)
    kk = _dot(kvb, wk_ref[...])
    v_o[...] = to3d(_dot(kvb, wv_ref[...]).astype(bf16))
    gk = gk_ref[...]
    for hd in range(n_heads):
        kh = kk[:, hd * LANES:(hd + 1) * LANES]
        ssq = jnp.sum(kh * kh, axis=-1, keepdims=True)
        khn = kh * lax.rsqrt(ssq * (1.0 / _NOPE) + EPS) * gk
        k_o[:, :, hd * LANES:(hd + 1) * LANES] = (to3d(khn) + kr).astype(bf16)


def _const_spec(shape):
    nd = len(shape)
    return pl.BlockSpec(shape, lambda *_: (0,) * nd, pipeline_mode=pl.Buffered(1))


def _inproj_call(x, sh1, sc1, cc, ss, p, n_heads, ts):
    b, s, d = x.shape
    w = p["lam"].shape[-1]
    kl = p["kvg"].shape[-1]
    hw = n_heads * LANES
    m = b * ts
    consts = [p["n1g"], p["win"], p["cw"], p["cb"], p["wa"], p["ba"], p["wx"], p["bx"], p["lam"], p["qlg"],
              p["wuq"], p["gq"], p["mn"], p["mr"], p["kvg"], p["gkr"], p["wk"], p["gk"], p["wv"], p["log"]]
    in_specs = [pl.BlockSpec((b, ts, d), lambda i: (0, i, 0)),
                _const_spec(sh1.shape), _const_spec(sc1.shape),
                pl.BlockSpec((ts, LANES), lambda i: (i, 0)),
                pl.BlockSpec((ts, LANES), lambda i: (i, 0))] + [_const_spec(c.shape) for c in consts]

    def tile(n, dt):
        return pl.BlockSpec((b, ts, n), lambda i: (0, i, 0)), jax.ShapeDtypeStruct((b, s, n), dt)

    outs = [tile(w, bf16), tile(hw, bf16), tile(hw, bf16), tile(hw, bf16), tile(kl, f32), tile(_ROPE, f32),
            (pl.BlockSpec((b, w), lambda i: (0, 0)), jax.ShapeDtypeStruct((b, w), f32)),
            (pl.BlockSpec((b, 8, w), lambda i: (0, 0, 0)), jax.ShapeDtypeStruct((b, 8, w), f32))]
    nlc = w // LANES
    return pl.pallas_call(
        functools.partial(_inproj_kernel, n_heads=n_heads),
        grid=(s // ts,),
        in_specs=in_specs,
        out_specs=[o[0] for o in outs],
        out_shape=[o[1] for o in outs],
        scratch_shapes=[pltpu.VMEM((m, w), f32), pltpu.VMEM((nlc, m, LANES), f32), pltpu.VMEM((nlc, m, LANES), f32),
                        pltpu.VMEM((nlc, m, LANES), f32), pltpu.VMEM((nlc, b, LANES), f32)],
        compiler_params=pltpu.CompilerParams(dimension_semantics=("arbitrary",), vmem_limit_bytes=VMEM_LIMIT),
        name="prompt_inproj",
    )(x, sh1, sc1, cc, ss, *consts)


def _attn_kernel(q_ref, k_ref, v_ref, g_ref, o_ref, *, n_heads, tk):
    qi = pl.program_id(1)
    tq = q_ref.shape[1]
    nmask = tq // tk
    n_full = qi * nmask
    row = lax.broadcasted_iota(jnp.int32, (tq, tk), 0)
    col = lax.broadcasted_iota(jnp.int32, (tq, tk), 1)
    pair_out = []
    for hd in range(n_heads):
        hs = slice(hd * LANES, (hd + 1) * LANES)
        q = q_ref[0, :, hs]

        def step(j, carry, masked_off=None):
            mx, l, acc = carry
            start = pl.multiple_of(j * tk, tk)
            k = k_ref[0, pl.ds(start, tk), hs]
            v = v_ref[0, pl.ds(start, tk), hs]
            s = _dot_nt(q, k)
            if masked_off is not None:
                s = jnp.where(col + masked_off * tk <= row, s, -1e30)
            m_new = jnp.maximum(mx, jnp.max(s, axis=-1, keepdims=True))
            alpha = jnp.exp(mx - m_new)
            pm = jnp.exp(s - m_new)
            l = alpha * l + jnp.sum(pm, axis=-1, keepdims=True)
            acc = alpha * acc + _dot(pm.astype(bf16), v)
            return m_new, l, acc

        carry = (jnp.full((tq, 1), -1e30, f32), jnp.zeros((tq, 1), f32), jnp.zeros((tq, LANES), f32))
        carry = lax.fori_loop(0, n_full, step, carry)
        for jm in range(nmask):
            carry = step(n_full + jm, carry, masked_off=jm)
        _, l, acc = carry
        pair_out.append(acc / l)
    o = jnp.concatenate([pair_out[2 * pp] + pair_out[2 * pp + 1] for pp in range(n_heads // 2)], axis=-1)
    o_ref[0] = _rms(o, g_ref[...]).astype(bf16)


def _attn_call(q, k, v, g, n_heads, tq, tk):
    b, s, hw = q.shape
    wout = g.shape[-1]
    return pl.pallas_call(
        functools.partial(_attn_kernel, n_heads=n_heads, tk=tk),
        grid=(b, s // tq),
        in_specs=[pl.BlockSpec((1, tq, hw), lambda bi, qi: (bi, qi, 0)),
                  pl.BlockSpec((1, s, hw), lambda bi, qi: (bi, 0, 0)),
                  pl.BlockSpec((1, s, hw), lambda bi, qi: (bi, 0, 0)),
                  pl.BlockSpec((1, wout), lambda bi, qi: (0, 0))],
        out_specs=pl.BlockSpec((1, tq, wout), lambda bi, qi: (bi, qi, 0)),
        out_shape=jax.ShapeDtypeStruct((b, s, wout), bf16),
        compiler_params=pltpu.CompilerParams(dimension_semantics=("arbitrary", "arbitrary"),
                                             vmem_limit_bytes=VMEM_LIMIT),
        name="prompt_attn",
    )(q, k, v, g)


def _mix_and_norm2(x, lru, attn, wo_ref, g1, n2g, sc2, sh2):
    w = lru.shape[-1]
    mixed = _dot(lru, wo_ref[pl.ds(0, w), :]) + _dot(attn, wo_ref[pl.ds(w, attn.shape[-1]), :])
    x1 = x + g1 * mixed
    h2 = (_rms(x1, n2g) * (1.0 + sc2) + sh2).astype(bf16)
    return x1, h2


def _ffn_kernel(x_ref, lru_ref, attn_ref, g1_ref, sh2_ref, sc2_ref, g2_ref, n2g_ref, wo_ref, wup_ref,
                fcw_ref, fcb_ref, wdn_ref, y_o, tail_o, tail_s):
    si = pl.program_id(1)
    tm = x_ref.shape[1]
    nc = wdn_ref.shape[0]
    ck = wdn_ref.shape[1]
    kconv = fcw_ref.shape[1]

    @pl.when(si == 0)
    def _():
        tail_s[...] = jnp.zeros_like(tail_s)

    x1, h2 = _mix_and_norm2(x_ref[0], lru_ref[0], attn_ref[0], wo_ref, g1_ref[0], n2g_ref[...], sc2_ref[0], sh2_ref[0])
    r8 = lax.broadcasted_iota(jnp.int32, (8, 1), 0)

    def conv(up, c):
        tail = tail_s[c]
        first = up[0:8]
        out = fcb_ref[c]
        cw = fcw_ref[c]
        for j in range(kconv - 1):
            k = kconv - 1 - j
            head = jnp.where(r8 < k, pltpu.roll(tail, k, 0), pltpu.roll(first, k, 0))
            sh = jnp.concatenate([head, pltpu.roll(up, k, 0)[8:]], axis=0)
            out = out + sh * cw[j:j + 1, :]
        out = out + up * cw[kconv - 1:kconv, :]
        tail_s[c] = up[tm - 8:]
        tail_o[0, :, c * ck:(c + 1) * ck] = up[tm - 8:]
        return out

    acc = jnp.zeros((tm, x1.shape[-1]), f32)
    for c in range(nc):
        val = conv(_dot(h2, wup_ref[c]), c)
        gt = conv(_dot(h2, wup_ref[nc + c]), nc + c)
        act = (jax.nn.gelu(gt) * val).astype(bf16)
        acc = acc + _dot(act, wdn_ref[c])
    y_o[0] = x1 + g2_ref[0] * acc


def _ffn_call(x, lru, attn, g1, sh2, sc2, g2, p, tm):
    b, s, d = x.shape
    w = lru.shape[-1]
    nc, ck, _ = p["wdn"].shape
    row = lambda n: pl.BlockSpec((1, tm, n), lambda bi, si: (bi, si, 0))
    modspec = pl.BlockSpec((1, 1, d), lambda bi, si: (bi, 0, 0))
    consts = [p["n2g"], p["wo"], p["wup"], p["fcw"], p["fcb"], p["wdn"]]
    return pl.pallas_call(
        _ffn_kernel,
        grid=(b, s // tm),
        in_specs=[row(d), row(w), row(attn.shape[-1]), modspec, modspec, modspec, modspec]
        + [_const_spec(c.shape) for c in consts],
        out_specs=[row(d), pl.BlockSpec((1, 8, 2 * nc * ck), lambda bi, si: (bi, 0, 0))],
        out_shape=[jax.ShapeDtypeStruct((b, s, d), f32), jax.ShapeDtypeStruct((b, 8, 2 * nc * ck), f32)],
        scratch_shapes=[pltpu.VMEM((2 * nc, 8, ck), f32)],
        compiler_params=pltpu.CompilerParams(dimension_semantics=("arbitrary", "arbitrary"),
                                             vmem_limit_bytes=VMEM_LIMIT),
        name="prompt_ffn",
    )(x, lru, attn, g1, sh2, sc2, g2, *consts)


def _spre_kernel(x_ref, sh_ref, sc_ref, cc_ref, ss_ref, n1g_ref, win_ref, cbuf_ref, cw_ref, cb_ref,
                 wa_ref, ba_ref, wx_ref, bx_ref, lam_ref, h0_ref, qlg_ref, wuq_ref, gq_ref, mn_ref, mr_ref,
                 kvg_ref, gkr_ref, wk_ref, gk_ref, log_ref,
                 lru_o, xlru_o, hnew_o, qabs_o, qr_o, kvn_o, kr_o, *, n_heads, first_pos):
    w = lam_ref.shape[-1]
    ql = qlg_ref.shape[-1]
    kl = kvg_ref.shape[-1]
    kconv = cw_ref.shape[0]
    ident = lambda v: v

    h = _rms(x_ref[...], n1g_ref[...]) * (1.0 + sc_ref[...]) + sh_ref[...]
    z = _dot(h.astype(bf16), win_ref[...])
    x_lru = z[:, 0:w]
    g_lru = z[:, w:2 * w]
    q_lat = z[:, 2 * w:2 * w + ql]
    kv_lat = z[:, 2 * w + ql:2 * w + ql + kl]
    kr_pre = z[:, 2 * w + ql + kl:2 * w + ql + kl + LANES]

    x_conv = cb_ref[...]
    for j in range(kconv - 1):
        x_conv = x_conv + cbuf_ref[j] * cw_ref[j:j + 1, :]
    x_conv = x_conv + x_lru * cw_ref[kconv - 1:kconv, :]
    xlru_o[...] = x_lru
    a, mult, ux = _lru_gates(x_conv, wa_ref, ba_ref[...], wx_ref, bx_ref[...], lam_ref[...])
    if first_pos:
        mult = jnp.ones_like(mult)
    hn = a * h0_ref[...] + mult * ux
    hnew_o[...] = hn
    lru_o[...] = _rms(hn * jax.nn.gelu(g_lru), log_ref[...]).astype(bf16)

    cc = cc_ref[...]
    ss = ss_ref[...]
    qn = _rms(q_lat, qlg_ref[...]).astype(bf16)
    q = _dot(qn, wuq_ref[...])
    gk = gk_ref[...]
    for hd, o in enumerate(_q_heads(q, gq_ref[...], mn_ref[...], mr_ref[...], cc, ss, n_heads, ident)):
        wkh = wk_ref[:, hd * LANES:(hd + 1) * LANES]
        qabs_o[hd] = _dot_nt((o * gk).astype(bf16), wkh).astype(bf16)
        qr_o[hd] = jnp.concatenate([o[:, _R1_LO:_R1_LO + 16], o[:, _R2_LO:_R2_LO + 16]], axis=-1).astype(bf16)

    kvn_o[...] = _rms(kv_lat, kvg_ref[...])
    kr = _krope(kr_pre, gkr_ref[...], cc, ss, ident)
    kr_o[...] = jnp.concatenate([kr[:, _R1_LO:_R1_LO + 16], kr[:, _R2_LO:_R2_LO + 16]], axis=-1)


def _spre_call(x, sh1, sc1, cc, ss, cbuf, h0, p, n_heads, first_pos):
    nb, d = x.shape
    w = p["lam"].shape[-1]
    kl = p["kvg"].shape[-1]
    out_shape = [jax.ShapeDtypeStruct((nb, w), bf16), jax.ShapeDtypeStruct((nb, w), f32),
                 jax.ShapeDtypeStruct((nb, w), f32), jax.ShapeDtypeStruct((n_heads, nb, kl), bf16),
                 jax.ShapeDtypeStruct((n_heads, nb, _ROPE), bf16), jax.ShapeDtypeStruct((nb, kl), f32),
                 jax.ShapeDtypeStruct((nb, _ROPE), f32)]
    return pl.pallas_call(
        functools.partial(_spre_kernel, n_heads=n_heads, first_pos=first_pos),
        out_shape=out_shape,
        compiler_params=pltpu.CompilerParams(vmem_limit_bytes=VMEM_LIMIT),
        name="sample_inproj",
    )(x, sh1, sc1, cc, ss, p["n1g"], p["win"], cbuf, p["cw"], p["cb"], p["wa"], p["ba"], p["wx"], p["bx"],
      p["lam"], h0, p["qlg"], p["wuq"], p["gq"], p["mn"], p["mr"], p["kvg"], p["gkr"], p["wk"], p["gk"], p["log"])


def _sattn_kernel(pt_ref, lat_hbm, krt_hbm, wkt_ref, qabs_ref, qr_ref, latn_ref, krn_ref, o_ref,
                  lat_buf, kr_buf, lhs_s, latb_s, s_s, sems, *, layer, n_heads, ppsub):
    b = pl.program_id(0)
    nseq = pl.num_programs(0)
    _, npg, page, kl = lat_buf.shape
    nk = wkt_ref.shape[0]
    hp = qabs_ref.shape[0]
    n_past = npg * page
    tk = ppsub * page
    slot = lax.rem(b, 2)

    def page_copies(seq, slot_, pg):
        src = pt_ref[seq * npg + pg]
        return (pltpu.make_async_copy(lat_hbm.at[layer, src], lat_buf.at[slot_, pg], sems.at[0, slot_]),
                pltpu.make_async_copy(krt_hbm.at[layer, src], kr_buf.at[slot_, pg], sems.at[1, slot_]))

    def fetch(seq, slot_):
        def body(pg, c):
            for cp in page_copies(seq, slot_, pg):
                cp.start()
            return c
        lax.fori_loop(0, npg, body, 0)

    @pl.when(b == 0)
    def _():
        fetch(0, 0)
        lhs_s[0:nk, :] = wkt_ref[...]

    @pl.when(b + 1 < nseq)
    def _():
        fetch(b + 1, 1 - slot)

    lhs_s[nk:nk + hp, :] = qabs_ref[...]

    def wait_body(pg, c):
        for cp in page_copies(b, slot, pg):
            cp.wait()
        return c
    lax.fori_loop(0, npg, wait_body, 0)

    def scores(latb, krt):
        n = latb.shape[0]
        big = _dot_nt(lhs_s[...], latb)
        knt = big[0:nk]
        ssq = jnp.sum((knt * knt).reshape(n_heads, _NOPE, n), axis=1)
        rs = lax.rsqrt(ssq * (1.0 / _NOPE) + EPS)
        sr = _dot(qr_ref[...], krt)
        s8 = big[nk:nk + n_heads] * rs + sr[0:n_heads]
        return jnp.concatenate([s8, jnp.zeros((hp - n_heads, n), f32)], axis=0)

    def sub(j, c):
        p0 = pl.multiple_of(j * ppsub, ppsub)
        k0 = pl.multiple_of(j * tk, tk)
        latb = lat_buf[slot, pl.ds(p0, ppsub)].reshape(tk, kl).astype(bf16)
        krt = jnp.concatenate([kr_buf[slot, p0 + i] for i in range(ppsub)], axis=-1).astype(bf16)
        latb_s[pl.ds(k0, tk), :] = latb
        s_s[:, pl.ds(k0, tk)] = scores(latb, krt)
        return c

    lax.fori_loop(0, npg // ppsub, sub, 0, unroll=2)

    latn = jnp.broadcast_to(latn_ref[...], (page, kl)).astype(bf16)
    krn = jnp.broadcast_to(krn_ref[...], (krn_ref.shape[0], page)).astype(bf16)
    s_new = scores(latn, krn)
    s_s[:, n_past:n_past + page] = jnp.where(lax.broadcasted_iota(jnp.int32, s_new.shape, 1) == 0, s_new, -1e30)
    latb_s[n_past:n_past + page, :] = latn

    s = s_s[...]
    pm = jnp.exp(s - jnp.max(s, axis=-1, keepdims=True))
    l = jnp.sum(pm, axis=-1, keepdims=True)
    o_ref[...] = _dot(pm.astype(bf16), latb_s[...]) / l


def _sattn_call(page_table, cache_lat, cache_krt, layer, wkt, qabs, qr, latn, krn, n_heads, ppsub):
    nb, npg = page_table.shape
    _, _, page, kl = cache_lat.shape
    rd = cache_krt.shape[2]
    hp = qabs.shape[1]
    nk = wkt.shape[0]
    nkeys = npg * page + page
    grid_spec = pltpu.PrefetchScalarGridSpec(
        num_scalar_prefetch=1,
        grid=(nb,),
        in_specs=[pl.BlockSpec(memory_space=pl.ANY),
                  pl.BlockSpec(memory_space=pl.ANY),
                  pl.BlockSpec((nk, kl), lambda bi, pt: (0, 0)),
                  pl.BlockSpec((None, hp, kl), lambda bi, pt: (bi, 0, 0)),
                  pl.BlockSpec((None, hp, rd), lambda bi, pt: (bi, 0, 0)),
                  pl.BlockSpec((None, 1, kl), lambda bi, pt: (bi, 0, 0)),
                  pl.BlockSpec((None, rd, 1), lambda bi, pt: (bi, 0, 0))],
        out_specs=pl.BlockSpec((None, hp, kl), lambda bi, pt: (bi, 0, 0)),
        scratch_shapes=[pltpu.VMEM((2, npg, page, kl), f32), pltpu.VMEM((2, npg, rd, page), f32),
                        pltpu.VMEM((nk + hp, kl), bf16), pltpu.VMEM((nkeys, kl), bf16),
                        pltpu.VMEM((hp, nkeys), f32), pltpu.SemaphoreType.DMA((2, 2))])
    return pl.pallas_call(
        functools.partial(_sattn_kernel, layer=layer, n_heads=n_heads, ppsub=ppsub),
        grid_spec=grid_spec,
        out_shape=jax.ShapeDtypeStruct((nb, hp, kl), f32),
        compiler_params=pltpu.CompilerParams(dimension_semantics=("arbitrary",), vmem_limit_bytes=VMEM_LIMIT),
        name="sample_attn",
    )(page_table.reshape(-1), cache_lat, cache_krt, wkt, qabs, qr, latn, krn)


def _spost_kernel(x_ref, lru_ref, olat_ref, g1_ref, sh2_ref, sc2_ref, g2_ref, mog_ref, n2g_ref, wv_ref, wo_ref,
                  wup_ref, fcw_ref, fcb_ref, wdn_ref, fbuf_ref, y_o, up_o, *, n_heads):
    nc = wdn_ref.shape[0]
    ck = wdn_ref.shape[1]
    kconv = fcw_ref.shape[1]
    heads = [_dot(olat_ref[hd].astype(bf16), wv_ref[:, hd * LANES:(hd + 1) * LANES]) for hd in range(n_heads)]
    attn = jnp.concatenate([heads[2 * pp] + heads[2 * pp + 1] for pp in range(n_heads // 2)], axis=-1)
    attn = _rms(attn, mog_ref[...]).astype(bf16)
    x1, h2 = _mix_and_norm2(x_ref[...], lru_ref[...], attn, wo_ref, g1_ref[...], n2g_ref[...], sc2_ref[...], sh2_ref[...])

    def conv(up, c):
        out = fcb_ref[c]
        cw = fcw_ref[c]
        for j in range(kconv - 1):
            out = out + fbuf_ref[j, :, c * ck:(c + 1) * ck] * cw[j:j + 1, :]
        up_o[:, c * ck:(c + 1) * ck] = up
        return out + up * cw[kconv - 1:kconv, :]

    acc = jnp.zeros(x1.shape, f32)
    for c in range(nc):
        val = conv(_dot(h2, wup_ref[c]), c)
        gt = conv(_dot(h2, wup_ref[nc + c]), nc + c)
        acc = acc + _dot((jax.nn.gelu(gt) * val).astype(bf16), wdn_ref[c])
    y_o[...] = x1 + g2_ref[...] * acc


def _spost_call(x, lru, olat, g1, sh2, sc2, g2, fbuf, p, n_heads):
    nb, d = x.shape
    nc, ck, _ = p["wdn"].shape
    return pl.pallas_call(
        functools.partial(_spost_kernel, n_heads=n_heads),
        out_shape=[jax.ShapeDtypeStruct((nb, d), f32), jax.ShapeDtypeStruct((nb, 2 * nc * ck), f32)],
        compiler_params=pltpu.CompilerParams(vmem_limit_bytes=VMEM_LIMIT),
        name="sample_ffn",
    )(x, lru, olat, g1, sh2, sc2, g2, p["mog"], p["n2g"], p["wv"], p["wo"], p["wup"], p["fcw"], p["fcb"], p["wdn"], fbuf)


def _take_cols(wmat, idx):
    padded = jnp.concatenate([wmat, jnp.zeros(wmat.shape[:-1] + (1,), wmat.dtype)], axis=-1)
    return jnp.take(padded, jnp.asarray(np.where(idx < 0, wmat.shape[-1], idx)), axis=-1)


def _block_diag_groups(wh):
    nh, hd, _ = wh.shape
    per = MXU_DIM // hd
    groups = []
    for g in range(nh // per):
        blk = jnp.zeros((MXU_DIM, MXU_DIM), wh.dtype)
        for j in range(per):
            blk = lax.dynamic_update_slice(blk, wh[g * per + j], (j * hd, j * hd))
        groups.append(blk)
    return jnp.stack(groups).astype(bf16)


def _prep_layer(l, n_heads, scale, w_in, lru_conv_w, lru_conv_b, lru_w_a, lru_b_a, lru_w_x, lru_b_x, lru_lambda,
                norm1_g, q_lora_norm_g, w_uq, q_nope_norm_g, q_rope_norm_g, kv_lora_norm_g, k_rope_norm_g, w_ukv,
                k_nope_norm_g, lru_out_norm_g, mla_out_norm_g, w_o, norm2_g, w_up, ffn_conv_w, ffn_conv_b, w_down):
    src = _h128_src()
    w = lru_lambda.shape[-1]
    ql = q_lora_norm_g.shape[-1]
    kl = kv_lora_norm_g.shape[-1]
    qk = _NOPE + _ROPE
    vd = w_ukv.shape[-1] // n_heads - _NOPE
    row = lambda v: v.reshape(1, -1).astype(f32)

    kr_src = np.where(src >= _NOPE, src - _NOPE, -1)
    win = w_in[l]
    base = 2 * w + ql + kl
    win_ext = jnp.concatenate([win[:, :base], _take_cols(win[:, base:], kr_src)], axis=-1).astype(bf16)

    q_idx = np.concatenate([np.where(src >= 0, src + h * qk, -1) for h in range(n_heads)])
    wuq = _take_cols(w_uq[l], q_idx).astype(bf16)
    k_src = np.where((src >= 0) & (src < _NOPE), src, -1)
    k_idx = np.concatenate([np.where(k_src >= 0, k_src + h * (_NOPE + vd), -1) for h in range(n_heads)])
    wk = _take_cols(w_ukv[l], k_idx).astype(bf16)
    v_idx = []
    for h in range(n_heads):
        slab = np.full((LANES,), -1, np.int64)
        off = (h % 2) * vd
        slab[off:off + vd] = h * (_NOPE + vd) + _NOPE + np.arange(vd)
        v_idx.append(slab)
    wv = _take_cols(w_ukv[l], np.concatenate(v_idx)).astype(bf16)
    wkt_idx = np.concatenate([h * (_NOPE + vd) + np.arange(_NOPE) for h in range(n_heads)])
    wkt = jnp.take(w_ukv[l], jnp.asarray(wkt_idx), axis=-1).T.astype(bf16)

    nope_tab = lambda g: _take_cols(g.reshape(1, -1), k_src)
    rope_tab = lambda g: _take_cols(g.reshape(1, -1), kr_src)
    mn = jnp.asarray((k_src >= 0).astype(np.float32)).reshape(1, LANES)
    mr = jnp.asarray((kr_src >= 0).astype(np.float32)).reshape(1, LANES)
    gq = (nope_tab(q_nope_norm_g[l]) + rope_tab(q_rope_norm_g[l])) * scale

    dff = w_down.shape[1]
    ck = MXU_DIM
    nc = dff // ck
    wup = w_up[l].reshape(w_up.shape[1], 2 * nc, ck).transpose(1, 0, 2).astype(bf16)
    fcw = ffn_conv_w[l].reshape(-1, 2 * nc, ck).transpose(1, 0, 2).astype(f32)
    fcb = ffn_conv_b[l].reshape(2 * nc, 1, ck).astype(f32)
    wdn = w_down[l].reshape(nc, ck, -1).astype(bf16)
    return dict(
        n1g=row(norm1_g[l]), win=win_ext, cw=lru_conv_w[l].astype(f32), cb=row(lru_conv_b[l]),
        wa=_block_diag_groups(lru_w_a[l]), ba=row(lru_b_a[l]), wx=_block_diag_groups(lru_w_x[l]), bx=row(lru_b_x[l]),
        lam=row(lru_lambda[l]), qlg=row(q_lora_norm_g[l]), wuq=wuq, gq=gq.astype(f32), mn=mn, mr=mr,
        kvg=row(kv_lora_norm_g[l]), gkr=rope_tab(k_rope_norm_g[l]).astype(f32), wk=wk,
        gk=nope_tab(k_nope_norm_g[l]).astype(f32), wv=wv, wkt=wkt, log=row(lru_out_norm_g[l]),
        mog=row(mla_out_norm_g[l]), wo=w_o[l].astype(bf16), n2g=row(norm2_g[l]), wup=wup, fcw=fcw, fcb=fcb, wdn=wdn)


def _rope_tables(pos):
    half = _ROPE // 2
    inv = ROPE_THETA ** (-jnp.arange(0, _ROPE, 2, dtype=f32) / _ROPE)
    ang = pos.astype(f32)[:, None] * inv[None, :]
    cos, sin = jnp.cos(ang), jnp.sin(ang)
    n = pos.shape[0]
    cc = jnp.zeros((n, LANES), f32)
    cc = cc.at[:, 0:_R1_LO].set(1.0).at[:, _R1_LO + half:_R1_LO + half + 16].set(1.0)
    cc = cc.at[:, _R1_LO:_R1_LO + half].set(cos).at[:, _R2_LO:_R2_LO + half].set(cos)
    ss = jnp.zeros((n, LANES), f32)
    ss = ss.at[:, _R1_LO:_R1_LO + half].set(-sin).at[:, _R2_LO:_R2_LO + half].set(sin)
    return cc, ss


def kernel(x_prompt, x_sample, cache_kv_latent, cache_k_rope, state_lru_h, state_lru_conv, state_ffn_conv,
           page_table, c_prompt, c_sample, w_ada, b_ada, norm1_g, w_in, lru_conv_w, lru_conv_b, lru_w_a, lru_b_a,
           lru_w_x, lru_b_x, lru_lambda, q_lora_norm_g, w_uq, q_nope_norm_g, q_rope_norm_g, kv_lora_norm_g,
           k_rope_norm_g, w_ukv, k_nope_norm_g, lru_out_norm_g, mla_out_norm_g, w_o, norm2_g, w_up, ffn_conv_w,
           ffn_conv_b, w_down):
    b, s, d = x_prompt.shape
    nb, ds, _ = x_sample.shape
    depth = w_in.shape[0]
    assert ds == 1 and q_nope_norm_g.shape[-1] == _NOPE and q_rope_norm_g.shape[-1] == _ROPE
    n_heads = w_uq.shape[-1] // (_NOPE + _ROPE)
    scale = float(_NOPE + _ROPE) ** -0.5
    npg = page_table.shape[1]
    n_past = npg * cache_kv_latent.shape[2]
    ts = 32
    tq = min(512, s)
    tk = min(256, s)
    tm = min(512, s)
    ppsub = 8 if npg % 16 == 0 else 1
    hp = 16

    cache_krt = jnp.swapaxes(cache_k_rope, 2, 3)
    cc_p, ss_p = _rope_tables(jnp.arange(s))
    cc_s, ss_s = _rope_tables(n_past + jnp.arange(1))

    y_p = x_prompt
    y_s = x_sample.reshape(nb, d)
    c_all = jnp.concatenate([c_prompt, c_sample], axis=0)
    outs_p = [[] for _ in range(5)]
    outs_s = [[] for _ in range(5)]
    for l in range(depth):
        p = _prep_layer(l, n_heads, scale, w_in, lru_conv_w, lru_conv_b, lru_w_a, lru_b_a, lru_w_x, lru_b_x,
                        lru_lambda, norm1_g, q_lora_norm_g, w_uq, q_nope_norm_g, q_rope_norm_g, kv_lora_norm_g,
                        k_rope_norm_g, w_ukv, k_nope_norm_g, lru_out_norm_g, mla_out_norm_g, w_o, norm2_g, w_up,
                        ffn_conv_w, ffn_conv_b, w_down)
        mod = _mod_call(c_all, w_ada[l], b_ada[l])
        mp = [m_[:, None, :] for m_ in jnp.split(mod[:b], 6, axis=-1)]
        ms = jnp.split(mod[b:], 6, axis=-1)

        lru_p, q_p, k_p, v_p, kvlat_p, krope_p, hlast_p, xtail_p = _inproj_call(
            y_p, mp[0], mp[1], cc_p, ss_p, p, n_heads, ts)
        attn_p = _attn_call(q_p, k_p, v_p, p["mog"], n_heads, tq, tk)
        y_p, ftail_p = _ffn_call(y_p, lru_p, attn_p, mp[2], mp[3], mp[4], mp[5], p, tm)
        kc = lru_conv_w.shape[1]
        fk = ffn_conv_w.shape[1]
        for j, o in enumerate((kvlat_p, krope_p, hlast_p, xtail_p[:, 8 - (kc - 1):], ftail_p[:, 8 - (fk - 1):])):
            outs_p[j].append(o)

        cbuf = jnp.swapaxes(state_lru_conv[l], 0, 1)
        fbuf = jnp.swapaxes(state_ffn_conv[l], 0, 1)
        lru_s, xlru_s, hnew_s, qabs, qr, kvn_s, kr_s = _spre_call(
            y_s, ms[0], ms[1], cc_s, ss_s, cbuf, state_lru_h[l], p, n_heads, n_past == 0)
        pad_heads = lambda t: jnp.pad(jnp.swapaxes(t, 0, 1), ((0, 0), (0, hp - n_heads), (0, 0)))
        olat = _sattn_call(page_table, cache_kv_latent, cache_krt, l, p["wkt"], pad_heads(qabs), pad_heads(qr),
                           kvn_s[:, None, :], kr_s[:, :, None], n_heads, ppsub)
        olat = jnp.swapaxes(olat[:, :n_heads], 0, 1)
        y_s, up_s = _spost_call(y_s, lru_s, olat, ms[2], ms[3], ms[4], ms[5], fbuf, p, n_heads)
        lru_conv_new = jnp.concatenate([state_lru_conv[l][:, 1:], xlru_s[:, None, :]], axis=1)
        ffn_conv_new = jnp.concatenate([state_ffn_conv[l][:, 1:], up_s[:, None, :]], axis=1)
        for j, o in enumerate((kvn_s[:, None, :], kr_s[:, None, :], hnew_s, lru_conv_new, ffn_conv_new)):
            outs_s[j].append(o)

    return (y_p, y_s.reshape(nb, 1, d), *[jnp.stack(o) for o in outs_p], *[jnp.stack(o) for o in outs_s])
```

```python
import functools

import numpy as np
import jax
import jax.numpy as jnp
from jax import lax
from jax.experimental import pallas as pl
from jax.experimental.pallas import tpu as pltpu

f32 = jnp.float32
bf16 = jnp.bfloat16

EPS = 1e-6
LRU_C = 8.0
ROPE_THETA = 10000.0
LANES = 128
MXU_DIM = 256
VMEM_LIMIT = 56 * 1024 * 1024

_NOPE, _ROPE = 64, 32
_R1_LO, _R2_LO = 48, 112


def _h128_src():
    src = np.full((LANES,), -1, np.int32)
    src[0:48] = np.arange(0, 48)
    src[48:64] = _NOPE + np.arange(0, 16)
    src[64:80] = np.arange(48, 64)
    src[112:128] = _NOPE + 16 + np.arange(0, 16)
    return src


def _dot(a, b):
    return jnp.dot(a, b, preferred_element_type=f32)


def _dot_nt(a, b):
    return lax.dot_general(a, b, (((1,), (1,)), ((), ())), preferred_element_type=f32)


def _rms(x, g):
    ms = jnp.mean(x * x, axis=-1, keepdims=True)
    return x * lax.rsqrt(ms + EPS) * g


def _neg_expm1_2x(y):
    t = jnp.tanh(y)
    return -2.0 * t / (1.0 - t)


def _lru_gates(x_conv, wa_ref, ba, wx_ref, bx, lam):
    xb = x_conv.astype(bf16)
    ng = wa_ref.shape[0]
    ra = jnp.concatenate([_dot(xb[:, g * MXU_DIM:(g + 1) * MXU_DIM], wa_ref[g]) for g in range(ng)], axis=-1) + ba
    ia = jnp.concatenate([_dot(xb[:, g * MXU_DIM:(g + 1) * MXU_DIM], wx_ref[g]) for g in range(ng)], axis=-1) + bx
    r = jax.nn.sigmoid(ra)
    ig = jax.nn.sigmoid(ia)
    log_a = (-LRU_C) * r * jax.nn.softplus(-lam)
    a = jnp.exp(log_a)
    mult = jnp.sqrt(_neg_expm1_2x(log_a))
    return a, mult, ig * x_conv


def _q_heads(q, gq, mn, mr, cc, ss, n_heads, to3d):
    outs = []
    for h in range(n_heads):
        xh = q[:, h * LANES:(h + 1) * LANES]
        x2 = xh * xh
        ssn = jnp.sum(x2 * mn, axis=-1, keepdims=True)
        ssr = jnp.sum(x2 * mr, axis=-1, keepdims=True)
        inv = lax.rsqrt(ssn * (1.0 / _NOPE) + EPS) * mn + lax.rsqrt(ssr * (1.0 / _ROPE) + EPS) * mr
        xn = xh * inv * gq
        outs.append(to3d(xn) * cc + to3d(pltpu.roll(xn, LANES // 2, 1)) * ss)
    return outs


def _krope(x, gkr, cc, ss, to3d):
    ssq = jnp.sum(x * x, axis=-1, keepdims=True)
    xn = x * lax.rsqrt(ssq * (1.0 / _ROPE) + EPS) * gkr
    return to3d(xn) * cc + to3d(pltpu.roll(xn, LANES // 2, 1)) * ss


def _mod_kernel(c_ref, w_ref, b_ref, o_ref):
    c = c_ref[...]
    sc = (c * jax.nn.sigmoid(c)).astype(bf16)
    o_ref[...] = _dot(sc, w_ref[...].astype(bf16)) + b_ref[...]


def _mod_call(c_all, w_ada, b_ada):
    m, d = c_all.shape
    n = w_ada.shape[1]
    tn = 1536 if n % 1536 == 0 else n
    return pl.pallas_call(
        _mod_kernel,
        grid=(n // tn,),
        in_specs=[pl.BlockSpec((m, d), lambda j: (0, 0)),
                  pl.BlockSpec((d, tn), lambda j: (0, j)),
                  pl.BlockSpec((1, tn), lambda j: (0, j))],
        out_specs=pl.BlockSpec((m, tn), lambda j: (0, j)),
        out_shape=jax.ShapeDtypeStruct((m, n), f32),
        compiler_params=pltpu.CompilerParams(dimension_semantics=("arbitrary",), vmem_limit_bytes=VMEM_LIMIT),
        name="adaln_mod",
    )(c_all, w_ada, b_ada.reshape(1, n))


def _inproj_kernel(x_ref, sh_ref, sc_ref, cc_ref, ss_ref, n1g_ref, win_ref, cw_ref, cb_ref,
                   wa_ref, ba_ref, wx_ref, bx_ref, lam_ref, qlg_ref, wuq_ref, gq_ref, mn_ref, mr_ref,
                   kvg_ref, gkr_ref, wk_ref, gk_ref, wv_ref, vone_ref, log_ref,
                   lru_o, q_o, k_o, v_o, kvlat_o, krope_o, hlast_o, xtail_o,
                   xprev_s, a_s, u_s, hs_s, h_s, *, n_heads):
    i = pl.program_id(0)
    nb, ts, d = x_ref.shape
    m = nb * ts
    w = lam_ref.shape[-1]
    ql = qlg_ref.shape[-1]
    kl = kvg_ref.shape[-1]
    kconv = cw_ref.shape[0]
    nlc = w // LANES

    @pl.when(i == 0)
    def _():
        xprev_s[...] = jnp.zeros_like(xprev_s)
        h_s[...] = jnp.zeros_like(h_s)

    def to3d(v):
        return v.reshape(nb, ts, v.shape[-1])

    x = x_ref[...]
    h = _rms(x, n1g_ref[...]) * (1.0 + sc_ref[...]) + sh_ref[...]
    z = _dot(h.reshape(m, d).astype(bf16), win_ref[...])
    x_lru = z[:, 0:w]
    g_lru = z[:, w:2 * w]
    q_lat = z[:, 2 * w:2 * w + ql]
    kv_lat = z[:, 2 * w + ql:2 * w + ql + kl]
    kr_pre = z[:, 2 * w + ql + kl:2 * w + ql + kl + LANES]

    t_idx = lax.broadcasted_iota(jnp.int32, (m, 1), 0) & (ts - 1)
    xp = xprev_s[...]
    x_conv = cb_ref[...]
    for j in range(kconv - 1):
        k = kconv - 1 - j
        sh = jnp.where(t_idx >= k, pltpu.roll(x_lru, k, 0), pltpu.roll(xp, (m + k - ts) % m, 0))
        x_conv = x_conv + sh * cw_ref[j:j + 1, :]
    x_conv = x_conv + x_lru * cw_ref[kconv - 1:kconv, :]
    xprev_s[...] = x_lru
    xtail_o[...] = to3d(x_lru)[:, ts - 8:, :]

    a, mult, ux = _lru_gates(x_conv, wa_ref, ba_ref[...], wx_ref, bx_ref[...], lam_ref[...])
    mult = jnp.where(jnp.logical_and(t_idx == 0, i == 0), 1.0, mult)
    u = mult * ux
    for j in range(nlc):
        a_s[j] = a[:, j * LANES:(j + 1) * LANES]
        u_s[j] = u[:, j * LANES:(j + 1) * LANES]

    def scan_step(t, hc):
        out = []
        for j in range(nlc):
            hj = a_s[j, pl.ds(t, nb, stride=ts), :] * hc[j] + u_s[j, pl.ds(t, nb, stride=ts), :]
            hs_s[j, pl.ds(t, nb, stride=ts), :] = hj
            out.append(hj)
        return tuple(out)

    hc = lax.fori_loop(0, ts, scan_step, tuple(h_s[j] for j in range(nlc)), unroll=8)
    for j in range(nlc):
        h_s[j] = hc[j]
    hlast_o[...] = jnp.concatenate(list(hc), axis=-1)
    hs = jnp.concatenate([hs_s[j] for j in range(nlc)], axis=-1)
    lru_out = hs * jax.nn.gelu(g_lru)
    lru_o[...] = to3d(_rms(lru_out, log_ref[...]).astype(bf16))

    cc = cc_ref[...][None]
    ss = ss_ref[...][None]
    qn = _rms(q_lat, qlg_ref[...]).astype(bf16)
    q = _dot(qn, wuq_ref[...])
    for hd, o in enumerate(_q_heads(q, gq_ref[...], mn_ref[...], mr_ref[...], cc, ss, n_heads, to3d)):
        q_o[:, :, hd * LANES:(hd + 1) * LANES] = o.astype(bf16)

    kvn = _rms(kv_lat, kvg_ref[...])
    kvlat_o[...] = to3d(kvn)
    kr = _krope(kr_pre, gkr_ref[...], cc, ss, to3d)
    krope_o[...] = jnp.concatenate([kr[:, :, _R1_LO:_R1_LO + 16], kr[:, :, _R2_LO:_R2_LO + 16]], axis=-1)
    kvb = kvn.astype(bf16)
    kk = _dot(kvb, wk_ref[...])
    v_o[...] = to3d((_dot(kvb, wv_ref[...]) + vone_ref[...]).astype(bf16))
    gk = gk_ref[...]
    for hd in range(n_heads):
        kh = kk[:, hd * LANES:(hd + 1) * LANES]
        ssq = jnp.sum(kh * kh, axis=-1, keepdims=True)
        khn = kh * lax.rsqrt(ssq * (1.0 / _NOPE) + EPS) * gk
        k_o[:, :, hd * LANES:(hd + 1) * LANES] = (to3d(khn) + kr).astype(bf16)


def _const_spec(shape):
    nd = len(shape)
    return pl.BlockSpec(shape, lambda *_: (0,) * nd, pipeline_mode=pl.Buffered(1))


def _inproj_call(x, sh1, sc1, cc, ss, p, n_heads, ts):
    b, s, d = x.shape
    w = p["lam"].shape[-1]
    kl = p["kvg"].shape[-1]
    hw = n_heads * LANES
    m = b * ts
    consts = [p["n1g"], p["win"], p["cw"], p["cb"], p["wa"], p["ba"], p["wx"], p["bx"], p["lam"], p["qlg"],
              p["wuq"], p["gq"], p["mn"], p["mr"], p["kvg"], p["gkr"], p["wk"], p["gk"], p["wv"], p["vone"], p["log"]]
    in_specs = [pl.BlockSpec((b, ts, d), lambda i: (0, i, 0)),
                _const_spec(sh1.shape), _const_spec(sc1.shape),
                pl.BlockSpec((ts, LANES), lambda i: (i, 0)),
                pl.BlockSpec((ts, LANES), lambda i: (i, 0))] + [_const_spec(c.shape) for c in consts]

    def tile(n, dt):
        return pl.BlockSpec((b, ts, n), lambda i: (0, i, 0)), jax.ShapeDtypeStruct((b, s, n), dt)

    outs = [tile(w, bf16), tile(hw, bf16), tile(hw, bf16), tile(hw, bf16), tile(kl, f32), tile(_ROPE, f32),
            (pl.BlockSpec((b, w), lambda i: (0, 0)), jax.ShapeDtypeStruct((b, w), f32)),
            (pl.BlockSpec((b, 8, w), lambda i: (0, 0, 0)), jax.ShapeDtypeStruct((b, 8, w), f32))]
    nlc = w // LANES
    return pl.pallas_call(
        functools.partial(_inproj_kernel, n_heads=n_heads),
        grid=(s // ts,),
        in_specs=in_specs,
        out_specs=[o[0] for o in outs],
        out_shape=[o[1] for o in outs],
        scratch_shapes=[pltpu.VMEM((m, w), f32), pltpu.VMEM((nlc, m, LANES), f32), pltpu.VMEM((nlc, m, LANES), f32),
                        pltpu.VMEM((nlc, m, LANES), f32), pltpu.VMEM((nlc, b, LANES), f32)],
        compiler_params=pltpu.CompilerParams(dimension_semantics=("arbitrary",), vmem_limit_bytes=VMEM_LIMIT),
        name="prompt_inproj",
    )(x, sh1, sc1, cc, ss, *consts)


def _attn_kernel(q_ref, k_ref, v_ref, g_ref, o_ref, m_s, acc_s, *, n_heads, tk):
    qi = pl.program_id(1)
    tq = q_ref.shape[1]
    nmask = tq // tk
    n_full = qi * nmask
    m_s[...] = jnp.full_like(m_s, -1e30)
    acc_s[...] = jnp.zeros_like(acc_s)

    def block(j, r0, masked):
        start = pl.multiple_of(j * tk, tk)
        if masked:
            vis = (lax.broadcasted_iota(jnp.int32, (tq - r0, tk), 1)
                   <= lax.broadcasted_iota(jnp.int32, (tq - r0, tk), 0))
        for hd in range(n_heads):
            hs = slice(hd * LANES, (hd + 1) * LANES)
            s = _dot_nt(q_ref[0, r0:, hs], k_ref[0, pl.ds(start, tk), hs])
            if masked:
                s = jnp.where(vis, s, -1e30)
            mx = m_s[hd, r0:, :]
            m_new = jnp.maximum(mx, jnp.max(s, axis=-1, keepdims=True))
            alpha = jnp.exp2(mx - m_new)
            pm = jnp.exp2(s - jnp.concatenate([m_new] * (tk // LANES), axis=-1))
            acc_s[hd, r0:, :] = alpha * acc_s[hd, r0:, :] + _dot(pm.astype(bf16), v_ref[0, pl.ds(start, tk), hs])
            m_s[hd, r0:, :] = m_new

    def full_block(j, c):
        block(j, 0, False)
        return c

    lax.fori_loop(0, n_full, full_block, 0)
    for jm in range(nmask):
        block(n_full + jm, jm * tk, True)
    lane = lax.broadcasted_iota(jnp.int32, (1, LANES), 1)
    half = LANES // 2
    heads = []
    for hd in range(n_heads):
        acc = acc_s[hd]
        lo = (hd % 2) * half
        one_lane = (half - lo)
        l = jnp.sum(jnp.where(lane == one_lane, acc, 0.0), axis=-1, keepdims=True)
        heads.append(jnp.where((lane >= lo) & (lane < lo + half), acc, 0.0) / l)
    o = jnp.concatenate([heads[2 * pp] + heads[2 * pp + 1] for pp in range(n_heads // 2)], axis=-1)
    o_ref[0] = _rms(o, g_ref[...]).astype(bf16)


def _attn_call(q, k, v, g, n_heads, tq, tk):
    b, s, hw = q.shape
    wout = g.shape[-1]
    return pl.pallas_call(
        functools.partial(_attn_kernel, n_heads=n_heads, tk=tk),
        grid=(b, s // tq),
        in_specs=[pl.BlockSpec((1, tq, hw), lambda bi, qi: (bi, qi, 0)),
                  pl.BlockSpec((1, s, hw), lambda bi, qi: (bi, 0, 0)),
                  pl.BlockSpec((1, s, hw), lambda bi, qi: (bi, 0, 0)),
                  pl.BlockSpec((1, wout), lambda bi, qi: (0, 0))],
        out_specs=pl.BlockSpec((1, tq, wout), lambda bi, qi: (bi, qi, 0)),
        out_shape=jax.ShapeDtypeStruct((b, s, wout), bf16),
        scratch_shapes=[pltpu.VMEM((n_heads, tq, LANES), f32), pltpu.VMEM((n_heads, tq, LANES), f32)],
        compiler_params=pltpu.CompilerParams(dimension_semantics=("arbitrary", "arbitrary"),
                                             vmem_limit_bytes=VMEM_LIMIT),
        name="prompt_attn",
    )(q, k, v, g)


def _mix_and_norm2(x, lru, attn, wo_ref, g1, n2g, sc2, sh2):
    w = lru.shape[-1]
    mixed = _dot(lru, wo_ref[pl.ds(0, w), :]) + _dot(attn, wo_ref[pl.ds(w, attn.shape[-1]), :])
    x1 = x + g1 * mixed
    h2 = (_rms(x1, n2g) * (1.0 + sc2) + sh2).astype(bf16)
    return x1, h2


def _ffn_kernel(x_ref, lru_ref, attn_ref, g1_ref, sh2_ref, sc2_ref, g2_ref, n2g_ref, wo_ref, wup_ref,
                fcw_ref, fcb_ref, wdn_ref, y_o, tail_o, tail_s):
    si = pl.program_id(1)
    tm = x_ref.shape[1]
    nc = wdn_ref.shape[0]
    ck = wdn_ref.shape[1]
    kconv = fcw_ref.shape[1]

    @pl.when(si == 0)
    def _():
        tail_s[...] = jnp.zeros_like(tail_s)

    x1, h2 = _mix_and_norm2(x_ref[0], lru_ref[0], attn_ref[0], wo_ref, g1_ref[0], n2g_ref[...], sc2_ref[0], sh2_ref[0])
    r8 = lax.broadcasted_iota(jnp.int32, (8, 1), 0)

    def conv(up, c):
        tail = tail_s[c]
        first = up[0:8]
        out = fcb_ref[c]
        cw = fcw_ref[c]
        for j in range(kconv - 1):
            k = kconv - 1 - j
            head = jnp.where(r8 < k, pltpu.roll(tail, k, 0), pltpu.roll(first, k, 0))
            sh = jnp.concatenate([head, pltpu.roll(up, k, 0)[8:]], axis=0)
            out = out + sh * cw[j:j + 1, :]
        out = out + up * cw[kconv - 1:kconv, :]
        tail_s[c] = up[tm - 8:]
        tail_o[0, :, c * ck:(c + 1) * ck] = up[tm - 8:]
        return out

    acc = jnp.zeros((tm, x1.shape[-1]), f32)
    for c in range(nc):
        val = conv(_dot(h2, wup_ref[c]), c)
        gt = conv(_dot(h2, wup_ref[nc + c]), nc + c)
        act = (jax.nn.gelu(gt) * val).astype(bf16)
        acc = acc + _dot(act, wdn_ref[c])
    y_o[0] = x1 + g2_ref[0] * acc


def _ffn_call(x, lru, attn, g1, sh2, sc2, g2, p, tm):
    b, s, d = x.shape
    w = lru.shape[-1]
    nc, ck, _ = p["wdn"].shape
    row = lambda n: pl.BlockSpec((1, tm, n), lambda bi, si: (bi, si, 0))
    modspec = pl.BlockSpec((1, 1, d), lambda bi, si: (bi, 0, 0))
    consts = [p["n2g"], p["wo"], p["wup"], p["fcw"], p["fcb"], p["wdn"]]
    return pl.pallas_call(
        _ffn_kernel,
        grid=(b, s // tm),
        in_specs=[row(d), row(w), row(attn.shape[-1]), modspec, modspec, modspec, modspec]
        + [_const_spec(c.shape) for c in consts],
        out_specs=[row(d), pl.BlockSpec((1, 8, 2 * nc * ck), lambda bi, si: (bi, 0, 0))],
        out_shape=[jax.ShapeDtypeStruct((b, s, d), f32), jax.ShapeDtypeStruct((b, 8, 2 * nc * ck), f32)],
        scratch_shapes=[pltpu.VMEM((2 * nc, 8, ck), f32)],
        compiler_params=pltpu.CompilerParams(dimension_semantics=("arbitrary", "arbitrary"),
                                             vmem_limit_bytes=VMEM_LIMIT),
        name="prompt_ffn",
    )(x, lru, attn, g1, sh2, sc2, g2, *consts)


def _spre_kernel(x_ref, sh_ref, sc_ref, cc_ref, ss_ref, n1g_ref, win_ref, cbuf_ref, cw_ref, cb_ref,
                 wa_ref, ba_ref, wx_ref, bx_ref, lam_ref, h0_ref, qlg_ref, wuq_ref, gq_ref, mn_ref, mr_ref,
                 kvg_ref, gkr_ref, wk_ref, gk_ref, log_ref,
                 lru_o, xlru_o, hnew_o, qabs_o, qr_o, kvn_o, kr_o, *, n_heads, first_pos):
    w = lam_ref.shape[-1]
    ql = qlg_ref.shape[-1]
    kl = kvg_ref.shape[-1]
    kconv = cw_ref.shape[0]
    ident = lambda v: v

    h = _rms(x_ref[...], n1g_ref[...]) * (1.0 + sc_ref[...]) + sh_ref[...]
    z = _dot(h.astype(bf16), win_ref[...])
    x_lru = z[:, 0:w]
    g_lru = z[:, w:2 * w]
    q_lat = z[:, 2 * w:2 * w + ql]
    kv_lat = z[:, 2 * w + ql:2 * w + ql + kl]
    kr_pre = z[:, 2 * w + ql + kl:2 * w + ql + kl + LANES]

    x_conv = cb_ref[...]
    for j in range(kconv - 1):
        x_conv = x_conv + cbuf_ref[j] * cw_ref[j:j + 1, :]
    x_conv = x_conv + x_lru * cw_ref[kconv - 1:kconv, :]
    xlru_o[...] = x_lru
    a, mult, ux = _lru_gates(x_conv, wa_ref, ba_ref[...], wx_ref, bx_ref[...], lam_ref[...])
    if first_pos:
        mult = jnp.ones_like(mult)
    hn = a * h0_ref[...] + mult * ux
    hnew_o[...] = hn
    lru_o[...] = _rms(hn * jax.nn.gelu(g_lru), log_ref[...]).astype(bf16)

    cc = cc_ref[...]
    ss = ss_ref[...]
    qn = _rms(q_lat, qlg_ref[...]).astype(bf16)
    q = _dot(qn, wuq_ref[...])
    gk = gk_ref[...]
    for hd, o in enumerate(_q_heads(q, gq_ref[...], mn_ref[...], mr_ref[...], cc, ss, n_heads, ident)):
        wkh = wk_ref[:, hd * LANES:(hd + 1) * LANES]
        qabs_o[hd] = _dot_nt((o * gk).astype(bf16), wkh).astype(bf16)
        qr_o[hd] = jnp.concatenate([o[:, _R1_LO:_R1_LO + 16], o[:, _R2_LO:_R2_LO + 16]], axis=-1).astype(bf16)

    kvn_o[...] = _rms(kv_lat, kvg_ref[...])
    kr = _krope(kr_pre, gkr_ref[...], cc, ss, ident)
    kr_o[...] = jnp.concatenate([kr[:, _R1_LO:_R1_LO + 16], kr[:, _R2_LO:_R2_LO + 16]], axis=-1)


def _spre_call(x, sh1, sc1, cc, ss, cbuf, h0, p, n_heads, first_pos):
    nb, d = x.shape
    w = p["lam"].shape[-1]
    kl = p["kvg"].shape[-1]
    out_shape = [jax.ShapeDtypeStruct((nb, w), bf16), jax.ShapeDtypeStruct((nb, w), f32),
                 jax.ShapeDtypeStruct((nb, w), f32), jax.ShapeDtypeStruct((n_heads, nb, kl), bf16),
                 jax.ShapeDtypeStruct((n_heads, nb, _ROPE), bf16), jax.ShapeDtypeStruct((nb, kl), f32),
                 jax.ShapeDtypeStruct((nb, _ROPE), f32)]
    return pl.pallas_call(
        functools.partial(_spre_kernel, n_heads=n_heads, first_pos=first_pos),
        out_shape=out_shape,
        compiler_params=pltpu.CompilerParams(vmem_limit_bytes=VMEM_LIMIT),
        name="sample_inproj",
    )(x, sh1, sc1, cc, ss, p["n1g"], p["win"], cbuf, p["cw"], p["cb"], p["wa"], p["ba"], p["wx"], p["bx"],
      p["lam"], h0, p["qlg"], p["wuq"], p["gq"], p["mn"], p["mr"], p["kvg"], p["gkr"], p["wk"], p["gk"], p["log"])


def _sattn_kernel(pt_ref, lat_hbm, krt_hbm, wkt_ref, qabs_ref, qr_ref, latn_ref, krn_ref, o_ref,
                  lat_buf, kr_buf, lhs_s, latb_s, s_s, sems, *, layer, n_heads, ppsub):
    b = pl.program_id(0)
    nseq = pl.num_programs(0)
    _, npg, page, kl = lat_buf.shape
    nk = wkt_ref.shape[0]
    hp = qabs_ref.shape[0]
    n_past = npg * page
    tk = ppsub * page
    slot = lax.rem(b, 2)

    def page_copies(seq, slot_, pg):
        src = pt_ref[seq * npg + pg]
        return (pltpu.make_async_copy(lat_hbm.at[layer, src], lat_buf.at[slot_, pg], sems.at[0, slot_]),
                pltpu.make_async_copy(krt_hbm.at[layer, src], kr_buf.at[slot_, pg], sems.at[1, slot_]))

    def fetch(seq, slot_):
        def body(pg, c):
            for cp in page_copies(seq, slot_, pg):
                cp.start()
            return c
        lax.fori_loop(0, npg, body, 0)

    @pl.when(b == 0)
    def _():
        fetch(0, 0)
        lhs_s[0:nk, :] = wkt_ref[...]

    @pl.when(b + 1 < nseq)
    def _():
        fetch(b + 1, 1 - slot)

    lhs_s[nk:nk + hp, :] = qabs_ref[...]

    def wait_body(pg, c):
        for cp in page_copies(b, slot, pg):
            cp.wait()
        return c
    lax.fori_loop(0, npg, wait_body, 0)

    def scores(latb, krt):
        n = latb.shape[0]
        big = _dot_nt(lhs_s[...], latb)
        knt = big[0:nk]
        ssq = jnp.sum((knt * knt).reshape(n_heads, _NOPE, n), axis=1)
        rs = lax.rsqrt(ssq * (1.0 / _NOPE) + EPS)
        sr = _dot(qr_ref[...], krt)
        s8 = big[nk:nk + n_heads] * rs + sr[0:n_heads]
        return jnp.concatenate([s8, jnp.zeros((hp - n_heads, n), f32)], axis=0)

    def sub(j, c):
        p0 = pl.multiple_of(j * ppsub, ppsub)
        k0 = pl.multiple_of(j * tk, tk)
        latb = lat_buf[slot, pl.ds(p0, ppsub)].reshape(tk, kl).astype(bf16)
        krt = jnp.concatenate([kr_buf[slot, p0 + i] for i in range(ppsub)], axis=-1).astype(bf16)
        latb_s[pl.ds(k0, tk), :] = latb
        s_s[:, pl.ds(k0, tk)] = scores(latb, krt)
        return c

    lax.fori_loop(0, npg // ppsub, sub, 0, unroll=2)

    latn = jnp.broadcast_to(latn_ref[...], (page, kl)).astype(bf16)
    krn = jnp.broadcast_to(krn_ref[...], (krn_ref.shape[0], page)).astype(bf16)
    s_new = scores(latn, krn)
    s_s[:, n_past:n_past + page] = jnp.where(lax.broadcasted_iota(jnp.int32, s_new.shape, 1) == 0, s_new, -1e30)
    latb_s[n_past:n_past + page, :] = latn

    s = s_s[...]
    pm = jnp.exp2(s - jnp.max(s, axis=-1, keepdims=True))
    l = jnp.sum(pm, axis=-1, keepdims=True)
    o_ref[...] = _dot(pm.astype(bf16), latb_s[...]) / l


def _sattn_call(page_table, cache_lat, cache_krt, layer, wkt, qabs, qr, latn, krn, n_heads, ppsub):
    nb, npg = page_table.shape
    _, _, page, kl = cache_lat.shape
    rd = cache_krt.shape[2]
    hp = qabs.shape[1]
    nk = wkt.shape[0]
    nkeys = npg * page + page
    grid_spec = pltpu.PrefetchScalarGridSpec(
        num_scalar_prefetch=1,
        grid=(nb,),
        in_specs=[pl.BlockSpec(memory_space=pl.ANY),
                  pl.BlockSpec(memory_space=pl.ANY),
                  pl.BlockSpec((nk, kl), lambda bi, pt: (0, 0)),
                  pl.BlockSpec((None, hp, kl), lambda bi, pt: (bi, 0, 0)),
                  pl.BlockSpec((None, hp, rd), lambda bi, pt: (bi, 0, 0)),
                  pl.BlockSpec((None, 1, kl), lambda bi, pt: (bi, 0, 0)),
                  pl.BlockSpec((None, rd, 1), lambda bi, pt: (bi, 0, 0))],
        out_specs=pl.BlockSpec((None, hp, kl), lambda bi, pt: (bi, 0, 0)),
        scratch_shapes=[pltpu.VMEM((2, npg, page, kl), f32), pltpu.VMEM((2, npg, rd, page), f32),
                        pltpu.VMEM((nk + hp, kl), bf16), pltpu.VMEM((nkeys, kl), bf16),
                        pltpu.VMEM((hp, nkeys), f32), pltpu.SemaphoreType.DMA((2, 2))])
    return pl.pallas_call(
        functools.partial(_sattn_kernel, layer=layer, n_heads=n_heads, ppsub=ppsub),
        grid_spec=grid_spec,
        out_shape=jax.ShapeDtypeStruct((nb, hp, kl), f32),
        compiler_params=pltpu.CompilerParams(dimension_semantics=("arbitrary",), vmem_limit_bytes=VMEM_LIMIT),
        name="sample_attn",
    )(page_table.reshape(-1), cache_lat, cache_krt, wkt, qabs, qr, latn, krn)


def _spost_kernel(x_ref, lru_ref, olat_ref, g1_ref, sh2_ref, sc2_ref, g2_ref, mog_ref, n2g_ref, wv_ref, wo_ref,
                  wup_ref, fcw_ref, fcb_ref, wdn_ref, fbuf_ref, y_o, up_o, *, n_heads):
    nc = wdn_ref.shape[0]
    ck = wdn_ref.shape[1]
    kconv = fcw_ref.shape[1]
    heads = [_dot(olat_ref[hd].astype(bf16), wv_ref[:, hd * LANES:(hd + 1) * LANES]) for hd in range(n_heads)]
    attn = jnp.concatenate([heads[2 * pp] + heads[2 * pp + 1] for pp in range(n_heads // 2)], axis=-1)
    attn = _rms(attn, mog_ref[...]).astype(bf16)
    x1, h2 = _mix_and_norm2(x_ref[...], lru_ref[...], attn, wo_ref, g1_ref[...], n2g_ref[...], sc2_ref[...], sh2_ref[...])

    def conv(up, c):
        out = fcb_ref[c]
        cw = fcw_ref[c]
        for j in range(kconv - 1):
            out = out + fbuf_ref[j, :, c * ck:(c + 1) * ck] * cw[j:j + 1, :]
        up_o[:, c * ck:(c + 1) * ck] = up
        return out + up * cw[kconv - 1:kconv, :]

    acc = jnp.zeros(x1.shape, f32)
    for c in range(nc):
        val = conv(_dot(h2, wup_ref[c]), c)
        gt = conv(_dot(h2, wup_ref[nc + c]), nc + c)
        acc = acc + _dot((jax.nn.gelu(gt) * val).astype(bf16), wdn_ref[c])
    y_o[...] = x1 + g2_ref[...] * acc


def _spost_call(x, lru, olat, g1, sh2, sc2, g2, fbuf, p, n_heads):
    nb, d = x.shape
    nc, ck, _ = p["wdn"].shape
    return pl.pallas_call(
        functools.partial(_spost_kernel, n_heads=n_heads),
        out_shape=[jax.ShapeDtypeStruct((nb, d), f32), jax.ShapeDtypeStruct((nb, 2 * nc * ck), f32)],
        compiler_params=pltpu.CompilerParams(vmem_limit_bytes=VMEM_LIMIT),
        name="sample_ffn",
    )(x, lru, olat, g1, sh2, sc2, g2, p["mog"], p["n2g"], p["wv"], p["wo"], p["wup"], p["fcw"], p["fcb"], p["wdn"], fbuf)


def _take_cols(wmat, idx):
    padded = jnp.concatenate([wmat, jnp.zeros(wmat.shape[:-1] + (1,), wmat.dtype)], axis=-1)
    return jnp.take(padded, jnp.asarray(np.where(idx < 0, wmat.shape[-1], idx)), axis=-1)


def _block_diag_groups(wh):
    nh, hd, _ = wh.shape
    per = MXU_DIM // hd
    groups = []
    for g in range(nh // per):
        blk = jnp.zeros((MXU_DIM, MXU_DIM), wh.dtype)
        for j in range(per):
            blk = lax.dynamic_update_slice(blk, wh[g * per + j], (j * hd, j * hd))
        groups.append(blk)
    return jnp.stack(groups).astype(bf16)


def _prep_layer(l, n_heads, scale, w_in, lru_conv_w, lru_conv_b, lru_w_a, lru_b_a, lru_w_x, lru_b_x, lru_lambda,
                norm1_g, q_lora_norm_g, w_uq, q_nope_norm_g, q_rope_norm_g, kv_lora_norm_g, k_rope_norm_g, w_ukv,
                k_nope_norm_g, lru_out_norm_g, mla_out_norm_g, w_o, norm2_g, w_up, ffn_conv_w, ffn_conv_b, w_down):
    src = _h128_src()
    w = lru_lambda.shape[-1]
    ql = q_lora_norm_g.shape[-1]
    kl = kv_lora_norm_g.shape[-1]
    qk = _NOPE + _ROPE
    vd = w_ukv.shape[-1] // n_heads - _NOPE
    row = lambda v: v.reshape(1, -1).astype(f32)

    kr_src = np.where(src >= _NOPE, src - _NOPE, -1)
    win = w_in[l]
    base = 2 * w + ql + kl
    win_ext = jnp.concatenate([win[:, :base], _take_cols(win[:, base:], kr_src)], axis=-1).astype(bf16)

    q_idx = np.concatenate([np.where(src >= 0, src + h * qk, -1) for h in range(n_heads)])
    wuq = _take_cols(w_uq[l], q_idx).astype(bf16)
    k_src = np.where((src >= 0) & (src < _NOPE), src, -1)
    k_idx = np.concatenate([np.where(k_src >= 0, k_src + h * (_NOPE + vd), -1) for h in range(n_heads)])
    wk = _take_cols(w_ukv[l], k_idx).astype(bf16)
    v_idx = []
    vone = np.zeros((1, n_heads * LANES), np.float32)
    for h in range(n_heads):
        slab = np.full((LANES,), -1, np.int64)
        off = (h % 2) * vd
        slab[off:off + vd] = h * (_NOPE + vd) + _NOPE + np.arange(vd)
        v_idx.append(slab)
        vone[0, h * LANES + (vd - off)] = 1.0
    wv = _take_cols(w_ukv[l], np.concatenate(v_idx)).astype(bf16)
    wkt_idx = np.concatenate([h * (_NOPE + vd) + np.arange(_NOPE) for h in range(n_heads)])
    wkt = jnp.take(w_ukv[l], jnp.asarray(wkt_idx), axis=-1).T.astype(bf16)

    nope_tab = lambda g: _take_cols(g.reshape(1, -1), k_src)
    rope_tab = lambda g: _take_cols(g.reshape(1, -1), kr_src)
    mn = jnp.asarray((k_src >= 0).astype(np.float32)).reshape(1, LANES)
    mr = jnp.asarray((kr_src >= 0).astype(np.float32)).reshape(1, LANES)
    gq = (nope_tab(q_nope_norm_g[l]) + rope_tab(q_rope_norm_g[l])) * scale

    dff = w_down.shape[1]
    ck = MXU_DIM
    nc = dff // ck
    wup = w_up[l].reshape(w_up.shape[1], 2 * nc, ck).transpose(1, 0, 2).astype(bf16)
    fcw = ffn_conv_w[l].reshape(-1, 2 * nc, ck).transpose(1, 0, 2).astype(f32)
    fcb = ffn_conv_b[l].reshape(2 * nc, 1, ck).astype(f32)
    wdn = w_down[l].reshape(nc, ck, -1).astype(bf16)
    return dict(
        n1g=row(norm1_g[l]), win=win_ext, cw=lru_conv_w[l].astype(f32), cb=row(lru_conv_b[l]),
        wa=_block_diag_groups(lru_w_a[l]), ba=row(lru_b_a[l]), wx=_block_diag_groups(lru_w_x[l]), bx=row(lru_b_x[l]),
        lam=row(lru_lambda[l]), qlg=row(q_lora_norm_g[l]), wuq=wuq, gq=gq.astype(f32), mn=mn, mr=mr,
        kvg=row(kv_lora_norm_g[l]), gkr=rope_tab(k_rope_norm_g[l]).astype(f32), wk=wk,
        gk=nope_tab(k_nope_norm_g[l]).astype(f32), wv=wv, vone=jnp.asarray(vone), wkt=wkt, log=row(lru_out_norm_g[l]),
        mog=row(mla_out_norm_g[l]), wo=w_o[l].astype(bf16), n2g=row(norm2_g[l]), wup=wup, fcw=fcw, fcb=fcb, wdn=wdn)


def _rope_tables(pos):
    half = _ROPE // 2
    inv = ROPE_THETA ** (-jnp.arange(0, _ROPE, 2, dtype=f32) / _ROPE)
    ang = pos.astype(f32)[:, None] * inv[None, :]
    cos, sin = jnp.cos(ang), jnp.sin(ang)
    n = pos.shape[0]
    cc = jnp.zeros((n, LANES), f32)
    cc = cc.at[:, 0:_R1_LO].set(1.0).at[:, _R1_LO + half:_R1_LO + half + 16].set(1.0)
    cc = cc.at[:, _R1_LO:_R1_LO + half].set(cos).at[:, _R2_LO:_R2_LO + half].set(cos)
    ss = jnp.zeros((n, LANES), f32)
    ss = ss.at[:, _R1_LO:_R1_LO + half].set(-sin).at[:, _R2_LO:_R2_LO + half].set(sin)
    return cc, ss


def kernel(x_prompt, x_sample, cache_kv_latent, cache_k_rope, state_lru_h, state_lru_conv, state_ffn_conv,
           page_table, c_prompt, c_sample, w_ada, b_ada, norm1_g, w_in, lru_conv_w, lru_conv_b, lru_w_a, lru_b_a,
           lru_w_x, lru_b_x, lru_lambda, q_lora_norm_g, w_uq, q_nope_norm_g, q_rope_norm_g, kv_lora_norm_g,
           k_rope_norm_g, w_ukv, k_nope_norm_g, lru_out_norm_g, mla_out_norm_g, w_o, norm2_g, w_up, ffn_conv_w,
           ffn_conv_b, w_down):
    b, s, d = x_prompt.shape
    nb, ds, _ = x_sample.shape
    depth = w_in.shape[0]
    assert ds == 1 and q_nope_norm_g.shape[-1] == _NOPE and q_rope_norm_g.shape[-1] == _ROPE
    n_heads = w_uq.shape[-1] // (_NOPE + _ROPE)
    scale = float(_NOPE + _ROPE) ** -0.5 * float(np.log2(np.e))
    npg = page_table.shape[1]
    n_past = npg * cache_kv_latent.shape[2]
    ts = 32
    tq = min(512, s)
    tk = min(256, s)
    tm = min(512, s)
    ppsub = 8 if npg % 16 == 0 else 1
    hp = 16

    cache_krt = jnp.swapaxes(cache_k_rope, 2, 3)
    cc_p, ss_p = _rope_tables(jnp.arange(s))
    cc_s, ss_s = _rope_tables(n_past + jnp.arange(1))

    y_p = x_prompt
    y_s = x_sample.reshape(nb, d)
    c_all = jnp.concatenate([c_prompt, c_sample], axis=0)
    outs_p = [[] for _ in range(5)]
    outs_s = [[] for _ in range(5)]
    for l in range(depth):
        p = _prep_layer(l, n_heads, scale, w_in, lru_conv_w, lru_conv_b, lru_w_a, lru_b_a, lru_w_x, lru_b_x,
                        lru_lambda, norm1_g, q_lora_norm_g, w_uq, q_nope_norm_g, q_rope_norm_g, kv_lora_norm_g,
                        k_rope_norm_g, w_ukv, k_nope_norm_g, lru_out_norm_g, mla_out_norm_g, w_o, norm2_g, w_up,
                        ffn_conv_w, ffn_conv_b, w_down)
        mod = _mod_call(c_all, w_ada[l], b_ada[l])
        mp = [m_[:, None, :] for m_ in jnp.split(mod[:b], 6, axis=-1)]
        ms = jnp.split(mod[b:], 6, axis=-1)

        lru_p, q_p, k_p, v_p, kvlat_p, krope_p, hlast_p, xtail_p = _inproj_call(
            y_p, mp[0], mp[1], cc_p, ss_p, p, n_heads, ts)
        attn_p = _attn_call(q_p, k_p, v_p, p["mog"], n_heads, tq, tk)
        y_p, ftail_p = _ffn_call(y_p, lru_p, attn_p, mp[2], mp[3], mp[4], mp[5], p, tm)
        kc = lru_conv_w.shape[1]
        fk = ffn_conv_w.shape[1]
        for j, o in enumerate((kvlat_p, krope_p, hlast_p, xtail_p[:, 8 - (kc - 1):], ftail_p[:, 8 - (fk - 1):])):
            outs_p[j].append(o)

        cbuf = jnp.swapaxes(state_lru_conv[l], 0, 1)
        fbuf = jnp.swapaxes(state_ffn_conv[l], 0, 1)
        lru_s, xlru_s, hnew_s, qabs, qr, kvn_s, kr_s = _spre_call(
            y_s, ms[0], ms[1], cc_s, ss_s, cbuf, state_lru_h[l], p, n_heads, n_past == 0)
        pad_heads = lambda t: jnp.pad(jnp.swapaxes(t, 0, 1), ((0, 0), (0, hp - n_heads), (0, 0)))
        olat = _sattn_call(page_table, cache_kv_latent, cache_krt, l, p["wkt"], pad_heads(qabs), pad_heads(qr),
                           kvn_s[:, None, :], kr_s[:, :, None], n_heads, ppsub)
        olat = jnp.swapaxes(olat[:, :n_heads], 0, 1)
        y_s, up_s = _spost_call(y_s, lru_s, olat, ms[2], ms[3], ms[4], ms[5], fbuf, p, n_heads)
        lru_conv_new = jnp.concatenate([state_lru_conv[l][:, 1:], xlru_s[:, None, :]], axis=1)
        ffn_conv_new = jnp.concatenate([state_ffn_conv[l][:, 1:], up_s[:, None, :]], axis=1)
        for j, o in enumerate((kvn_s[:, None, :], kr_s[:, None, :], hnew_s, lru_conv_new, ffn_conv_new)):
            outs_s[j].append(o)

    return (y_p, y_s.reshape(nb, 1, d), *[jnp.stack(o) for o in outs_p], *[jnp.stack(o) for o in outs_s])
```

```python
import functools

import numpy as np
import jax
import jax.numpy as jnp
from jax import lax
from jax.experimental import pallas as pl
from jax.experimental.pallas import tpu as pltpu

f32 = jnp.float32
bf16 = jnp.bfloat16

EPS = 1e-6
LRU_C = 8.0
ROPE_THETA = 10000.0
LANES = 128
MXU_DIM = 256
VMEM_LIMIT = 56 * 1024 * 1024

_NOPE, _ROPE = 64, 32
_R1_LO, _R2_LO = 48, 112


def _h128_src():
    src = np.full((LANES,), -1, np.int32)
    src[0:48] = np.arange(0, 48)
    src[48:64] = _NOPE + np.arange(0, 16)
    src[64:80] = np.arange(48, 64)
    src[112:128] = _NOPE + 16 + np.arange(0, 16)
    return src


def _dot(a, b):
    return jnp.dot(a, b, preferred_element_type=f32)


def _dot_nt(a, b):
    return lax.dot_general(a, b, (((1,), (1,)), ((), ())), preferred_element_type=f32)


def _rms(x, g):
    ms = jnp.mean(x * x, axis=-1, keepdims=True)
    return x * lax.rsqrt(ms + EPS) * g


def _neg_expm1_2x(y):
    t = jnp.tanh(y)
    return -2.0 * t / (1.0 - t)


def _lru_gates(x_conv, wa_ref, ba, wx_ref, bx, lam):
    xb = x_conv.astype(bf16)
    ng = wa_ref.shape[0]
    ra = jnp.concatenate([_dot(xb[:, g * MXU_DIM:(g + 1) * MXU_DIM], wa_ref[g]) for g in range(ng)], axis=-1) + ba
    ia = jnp.concatenate([_dot(xb[:, g * MXU_DIM:(g + 1) * MXU_DIM], wx_ref[g]) for g in range(ng)], axis=-1) + bx
    r = jax.nn.sigmoid(ra)
    ig = jax.nn.sigmoid(ia)
    log_a = (-LRU_C) * r * jax.nn.softplus(-lam)
    a = jnp.exp(log_a)
    mult = jnp.sqrt(_neg_expm1_2x(log_a))
    return a, mult, ig * x_conv


def _group_ms(x, msel):
    return _dot((x * x).astype(bf16), msel)


def _norm_rope(x, msel, ca, sb, to3d):
    inv = lax.rsqrt(_group_ms(x, msel) + EPS)
    return to3d(inv) * (to3d(x) * ca + to3d(pltpu.roll(x, LANES // 2, 1)) * sb)


def _q_heads(q, msel, ca, sb, n_heads, to3d):
    return [_norm_rope(q[:, h * LANES:(h + 1) * LANES], msel, ca, sb, to3d) for h in range(n_heads)]


def _mod_kernel(c_ref, w_ref, b_ref, o_ref):
    c = c_ref[...]
    sc = (c * jax.nn.sigmoid(c)).astype(bf16)
    o_ref[...] = _dot(sc, w_ref[...].astype(bf16)) + b_ref[...]


def _mod_call(c_all, w_ada, b_ada):
    m, d = c_all.shape
    n = w_ada.shape[1]
    tn = 1536 if n % 1536 == 0 else n
    return pl.pallas_call(
        _mod_kernel,
        grid=(n // tn,),
        in_specs=[pl.BlockSpec((m, d), lambda j: (0, 0)),
                  pl.BlockSpec((d, tn), lambda j: (0, j)),
                  pl.BlockSpec((1, tn), lambda j: (0, j))],
        out_specs=pl.BlockSpec((m, tn), lambda j: (0, j)),
        out_shape=jax.ShapeDtypeStruct((m, n), f32),
        compiler_params=pltpu.CompilerParams(dimension_semantics=("arbitrary",), vmem_limit_bytes=VMEM_LIMIT),
        name="adaln_mod",
    )(c_all, w_ada, b_ada.reshape(1, n))


def _inproj_kernel(x_ref, sh_ref, sc_ref, qca_ref, qsb_ref, kca_ref, ksb_ref, n1g_ref, win_ref, cw_ref, cb_ref,
                   wa_ref, ba_ref, wx_ref, bx_ref, lam_ref, qlg_ref, wuq_ref, msel_ref,
                   kvg_ref, wk_ref, gk_ref, wv_ref, vone_ref, log_ref,
                   lru_o, q_o, k_o, v_o, kvlat_o, krope_o, hlast_o, xtail_o,
                   xtail_s, a_s, u_s, hs_s, h_s, *, n_heads):
    i = pl.program_id(0)
    nb, ts, d = x_ref.shape
    m = nb * ts
    w = lam_ref.shape[-1]
    ql = qlg_ref.shape[-1]
    kl = kvg_ref.shape[-1]
    kconv = cw_ref.shape[0]
    nlc = w // LANES

    @pl.when(i == 0)
    def _():
        xtail_s[...] = jnp.zeros_like(xtail_s)
        h_s[...] = jnp.zeros_like(h_s)

    def to3d(v):
        return v.reshape(nb, ts, v.shape[-1])

    x = x_ref[...]
    h = _rms(x, n1g_ref[...]) * (1.0 + sc_ref[...]) + sh_ref[...]
    z = _dot(h.reshape(m, d).astype(bf16), win_ref[...])
    x_lru = z[:, 0:w]
    g_lru = z[:, w:2 * w]
    q_lat = z[:, 2 * w:2 * w + ql]
    kv_lat = z[:, 2 * w + ql:2 * w + ql + kl]
    kr_pre = z[:, 2 * w + ql + kl:2 * w + ql + kl + LANES]

    t_idx = lax.broadcasted_iota(jnp.int32, (m, 1), 0) & (ts - 1)
    r8 = lax.broadcasted_iota(jnp.int32, (1, 8, 1), 1)
    tail2d = xtail_s[...].reshape(nb * 8, w)
    x_conv = to3d(jnp.broadcast_to(cb_ref[...], (m, w)))
    for j in range(kconv - 1):
        k = kconv - 1 - j
        rolled = to3d(pltpu.roll(x_lru, k, 0))
        prev = pltpu.roll(tail2d, nb * 8 + k - 8, 0).reshape(nb, 8, w)
        sh = jnp.concatenate([jnp.where(r8 < k, prev, rolled[:, 0:8]), rolled[:, 8:]], axis=1)
        x_conv = x_conv + sh * cw_ref[j:j + 1, :]
    x_conv = (x_conv + to3d(x_lru) * cw_ref[kconv - 1:kconv, :]).reshape(m, w)
    xtail_s[...] = to3d(x_lru)[:, ts - 8:, :]
    xtail_o[...] = to3d(x_lru)[:, ts - 8:, :]

    a, mult, ux = _lru_gates(x_conv, wa_ref, ba_ref[...], wx_ref, bx_ref[...], lam_ref[...])
    mult = jnp.where(jnp.logical_and(t_idx == 0, i == 0), 1.0, mult)
    u = mult * ux
    for j in range(nlc):
        a_s[j] = a[:, j * LANES:(j + 1) * LANES]
        u_s[j] = u[:, j * LANES:(j + 1) * LANES]

    def scan_step(t, hc):
        out = []
        for j in range(nlc):
            hj = a_s[j, pl.ds(t, nb, stride=ts), :] * hc[j] + u_s[j, pl.ds(t, nb, stride=ts), :]
            hs_s[j, pl.ds(t, nb, stride=ts), :] = hj
            out.append(hj)
        return tuple(out)

    hc = lax.fori_loop(0, ts, scan_step, tuple(h_s[j] for j in range(nlc)), unroll=8)
    for j in range(nlc):
        h_s[j] = hc[j]
    hlast_o[...] = jnp.concatenate(list(hc), axis=-1)
    hs = jnp.concatenate([hs_s[j] for j in range(nlc)], axis=-1)
    lru_out = hs * jax.nn.gelu(g_lru)
    lru_o[...] = to3d(_rms(lru_out, log_ref[...]).astype(bf16))

    msel = msel_ref[...]
    qn = _rms(q_lat, qlg_ref[...]).astype(bf16)
    q = _dot(qn, wuq_ref[...])
    for hd, o in enumerate(_q_heads(q, msel, qca_ref[...][None], qsb_ref[...][None], n_heads, to3d)):
        q_o[:, :, hd * LANES:(hd + 1) * LANES] = o.astype(bf16)

    kvn = _rms(kv_lat, kvg_ref[...])
    kvlat_o[...] = to3d(kvn)
    kr = _norm_rope(kr_pre, msel, kca_ref[...][None], ksb_ref[...][None], to3d)
    krope_o[...] = jnp.concatenate([kr[:, :, _R1_LO:_R1_LO + 16], kr[:, :, _R2_LO:_R2_LO + 16]], axis=-1)
    kvb = kvn.astype(bf16)
    kk = _dot(kvb, wk_ref[...])
    v_o[...] = to3d((_dot(kvb, wv_ref[...]) + vone_ref[...]).astype(bf16))
    gk = gk_ref[...]
    for hd in range(n_heads):
        kh = kk[:, hd * LANES:(hd + 1) * LANES]
        khn = kh * lax.rsqrt(_group_ms(kh, msel) + EPS) * gk
        k_o[:, :, hd * LANES:(hd + 1) * LANES] = (to3d(khn) + kr).astype(bf16)


def _const_spec(shape):
    nd = len(shape)
    return pl.BlockSpec(shape, lambda *_: (0,) * nd, pipeline_mode=pl.Buffered(1))


def _inproj_call(x, sh1, sc1, rope_tabs, p, n_heads, ts):
    b, s, d = x.shape
    w = p["lam"].shape[-1]
    kl = p["kvg"].shape[-1]
    hw = n_heads * LANES
    m = b * ts
    consts = [p["n1g"], p["win"], p["cw"], p["cb"], p["wa"], p["ba"], p["wx"], p["bx"], p["lam"], p["qlg"],
              p["wuq"], p["msel"], p["kvg"], p["wk"], p["gk"], p["wv"], p["vone"], p["log"]]
    in_specs = ([pl.BlockSpec((b, ts, d), lambda i: (0, i, 0)), _const_spec(sh1.shape), _const_spec(sc1.shape)]
                + [pl.BlockSpec((ts, LANES), lambda i: (i, 0))] * len(rope_tabs)
                + [_const_spec(c.shape) for c in consts])

    def tile(n, dt):
        return pl.BlockSpec((b, ts, n), lambda i: (0, i, 0)), jax.ShapeDtypeStruct((b, s, n), dt)

    outs = [tile(w, bf16), tile(hw, bf16), tile(hw, bf16), tile(hw, bf16), tile(kl, f32), tile(_ROPE, f32),
            (pl.BlockSpec((b, w), lambda i: (0, 0)), jax.ShapeDtypeStruct((b, w), f32)),
            (pl.BlockSpec((b, 8, w), lambda i: (0, 0, 0)), jax.ShapeDtypeStruct((b, 8, w), f32))]
    nlc = w // LANES
    return pl.pallas_call(
        functools.partial(_inproj_kernel, n_heads=n_heads),
        grid=(s // ts,),
        in_specs=in_specs,
        out_specs=[o[0] for o in outs],
        out_shape=[o[1] for o in outs],
        scratch_shapes=[pltpu.VMEM((b, 8, w), f32), pltpu.VMEM((nlc, m, LANES), f32), pltpu.VMEM((nlc, m, LANES), f32),
                        pltpu.VMEM((nlc, m, LANES), f32), pltpu.VMEM((nlc, b, LANES), f32)],
        compiler_params=pltpu.CompilerParams(dimension_semantics=("arbitrary",), vmem_limit_bytes=VMEM_LIMIT),
        name="prompt_inproj",
    )(x, sh1, sc1, *rope_tabs, *consts)


def _attn_kernel(q_ref, k_ref, v_ref, g_ref, o_ref, m_s, acc_s, *, n_heads, tk):
    qi = pl.program_id(1)
    tq = q_ref.shape[1]
    nmask = tq // tk
    n_full = qi * nmask
    m_s[...] = jnp.full_like(m_s, -1e30)
    acc_s[...] = jnp.zeros_like(acc_s)

    def block(j, r0, masked):
        start = pl.multiple_of(j * tk, tk)
        if masked:
            vis = (lax.broadcasted_iota(jnp.int32, (tq - r0, tk), 1)
                   <= lax.broadcasted_iota(jnp.int32, (tq - r0, tk), 0))
        for hd in range(n_heads):
            hs = slice(hd * LANES, (hd + 1) * LANES)
            s = _dot_nt(q_ref[0, r0:, hs], k_ref[0, pl.ds(start, tk), hs])
            if masked:
                s = jnp.where(vis, s, -1e30)
            mx = m_s[hd, r0:, :]
            m_new = jnp.maximum(mx, jnp.max(s, axis=-1, keepdims=True))
            alpha = jnp.exp2(mx - m_new)
            pm = jnp.exp2(s - jnp.concatenate([m_new] * (tk // LANES), axis=-1))
            acc_s[hd, r0:, :] = alpha * acc_s[hd, r0:, :] + _dot(pm.astype(bf16), v_ref[0, pl.ds(start, tk), hs])
            m_s[hd, r0:, :] = m_new

    def full_block(j, c):
        block(j, 0, False)
        return c

    lax.fori_loop(0, n_full, full_block, 0)
    for jm in range(nmask):
        block(n_full + jm, jm * tk, True)
    lane = lax.broadcasted_iota(jnp.int32, (1, LANES), 1)
    half = LANES // 2
    heads = []
    for hd in range(n_heads):
        acc = acc_s[hd]
        lo = (hd % 2) * half
        one_lane = (half - lo)
        l = jnp.sum(jnp.where(lane == one_lane, acc, 0.0), axis=-1, keepdims=True)
        heads.append(jnp.where((lane >= lo) & (lane < lo + half), acc, 0.0) / l)
    o = jnp.concatenate([heads[2 * pp] + heads[2 * pp + 1] for pp in range(n_heads // 2)], axis=-1)
    o_ref[0] = _rms(o, g_ref[...]).astype(bf16)


def _attn_call(q, k, v, g, n_heads, tq, tk):
    b, s, hw = q.shape
    wout = g.shape[-1]
    return pl.pallas_call(
        functools.partial(_attn_kernel, n_heads=n_heads, tk=tk),
        grid=(b, s // tq),
        in_specs=[pl.BlockSpec((1, tq, hw), lambda bi, qi: (bi, qi, 0)),
                  pl.BlockSpec((1, s, hw), lambda bi, qi: (bi, 0, 0)),
                  pl.BlockSpec((1, s, hw), lambda bi, qi: (bi, 0, 0)),
                  pl.BlockSpec((1, wout), lambda bi, qi: (0, 0))],
        out_specs=pl.BlockSpec((1, tq, wout), lambda bi, qi: (bi, qi, 0)),
        out_shape=jax.ShapeDtypeStruct((b, s, wout), bf16),
        scratch_shapes=[pltpu.VMEM((n_heads, tq, LANES), f32), pltpu.VMEM((n_heads, tq, LANES), f32)],
        compiler_params=pltpu.CompilerParams(dimension_semantics=("arbitrary", "arbitrary"),
                                             vmem_limit_bytes=VMEM_LIMIT),
        name="prompt_attn",
    )(q, k, v, g)


def _mix_and_norm2(x, lru, attn, wo_ref, g1, n2g, sc2, sh2):
    w = lru.shape[-1]
    mixed = _dot(lru, wo_ref[pl.ds(0, w), :]) + _dot(attn, wo_ref[pl.ds(w, attn.shape[-1]), :])
    x1 = x + g1 * mixed
    h2 = (_rms(x1, n2g) * (1.0 + sc2) + sh2).astype(bf16)
    return x1, h2


def _ffn_kernel(x_ref, lru_ref, attn_ref, g1_ref, sh2_ref, sc2_ref, g2_ref, n2g_ref, wo_ref, wup_ref,
                fcw_ref, fcb_ref, wdn_ref, y_o, tail_o, tail_s, act_s):
    si = pl.program_id(1)
    tm = x_ref.shape[1]
    nc = wdn_ref.shape[0]
    ck = wdn_ref.shape[1]
    kconv = fcw_ref.shape[1]

    @pl.when(si == 0)
    def _():
        tail_s[...] = jnp.zeros_like(tail_s)

    x1, h2 = _mix_and_norm2(x_ref[0], lru_ref[0], attn_ref[0], wo_ref, g1_ref[0], n2g_ref[...], sc2_ref[0], sh2_ref[0])
    r8 = lax.broadcasted_iota(jnp.int32, (8, 1), 0)

    def conv(up, c):
        tail = tail_s[c]
        first = up[0:8]
        out = fcb_ref[c]
        cw = fcw_ref[c]
        for j in range(kconv - 1):
            k = kconv - 1 - j
            head = jnp.where(r8 < k, pltpu.roll(tail, k, 0), pltpu.roll(first, k, 0))
            sh = jnp.concatenate([head, pltpu.roll(up, k, 0)[8:]], axis=0)
            out = out + sh * cw[j:j + 1, :]
        out = out + up * cw[kconv - 1:kconv, :]
        tail_s[c] = up[tm - 8:]
        tail_o[0, :, c * ck:(c + 1) * ck] = up[tm - 8:]
        return out

    for c in range(nc):
        val = conv(_dot(h2, wup_ref[c]), c)
        gt = conv(_dot(h2, wup_ref[nc + c]), nc + c)
        act_s[:, c * ck:(c + 1) * ck] = (jax.nn.gelu(gt) * val).astype(bf16)
    acc = _dot(act_s[...], wdn_ref[...].reshape(nc * ck, x1.shape[-1]))
    y_o[0] = x1 + g2_ref[0] * acc


def _ffn_call(x, lru, attn, g1, sh2, sc2, g2, p, tm):
    b, s, d = x.shape
    w = lru.shape[-1]
    nc, ck, _ = p["wdn"].shape
    row = lambda n: pl.BlockSpec((1, tm, n), lambda bi, si: (bi, si, 0))
    modspec = pl.BlockSpec((1, 1, d), lambda bi, si: (bi, 0, 0))
    consts = [p["n2g"], p["wo"], p["wup"], p["fcw"], p["fcb"], p["wdn"]]
    return pl.pallas_call(
        _ffn_kernel,
        grid=(b, s // tm),
        in_specs=[row(d), row(w), row(attn.shape[-1]), modspec, modspec, modspec, modspec]
        + [_const_spec(c.shape) for c in consts],
        out_specs=[row(d), pl.BlockSpec((1, 8, 2 * nc * ck), lambda bi, si: (bi, 0, 0))],
        out_shape=[jax.ShapeDtypeStruct((b, s, d), f32), jax.ShapeDtypeStruct((b, 8, 2 * nc * ck), f32)],
        scratch_shapes=[pltpu.VMEM((2 * nc, 8, ck), f32), pltpu.VMEM((tm, nc * ck), bf16)],
        compiler_params=pltpu.CompilerParams(dimension_semantics=("arbitrary", "arbitrary"),
                                             vmem_limit_bytes=VMEM_LIMIT),
        name="prompt_ffn",
    )(x, lru, attn, g1, sh2, sc2, g2, *consts)


def _spre_kernel(x_ref, sh_ref, sc_ref, qca_ref, qsb_ref, kca_ref, ksb_ref, n1g_ref, win_ref, cbuf_ref, cw_ref,
                 cb_ref, wa_ref, ba_ref, wx_ref, bx_ref, lam_ref, h0_ref, qlg_ref, wuq_ref, msel_ref,
                 kvg_ref, wk_ref, gk_ref, log_ref,
                 lru_o, xlru_o, hnew_o, qabs_o, qr_o, kvn_o, kr_o, *, n_heads, first_pos):
    w = lam_ref.shape[-1]
    ql = qlg_ref.shape[-1]
    kl = kvg_ref.shape[-1]
    kconv = cw_ref.shape[0]
    ident = lambda v: v

    h = _rms(x_ref[...], n1g_ref[...]) * (1.0 + sc_ref[...]) + sh_ref[...]
    z = _dot(h.astype(bf16), win_ref[...])
    x_lru = z[:, 0:w]
    g_lru = z[:, w:2 * w]
    q_lat = z[:, 2 * w:2 * w + ql]
    kv_lat = z[:, 2 * w + ql:2 * w + ql + kl]
    kr_pre = z[:, 2 * w + ql + kl:2 * w + ql + kl + LANES]

    x_conv = cb_ref[...]
    for j in range(kconv - 1):
        x_conv = x_conv + cbuf_ref[j] * cw_ref[j:j + 1, :]
    x_conv = x_conv + x_lru * cw_ref[kconv - 1:kconv, :]
    xlru_o[...] = x_lru
    a, mult, ux = _lru_gates(x_conv, wa_ref, ba_ref[...], wx_ref, bx_ref[...], lam_ref[...])
    if first_pos:
        mult = jnp.ones_like(mult)
    hn = a * h0_ref[...] + mult * ux
    hnew_o[...] = hn
    lru_o[...] = _rms(hn * jax.nn.gelu(g_lru), log_ref[...]).astype(bf16)

    msel = msel_ref[...]
    qn = _rms(q_lat, qlg_ref[...]).astype(bf16)
    q = _dot(qn, wuq_ref[...])
    gk = gk_ref[...]
    for hd, o in enumerate(_q_heads(q, msel, qca_ref[...], qsb_ref[...], n_heads, ident)):
        wkh = wk_ref[:, hd * LANES:(hd + 1) * LANES]
        qabs_o[hd] = _dot_nt((o * gk).astype(bf16), wkh).astype(bf16)
        qr_o[hd] = jnp.concatenate([o[:, _R1_LO:_R1_LO + 16], o[:, _R2_LO:_R2_LO + 16]], axis=-1).astype(bf16)

    kvn_o[...] = _rms(kv_lat, kvg_ref[...])
    kr = _norm_rope(kr_pre, msel, kca_ref[...], ksb_ref[...], ident)
    kr_o[...] = jnp.concatenate([kr[:, _R1_LO:_R1_LO + 16], kr[:, _R2_LO:_R2_LO + 16]], axis=-1)


def _spre_call(x, sh1, sc1, rope_tabs, cbuf, h0, p, n_heads, first_pos):
    nb, d = x.shape
    w = p["lam"].shape[-1]
    kl = p["kvg"].shape[-1]
    out_shape = [jax.ShapeDtypeStruct((nb, w), bf16), jax.ShapeDtypeStruct((nb, w), f32),
                 jax.ShapeDtypeStruct((nb, w), f32), jax.ShapeDtypeStruct((n_heads, nb, kl), bf16),
                 jax.ShapeDtypeStruct((n_heads, nb, _ROPE), bf16), jax.ShapeDtypeStruct((nb, kl), f32),
                 jax.ShapeDtypeStruct((nb, _ROPE), f32)]
    return pl.pallas_call(
        functools.partial(_spre_kernel, n_heads=n_heads, first_pos=first_pos),
        out_shape=out_shape,
        compiler_params=pltpu.CompilerParams(vmem_limit_bytes=VMEM_LIMIT),
        name="sample_inproj",
    )(x, sh1, sc1, *rope_tabs, p["n1g"], p["win"], cbuf, p["cw"], p["cb"], p["wa"], p["ba"], p["wx"], p["bx"],
      p["lam"], h0, p["qlg"], p["wuq"], p["msel"], p["kvg"], p["wk"], p["gk"], p["log"])


def _sattn_kernel(pt_ref, lat_hbm, krt_hbm, wkt_ref, qabs_ref, qr_ref, latn_ref, krn_ref, o_ref,
                  lat_buf, kr_buf, lhs_s, latb_s, s_s, sems, *, layer, n_heads, ppsub):
    b = pl.program_id(0)
    nseq = pl.num_programs(0)
    _, npg, page, kl = lat_buf.shape
    nk = wkt_ref.shape[0]
    hp = qabs_ref.shape[0]
    n_past = npg * page
    tk = ppsub * page
    slot = lax.rem(b, 2)

    def page_copies(src_page, slot_, pg):
        return (pltpu.make_async_copy(lat_hbm.at[layer, src_page], lat_buf.at[slot_, pg], sems.at[0, slot_]),
                pltpu.make_async_copy(krt_hbm.at[layer, src_page], kr_buf.at[slot_, pg], sems.at[1, slot_]))

    def start_pages(seq, slot_, pg0, n):
        for i in range(n):
            for cp in page_copies(pt_ref[seq * npg + pg0 + i], slot_, pg0 + i):
                cp.start()

    @pl.when(b == 0)
    def _():
        def body(pg, c):
            start_pages(0, 0, pg, 1)
            return c
        lax.fori_loop(0, npg, body, 0)
        lhs_s[0:nk, :] = wkt_ref[...]

    lhs_s[nk:nk + hp, :] = qabs_ref[...]

    def wait_slot(slot_):
        pltpu.make_async_copy(lat_hbm.at[layer, pl.ds(0, npg)], lat_buf.at[slot_], sems.at[0, slot_]).wait()
        pltpu.make_async_copy(krt_hbm.at[layer, pl.ds(0, npg)], kr_buf.at[slot_], sems.at[1, slot_]).wait()

    wait_slot(slot)

    def scores(latb, krt):
        n = latb.shape[0]
        big = _dot_nt(lhs_s[...], latb)
        knt = big[0:nk]
        ssq = jnp.sum((knt * knt).reshape(n_heads, _NOPE, n), axis=1)
        rs = lax.rsqrt(ssq * (1.0 / _NOPE) + EPS)
        sr = _dot(qr_ref[...], krt)
        s8 = big[nk:nk + n_heads] * rs + sr[0:n_heads]
        return jnp.concatenate([s8, jnp.zeros((hp - n_heads, n), f32)], axis=0)

    def sub(j):
        p0 = pl.multiple_of(j * ppsub, ppsub)
        k0 = pl.multiple_of(j * tk, tk)
        latb = lat_buf[slot, pl.ds(p0, ppsub)].reshape(tk, kl).astype(bf16)
        krt = jnp.concatenate([kr_buf[slot, p0 + i] for i in range(ppsub)], axis=-1).astype(bf16)
        latb_s[pl.ds(k0, tk), :] = latb
        s_s[:, pl.ds(k0, tk)] = scores(latb, krt)

    subs_per_trip = 2
    ppt = subs_per_trip * ppsub

    nxt = jnp.minimum(b + 1, nseq - 1)

    def trip(jj, c):
        start_pages(nxt, 1 - slot, jj * ppt, ppt)
        for u in range(subs_per_trip):
            sub(jj * subs_per_trip + u)
        return c

    lax.fori_loop(0, npg // ppt, trip, 0)

    @pl.when(b == nseq - 1)
    def _():
        wait_slot(1 - slot)

    latn = jnp.broadcast_to(latn_ref[...], (page, kl)).astype(bf16)
    krn = jnp.broadcast_to(krn_ref[...], (krn_ref.shape[0], page)).astype(bf16)
    s_new = scores(latn, krn)
    s_s[:, n_past:n_past + page] = jnp.where(lax.broadcasted_iota(jnp.int32, s_new.shape, 1) == 0, s_new, -1e30)
    latb_s[n_past:n_past + page, :] = latn

    s = s_s[...]
    pm = jnp.exp2(s - jnp.max(s, axis=-1, keepdims=True))
    l = jnp.sum(pm, axis=-1, keepdims=True)
    o_ref[...] = _dot(pm.astype(bf16), latb_s[...]) / l


def _sattn_call(page_table, cache_lat, cache_krt, layer, wkt, qabs, qr, latn, krn, n_heads, ppsub):
    nb, npg = page_table.shape
    _, _, page, kl = cache_lat.shape
    rd = cache_krt.shape[2]
    hp = qabs.shape[1]
    nk = wkt.shape[0]
    nkeys = npg * page + page
    assert npg % (2 * ppsub) == 0
    grid_spec = pltpu.PrefetchScalarGridSpec(
        num_scalar_prefetch=1,
        grid=(nb,),
        in_specs=[pl.BlockSpec(memory_space=pl.ANY),
                  pl.BlockSpec(memory_space=pl.ANY),
                  pl.BlockSpec((nk, kl), lambda bi, pt: (0, 0)),
                  pl.BlockSpec((None, hp, kl), lambda bi, pt: (bi, 0, 0)),
                  pl.BlockSpec((None, hp, rd), lambda bi, pt: (bi, 0, 0)),
                  pl.BlockSpec((None, 1, kl), lambda bi, pt: (bi, 0, 0)),
                  pl.BlockSpec((None, rd, 1), lambda bi, pt: (bi, 0, 0))],
        out_specs=pl.BlockSpec((None, hp, kl), lambda bi, pt: (bi, 0, 0)),
        scratch_shapes=[pltpu.VMEM((2, npg, page, kl), f32), pltpu.VMEM((2, npg, rd, page), f32),
                        pltpu.VMEM((nk + hp, kl), bf16), pltpu.VMEM((nkeys, kl), bf16),
                        pltpu.VMEM((hp, nkeys), f32), pltpu.SemaphoreType.DMA((2, 2))])
    return pl.pallas_call(
        functools.partial(_sattn_kernel, layer=layer, n_heads=n_heads, ppsub=ppsub),
        grid_spec=grid_spec,
        out_shape=jax.ShapeDtypeStruct((nb, hp, kl), f32),
        compiler_params=pltpu.CompilerParams(dimension_semantics=("arbitrary",), vmem_limit_bytes=VMEM_LIMIT),
        name="sample_attn",
    )(page_table.reshape(-1), cache_lat, cache_krt, wkt, qabs, qr, latn, krn)


def _spost_kernel(x_ref, lru_ref, olat_ref, g1_ref, sh2_ref, sc2_ref, g2_ref, mog_ref, n2g_ref, wv_ref, wo_ref,
                  wup_ref, fcw_ref, fcb_ref, wdn_ref, fbuf_ref, y_o, up_o, *, n_heads):
    nc = wdn_ref.shape[0]
    ck = wdn_ref.shape[1]
    kconv = fcw_ref.shape[1]
    heads = [_dot(olat_ref[hd].astype(bf16), wv_ref[:, hd * LANES:(hd + 1) * LANES]) for hd in range(n_heads)]
    attn = jnp.concatenate([heads[2 * pp] + heads[2 * pp + 1] for pp in range(n_heads // 2)], axis=-1)
    attn = _rms(attn, mog_ref[...]).astype(bf16)
    x1, h2 = _mix_and_norm2(x_ref[...], lru_ref[...], attn, wo_ref, g1_ref[...], n2g_ref[...], sc2_ref[...], sh2_ref[...])

    def conv(up, c):
        out = fcb_ref[c]
        cw = fcw_ref[c]
        for j in range(kconv - 1):
            out = out + fbuf_ref[j, :, c * ck:(c + 1) * ck] * cw[j:j + 1, :]
        up_o[:, c * ck:(c + 1) * ck] = up
        return out + up * cw[kconv - 1:kconv, :]

    acc = jnp.zeros(x1.shape, f32)
    for c in range(nc):
        val = conv(_dot(h2, wup_ref[c]), c)
        gt = conv(_dot(h2, wup_ref[nc + c]), nc + c)
        acc = acc + _dot((jax.nn.gelu(gt) * val).astype(bf16), wdn_ref[c])
    y_o[...] = x1 + g2_ref[...] * acc


def _spost_call(x, lru, olat, g1, sh2, sc2, g2, fbuf, p, n_heads):
    nb, d = x.shape
    nc, ck, _ = p["wdn"].shape
    return pl.pallas_call(
        functools.partial(_spost_kernel, n_heads=n_heads),
        out_shape=[jax.ShapeDtypeStruct((nb, d), f32), jax.ShapeDtypeStruct((nb, 2 * nc * ck), f32)],
        compiler_params=pltpu.CompilerParams(vmem_limit_bytes=VMEM_LIMIT),
        name="sample_ffn",
    )(x, lru, olat, g1, sh2, sc2, g2, p["mog"], p["n2g"], p["wv"], p["wo"], p["wup"], p["fcw"], p["fcb"], p["wdn"], fbuf)


def _take_cols(wmat, idx):
    padded = jnp.concatenate([wmat, jnp.zeros(wmat.shape[:-1] + (1,), wmat.dtype)], axis=-1)
    return jnp.take(padded, jnp.asarray(np.where(idx < 0, wmat.shape[-1], idx)), axis=-1)


def _block_diag_groups(wh):
    nh, hd, _ = wh.shape
    per = MXU_DIM // hd
    groups = []
    for g in range(nh // per):
        blk = jnp.zeros((MXU_DIM, MXU_DIM), wh.dtype)
        for j in range(per):
            blk = lax.dynamic_update_slice(blk, wh[g * per + j], (j * hd, j * hd))
        groups.append(blk)
    return jnp.stack(groups).astype(bf16)


def _prep_layer(l, n_heads, scale, w_in, lru_conv_w, lru_conv_b, lru_w_a, lru_b_a, lru_w_x, lru_b_x, lru_lambda,
                norm1_g, q_lora_norm_g, w_uq, q_nope_norm_g, q_rope_norm_g, kv_lora_norm_g, k_rope_norm_g, w_ukv,
                k_nope_norm_g, lru_out_norm_g, mla_out_norm_g, w_o, norm2_g, w_up, ffn_conv_w, ffn_conv_b, w_down):
    src = _h128_src()
    w = lru_lambda.shape[-1]
    ql = q_lora_norm_g.shape[-1]
    kl = kv_lora_norm_g.shape[-1]
    qk = _NOPE + _ROPE
    vd = w_ukv.shape[-1] // n_heads - _NOPE
    row = lambda v: v.reshape(1, -1).astype(f32)

    kr_src = np.where(src >= _NOPE, src - _NOPE, -1)
    win = w_in[l]
    base = 2 * w + ql + kl
    win_ext = jnp.concatenate([win[:, :base], _take_cols(win[:, base:], kr_src)], axis=-1).astype(bf16)

    q_idx = np.concatenate([np.where(src >= 0, src + h * qk, -1) for h in range(n_heads)])
    wuq = _take_cols(w_uq[l], q_idx).astype(bf16)
    k_src = np.where((src >= 0) & (src < _NOPE), src, -1)
    k_idx = np.concatenate([np.where(k_src >= 0, k_src + h * (_NOPE + vd), -1) for h in range(n_heads)])
    wk = _take_cols(w_ukv[l], k_idx).astype(bf16)
    v_idx = []
    vone = np.zeros((1, n_heads * LANES), np.float32)
    for h in range(n_heads):
        slab = np.full((LANES,), -1, np.int64)
        off = (h % 2) * vd
        slab[off:off + vd] = h * (_NOPE + vd) + _NOPE + np.arange(vd)
        v_idx.append(slab)
        vone[0, h * LANES + (vd - off)] = 1.0
    wv = _take_cols(w_ukv[l], np.concatenate(v_idx)).astype(bf16)
    wkt_idx = np.concatenate([h * (_NOPE + vd) + np.arange(_NOPE) for h in range(n_heads)])
    wkt = jnp.take(w_ukv[l], jnp.asarray(wkt_idx), axis=-1).T.astype(bf16)

    nope_tab = lambda g: _take_cols(g.reshape(1, -1), k_src)
    rope_tab = lambda g: _take_cols(g.reshape(1, -1), kr_src)
    is_n = (k_src >= 0).astype(np.float32)
    is_r = (kr_src >= 0).astype(np.float32)
    msel = jnp.asarray(np.outer(is_n, is_n) / _NOPE + np.outer(is_r, is_r) / _ROPE).astype(bf16)
    gq = (nope_tab(q_nope_norm_g[l]) + rope_tab(q_rope_norm_g[l])) * scale

    dff = w_down.shape[1]
    ck = MXU_DIM
    nc = dff // ck
    wup = w_up[l].reshape(w_up.shape[1], 2 * nc, ck).transpose(1, 0, 2).astype(bf16)
    fcw = ffn_conv_w[l].reshape(-1, 2 * nc, ck).transpose(1, 0, 2).astype(f32)
    fcb = ffn_conv_b[l].reshape(2 * nc, 1, ck).astype(f32)
    wdn = w_down[l].reshape(nc, ck, -1).astype(bf16)
    return dict(
        n1g=row(norm1_g[l]), win=win_ext, cw=lru_conv_w[l].astype(f32), cb=row(lru_conv_b[l]),
        wa=_block_diag_groups(lru_w_a[l]), ba=row(lru_b_a[l]), wx=_block_diag_groups(lru_w_x[l]), bx=row(lru_b_x[l]),
        lam=row(lru_lambda[l]), qlg=row(q_lora_norm_g[l]), wuq=wuq, gq=gq.astype(f32), msel=msel,
        kvg=row(kv_lora_norm_g[l]), gkr=rope_tab(k_rope_norm_g[l]).astype(f32), wk=wk,
        gk=nope_tab(k_nope_norm_g[l]).astype(f32), wv=wv, vone=jnp.asarray(vone), wkt=wkt, log=row(lru_out_norm_g[l]),
        mog=row(mla_out_norm_g[l]), wo=w_o[l].astype(bf16), n2g=row(norm2_g[l]), wup=wup, fcw=fcw, fcb=fcb, wdn=wdn)


def _rope_tables(pos):
    half = _ROPE // 2
    inv = ROPE_THETA ** (-jnp.arange(0, _ROPE, 2, dtype=f32) / _ROPE)
    ang = pos.astype(f32)[:, None] * inv[None, :]
    cos, sin = jnp.cos(ang), jnp.sin(ang)
    n = pos.shape[0]
    cc = jnp.zeros((n, LANES), f32)
    cc = cc.at[:, 0:_R1_LO].set(1.0).at[:, _R1_LO + half:_R1_LO + half + 16].set(1.0)
    cc = cc.at[:, _R1_LO:_R1_LO + half].set(cos).at[:, _R2_LO:_R2_LO + half].set(cos)
    ss = jnp.zeros((n, LANES), f32)
    ss = ss.at[:, _R1_LO:_R1_LO + half].set(-sin).at[:, _R2_LO:_R2_LO + half].set(sin)
    return cc, ss


def _gain_rope_tables(cc, ss, gq, gkr):
    swap = lambda g: jnp.roll(g, LANES // 2, axis=-1)
    return cc * gq, ss * swap(gq), cc * gkr, ss * swap(gkr)


def kernel(x_prompt, x_sample, cache_kv_latent, cache_k_rope, state_lru_h, state_lru_conv, state_ffn_conv,
           page_table, c_prompt, c_sample, w_ada, b_ada, norm1_g, w_in, lru_conv_w, lru_conv_b, lru_w_a, lru_b_a,
           lru_w_x, lru_b_x, lru_lambda, q_lora_norm_g, w_uq, q_nope_norm_g, q_rope_norm_g, kv_lora_norm_g,
           k_rope_norm_g, w_ukv, k_nope_norm_g, lru_out_norm_g, mla_out_norm_g, w_o, norm2_g, w_up, ffn_conv_w,
           ffn_conv_b, w_down):
    b, s, d = x_prompt.shape
    nb, ds, _ = x_sample.shape
    depth = w_in.shape[0]
    assert ds == 1 and q_nope_norm_g.shape[-1] == _NOPE and q_rope_norm_g.shape[-1] == _ROPE
    n_heads = w_uq.shape[-1] // (_NOPE + _ROPE)
    scale = float(_NOPE + _ROPE) ** -0.5 * float(np.log2(np.e))
    npg = page_table.shape[1]
    n_past = npg * cache_kv_latent.shape[2]
    ts = 64 if s % 64 == 0 else 32
    tq = min(512, s)
    tk = min(256, s)
    tm = min(512, s)
    ppsub = 8 if npg % 16 == 0 else 1
    hp = 16

    cache_krt = jnp.swapaxes(cache_k_rope, 2, 3)
    cc_p, ss_p = _rope_tables(jnp.arange(s))
    cc_s, ss_s = _rope_tables(n_past + jnp.arange(1))

    y_p = x_prompt
    y_s = x_sample.reshape(nb, d)
    c_all = jnp.concatenate([c_prompt, c_sample], axis=0)
    outs_p = [[] for _ in range(5)]
    outs_s = [[] for _ in range(5)]
    for l in range(depth):
        p = _prep_layer(l, n_heads, scale, w_in, lru_conv_w, lru_conv_b, lru_w_a, lru_b_a, lru_w_x, lru_b_x,
                        lru_lambda, norm1_g, q_lora_norm_g, w_uq, q_nope_norm_g, q_rope_norm_g, kv_lora_norm_g,
                        k_rope_norm_g, w_ukv, k_nope_norm_g, lru_out_norm_g, mla_out_norm_g, w_o, norm2_g, w_up,
                        ffn_conv_w, ffn_conv_b, w_down)
        mod = _mod_call(c_all, w_ada[l], b_ada[l])
        mp = [m_[:, None, :] for m_ in jnp.split(mod[:b], 6, axis=-1)]
        ms = jnp.split(mod[b:], 6, axis=-1)

        lru_p, q_p, k_p, v_p, kvlat_p, krope_p, hlast_p, xtail_p = _inproj_call(
            y_p, mp[0], mp[1], _gain_rope_tables(cc_p, ss_p, p["gq"], p["gkr"]), p, n_heads, ts)
        attn_p = _attn_call(q_p, k_p, v_p, p["mog"], n_heads, tq, tk)
        y_p, ftail_p = _ffn_call(y_p, lru_p, attn_p, mp[2], mp[3], mp[4], mp[5], p, tm)
        kc = lru_conv_w.shape[1]
        fk = ffn_conv_w.shape[1]
        for j, o in enumerate((kvlat_p, krope_p, hlast_p, xtail_p[:, 8 - (kc - 1):], ftail_p[:, 8 - (fk - 1):])):
            outs_p[j].append(o)

        cbuf = jnp.swapaxes(state_lru_conv[l], 0, 1)
        fbuf = jnp.swapaxes(state_ffn_conv[l], 0, 1)
        lru_s, xlru_s, hnew_s, qabs, qr, kvn_s, kr_s = _spre_call(
            y_s, ms[0], ms[1], _gain_rope_tables(cc_s, ss_s, p["gq"], p["gkr"]), cbuf, state_lru_h[l], p, n_heads,
            n_past == 0)
        pad_heads = lambda t: jnp.pad(jnp.swapaxes(t, 0, 1), ((0, 0), (0, hp - n_heads), (0, 0)))
        olat = _sattn_call(page_table, cache_kv_latent, cache_krt, l, p["wkt"], pad_heads(qabs), pad_heads(qr),
                           kvn_s[:, None, :], kr_s[:, :, None], n_heads, ppsub)
        olat = jnp.swapaxes(olat[:, :n_heads], 0, 1)
        y_s, up_s = _spost_call(y_s, lru_s, olat, ms[2], ms[3], ms[4], ms[5], fbuf, p, n_heads)
        lru_conv_new = jnp.concatenate([state_lru_conv[l][:, 1:], xlru_s[:, None, :]], axis=1)
        ffn_conv_new = jnp.concatenate([state_ffn_conv[l][:, 1:], up_s[:, None, :]], axis=1)
        for j, o in enumerate((kvn_s[:, None, :], kr_s[:, None, :], hnew_s, lru_conv_new, ffn_conv_new)):
            outs_s[j].append(o)

    return (y_p, y_s.reshape(nb, 1, d), *[jnp.stack(o) for o in outs_p], *[jnp.stack(o) for o in outs_s])
```

```python
import functools

import numpy as np
import jax
import jax.numpy as jnp
from jax import lax
from jax.experimental import pallas as pl
from jax.experimental.pallas import tpu as pltpu

f32 = jnp.float32
bf16 = jnp.bfloat16

EPS = 1e-6
LRU_C = 8.0
ROPE_THETA = 10000.0
LANES = 128
MXU_DIM = 256
VMEM_LIMIT = 56 * 1024 * 1024

_NOPE, _ROPE = 64, 32
_R1_LO, _R2_LO = 48, 112


def _h128_src():
    src = np.full((LANES,), -1, np.int32)
    src[0:48] = np.arange(0, 48)
    src[48:64] = _NOPE + np.arange(0, 16)
    src[64:80] = np.arange(48, 64)
    src[112:128] = _NOPE + 16 + np.arange(0, 16)
    return src


def _dot(a, b):
    return jnp.dot(a, b, preferred_element_type=f32)


def _dot_nt(a, b):
    return lax.dot_general(a, b, (((1,), (1,)), ((), ())), preferred_element_type=f32)


def _rms(x, g):
    ms = jnp.mean(x * x, axis=-1, keepdims=True)
    return x * lax.rsqrt(ms + EPS) * g


def _neg_expm1_2x(y):
    t = jnp.tanh(y)
    return -2.0 * t / (1.0 - t)


def _lru_gates(x_conv, wa_ref, ba, wx_ref, bx, lam):
    xb = x_conv.astype(bf16)
    ng = wa_ref.shape[0]
    ra = jnp.concatenate([_dot(xb[:, g * MXU_DIM:(g + 1) * MXU_DIM], wa_ref[g]) for g in range(ng)], axis=-1) + ba
    ia = jnp.concatenate([_dot(xb[:, g * MXU_DIM:(g + 1) * MXU_DIM], wx_ref[g]) for g in range(ng)], axis=-1) + bx
    r = jax.nn.sigmoid(ra)
    ig = jax.nn.sigmoid(ia)
    log_a = (-LRU_C) * r * jax.nn.softplus(-lam)
    a = jnp.exp(log_a)
    mult = jnp.sqrt(_neg_expm1_2x(log_a))
    return a, mult, ig * x_conv


def _group_ms(x, msel):
    return _dot((x * x).astype(bf16), msel)


def _norm_rope(x, msel, ca, sb, to3d):
    inv = lax.rsqrt(_group_ms(x, msel) + EPS)
    return to3d(inv) * (to3d(x) * ca + to3d(pltpu.roll(x, LANES // 2, 1)) * sb)


def _q_heads(q, msel, ca, sb, n_heads, to3d):
    return [_norm_rope(q[:, h * LANES:(h + 1) * LANES], msel, ca, sb, to3d) for h in range(n_heads)]


def _mod_kernel(c_ref, w_ref, b_ref, o_ref):
    c = c_ref[...]
    sc = (c * jax.nn.sigmoid(c)).astype(bf16)
    o_ref[...] = _dot(sc, w_ref[...].astype(bf16)) + b_ref[...]


def _mod_call(c_all, w_ada, b_ada):
    m, d = c_all.shape
    n = w_ada.shape[1]
    tn = 1536 if n % 1536 == 0 else n
    return pl.pallas_call(
        _mod_kernel,
        grid=(n // tn,),
        in_specs=[pl.BlockSpec((m, d), lambda j: (0, 0)),
                  pl.BlockSpec((d, tn), lambda j: (0, j)),
                  pl.BlockSpec((1, tn), lambda j: (0, j))],
        out_specs=pl.BlockSpec((m, tn), lambda j: (0, j)),
        out_shape=jax.ShapeDtypeStruct((m, n), f32),
        compiler_params=pltpu.CompilerParams(dimension_semantics=("arbitrary",), vmem_limit_bytes=VMEM_LIMIT),
        name="adaln_mod",
    )(c_all, w_ada, b_ada.reshape(1, n))


def _inproj_kernel(x_ref, sh_ref, sc_ref, qca_ref, qsb_ref, kca_ref, ksb_ref, n1g_ref, win_ref, cw_ref, cb_ref,
                   wa_ref, ba_ref, wx_ref, bx_ref, lam_ref, qlg_ref, wuq_ref, msel_ref,
                   kvg_ref, wk_ref, gk_ref, wv_ref, vone_ref, log_ref,
                   lru_o, q_o, k_o, v_o, kvlat_o, krope_o, hlast_o, xtail_o,
                   xtail_s, a_s, u_s, hs_s, h_s, *, n_heads):
    i = pl.program_id(0)
    nb, ts, d = x_ref.shape
    m = nb * ts
    w = lam_ref.shape[-1]
    ql = qlg_ref.shape[-1]
    kl = kvg_ref.shape[-1]
    kconv = cw_ref.shape[0]
    nlc = w // LANES

    @pl.when(i == 0)
    def _():
        xtail_s[...] = jnp.zeros_like(xtail_s)
        h_s[...] = jnp.zeros_like(h_s)

    def to3d(v):
        return v.reshape(nb, ts, v.shape[-1])

    x = x_ref[...]
    h = _rms(x, n1g_ref[...]) * (1.0 + sc_ref[...]) + sh_ref[...]
    z = _dot(h.reshape(m, d).astype(bf16), win_ref[...])
    x_lru = z[:, 0:w]
    g_lru = z[:, w:2 * w]
    q_lat = z[:, 2 * w:2 * w + ql]
    kv_lat = z[:, 2 * w + ql:2 * w + ql + kl]
    kr_pre = z[:, 2 * w + ql + kl:2 * w + ql + kl + LANES]

    t_idx = lax.broadcasted_iota(jnp.int32, (m, 1), 0) & (ts - 1)
    r8 = lax.broadcasted_iota(jnp.int32, (1, 8, 1), 1)
    tail2d = xtail_s[...].reshape(nb * 8, w)
    x_conv = to3d(jnp.broadcast_to(cb_ref[...], (m, w)))
    for j in range(kconv - 1):
        k = kconv - 1 - j
        rolled = to3d(pltpu.roll(x_lru, k, 0))
        prev = pltpu.roll(tail2d, nb * 8 + k - 8, 0).reshape(nb, 8, w)
        sh = jnp.concatenate([jnp.where(r8 < k, prev, rolled[:, 0:8]), rolled[:, 8:]], axis=1)
        x_conv = x_conv + sh * cw_ref[j:j + 1, :]
    x_conv = (x_conv + to3d(x_lru) * cw_ref[kconv - 1:kconv, :]).reshape(m, w)
    xtail_s[...] = to3d(x_lru)[:, ts - 8:, :]
    xtail_o[...] = to3d(x_lru)[:, ts - 8:, :]

    a, mult, ux = _lru_gates(x_conv, wa_ref, ba_ref[...], wx_ref, bx_ref[...], lam_ref[...])
    mult = jnp.where(jnp.logical_and(t_idx == 0, i == 0), 1.0, mult)
    u = mult * ux
    pitch = a_s.shape[1] // nb
    for j in range(nlc):
        for bi in range(nb):
            a_s[j, bi * pitch:bi * pitch + ts, :] = a[bi * ts:(bi + 1) * ts, j * LANES:(j + 1) * LANES]
            u_s[j, bi * pitch:bi * pitch + ts, :] = u[bi * ts:(bi + 1) * ts, j * LANES:(j + 1) * LANES]

    def scan_step(t, hc):
        out = []
        for j in range(nlc):
            hj = a_s[j, pl.ds(t, nb, stride=pitch), :] * hc[j] + u_s[j, pl.ds(t, nb, stride=pitch), :]
            hs_s[j, pl.ds(t, nb, stride=pitch), :] = hj
            out.append(hj)
        return tuple(out)

    hc = lax.fori_loop(0, ts, scan_step, tuple(h_s[j] for j in range(nlc)), unroll=8)
    for j in range(nlc):
        h_s[j] = hc[j]
    hlast_o[...] = jnp.concatenate(list(hc), axis=-1)
    hs = jnp.concatenate(
        [jnp.concatenate([hs_s[j, bi * pitch:bi * pitch + ts, :] for bi in range(nb)], axis=0) for j in range(nlc)],
        axis=-1)
    lru_out = hs * jax.nn.gelu(g_lru)
    lru_o[...] = to3d(_rms(lru_out, log_ref[...]).astype(bf16))

    msel = msel_ref[...]
    qn = _rms(q_lat, qlg_ref[...]).astype(bf16)
    q = _dot(qn, wuq_ref[...])
    for hd, o in enumerate(_q_heads(q, msel, qca_ref[...][None], qsb_ref[...][None], n_heads, to3d)):
        q_o[:, :, hd * LANES:(hd + 1) * LANES] = o.astype(bf16)

    kvn = _rms(kv_lat, kvg_ref[...])
    kvlat_o[...] = to3d(kvn)
    kr = _norm_rope(kr_pre, msel, kca_ref[...][None], ksb_ref[...][None], to3d)
    krope_o[...] = jnp.concatenate([kr[:, :, _R1_LO:_R1_LO + 16], kr[:, :, _R2_LO:_R2_LO + 16]], axis=-1)
    kvb = kvn.astype(bf16)
    kk = _dot(kvb, wk_ref[...])
    v_o[...] = to3d((_dot(kvb, wv_ref[...]) + vone_ref[...]).astype(bf16))
    gk = gk_ref[...]
    for hd in range(n_heads):
        kh = kk[:, hd * LANES:(hd + 1) * LANES]
        khn = kh * lax.rsqrt(_group_ms(kh, msel) + EPS) * gk
        k_o[:, :, hd * LANES:(hd + 1) * LANES] = (to3d(khn) + kr).astype(bf16)


def _const_spec(shape):
    nd = len(shape)
    return pl.BlockSpec(shape, lambda *_: (0,) * nd, pipeline_mode=pl.Buffered(1))


def _inproj_call(x, sh1, sc1, rope_tabs, p, n_heads, ts):
    b, s, d = x.shape
    w = p["lam"].shape[-1]
    kl = p["kvg"].shape[-1]
    hw = n_heads * LANES
    m = b * ts
    consts = [p["n1g"], p["win"], p["cw"], p["cb"], p["wa"], p["ba"], p["wx"], p["bx"], p["lam"], p["qlg"],
              p["wuq"], p["msel"], p["kvg"], p["wk"], p["gk"], p["wv"], p["vone"], p["log"]]
    in_specs = ([pl.BlockSpec((b, ts, d), lambda i: (0, i, 0)), _const_spec(sh1.shape), _const_spec(sc1.shape)]
                + [pl.BlockSpec((ts, LANES), lambda i: (i, 0))] * len(rope_tabs)
                + [_const_spec(c.shape) for c in consts])

    def tile(n, dt):
        return pl.BlockSpec((b, ts, n), lambda i: (0, i, 0)), jax.ShapeDtypeStruct((b, s, n), dt)

    outs = [tile(w, bf16), tile(hw, bf16), tile(hw, bf16), tile(hw, bf16), tile(kl, f32), tile(_ROPE, f32),
            (pl.BlockSpec((b, w), lambda i: (0, 0)), jax.ShapeDtypeStruct((b, w), f32)),
            (pl.BlockSpec((b, 8, w), lambda i: (0, 0, 0)), jax.ShapeDtypeStruct((b, 8, w), f32))]
    nlc = w // LANES
    return pl.pallas_call(
        functools.partial(_inproj_kernel, n_heads=n_heads),
        grid=(s // ts,),
        in_specs=in_specs,
        out_specs=[o[0] for o in outs],
        out_shape=[o[1] for o in outs],
        scratch_shapes=[pltpu.VMEM((b, 8, w), f32)] + [pltpu.VMEM((nlc, b * (ts + 8), LANES), f32)] * 3
        + [pltpu.VMEM((nlc, b, LANES), f32)],
        compiler_params=pltpu.CompilerParams(dimension_semantics=("arbitrary",), vmem_limit_bytes=VMEM_LIMIT),
        name="prompt_inproj",
    )(x, sh1, sc1, *rope_tabs, *consts)


def _attn_kernel(q_ref, k_ref, v_ref, g_ref, o_ref, m_s, acc_s, *, n_heads, tk):
    qi = pl.program_id(1)
    tq = q_ref.shape[1]
    nmask = tq // tk
    n_full = qi * nmask
    m_s[...] = jnp.full_like(m_s, -1e30)
    acc_s[...] = jnp.zeros_like(acc_s)

    def block(j, r0, masked):
        start = pl.multiple_of(j * tk, tk)
        if masked:
            vis = (lax.broadcasted_iota(jnp.int32, (tq - r0, tk), 1)
                   <= lax.broadcasted_iota(jnp.int32, (tq - r0, tk), 0))
        for hd in range(n_heads):
            hs = slice(hd * LANES, (hd + 1) * LANES)
            s = _dot_nt(q_ref[0, r0:, hs], k_ref[0, pl.ds(start, tk), hs])
            if masked:
                s = jnp.where(vis, s, -1e30)
            mx = m_s[hd, r0:, :]
            m_new = jnp.maximum(mx, jnp.max(s, axis=-1, keepdims=True))
            alpha = jnp.exp2(mx - m_new)
            pm = jnp.exp2(s - jnp.concatenate([m_new] * (tk // LANES), axis=-1))
            acc_s[hd, r0:, :] = alpha * acc_s[hd, r0:, :] + _dot(pm.astype(bf16), v_ref[0, pl.ds(start, tk), hs])
            m_s[hd, r0:, :] = m_new

    def full_block(j, c):
        block(j, 0, False)
        return c

    lax.fori_loop(0, n_full, full_block, 0)
    for jm in range(nmask):
        block(n_full + jm, jm * tk, True)
    lane = lax.broadcasted_iota(jnp.int32, (1, LANES), 1)
    half = LANES // 2
    heads = []
    for hd in range(n_heads):
        acc = acc_s[hd]
        lo = (hd % 2) * half
        one_lane = (half - lo)
        l = jnp.sum(jnp.where(lane == one_lane, acc, 0.0), axis=-1, keepdims=True)
        heads.append(jnp.where((lane >= lo) & (lane < lo + half), acc, 0.0) / l)
    o = jnp.concatenate([heads[2 * pp] + heads[2 * pp + 1] for pp in range(n_heads // 2)], axis=-1)
    o_ref[0] = _rms(o, g_ref[...]).astype(bf16)


def _attn_call(q, k, v, g, n_heads, tq, tk):
    b, s, hw = q.shape
    wout = g.shape[-1]
    return pl.pallas_call(
        functools.partial(_attn_kernel, n_heads=n_heads, tk=tk),
        grid=(b, s // tq),
        in_specs=[pl.BlockSpec((1, tq, hw), lambda bi, qi: (bi, qi, 0)),
                  pl.BlockSpec((1, s, hw), lambda bi, qi: (bi, 0, 0)),
                  pl.BlockSpec((1, s, hw), lambda bi, qi: (bi, 0, 0)),
                  pl.BlockSpec((1, wout), lambda bi, qi: (0, 0))],
        out_specs=pl.BlockSpec((1, tq, wout), lambda bi, qi: (bi, qi, 0)),
        out_shape=jax.ShapeDtypeStruct((b, s, wout), bf16),
        scratch_shapes=[pltpu.VMEM((n_heads, tq, LANES), f32), pltpu.VMEM((n_heads, tq, LANES), f32)],
        compiler_params=pltpu.CompilerParams(dimension_semantics=("arbitrary", "arbitrary"),
                                             vmem_limit_bytes=VMEM_LIMIT),
        name="prompt_attn",
    )(q, k, v, g)


def _mix_and_norm2(x, lru, attn, wo_ref, g1, n2g, sc2, sh2):
    w = lru.shape[-1]
    mixed = _dot(lru, wo_ref[pl.ds(0, w), :]) + _dot(attn, wo_ref[pl.ds(w, attn.shape[-1]), :])
    x1 = x + g1 * mixed
    h2 = (_rms(x1, n2g) * (1.0 + sc2) + sh2).astype(bf16)
    return x1, h2


def _ffn_kernel(x_ref, lru_ref, attn_ref, g1_ref, sh2_ref, sc2_ref, g2_ref, n2g_ref, wo_ref, wup_ref,
                fcw_ref, fcb_ref, wdn_ref, y_o, tail_o, tail_s, act_s):
    si = pl.program_id(1)
    tm = x_ref.shape[1]
    nc = wdn_ref.shape[0]
    ck = wdn_ref.shape[1]
    kconv = fcw_ref.shape[1]

    @pl.when(si == 0)
    def _():
        tail_s[...] = jnp.zeros_like(tail_s)

    x1, h2 = _mix_and_norm2(x_ref[0], lru_ref[0], attn_ref[0], wo_ref, g1_ref[0], n2g_ref[...], sc2_ref[0], sh2_ref[0])
    r8 = lax.broadcasted_iota(jnp.int32, (8, 1), 0)

    def conv(up, c):
        tail = tail_s[c]
        first = up[0:8]
        out = fcb_ref[c]
        cw = fcw_ref[c]
        for j in range(kconv - 1):
            k = kconv - 1 - j
            head = jnp.where(r8 < k, pltpu.roll(tail, k, 0), pltpu.roll(first, k, 0))
            sh = jnp.concatenate([head, pltpu.roll(up, k, 0)[8:]], axis=0)
            out = out + sh * cw[j:j + 1, :]
        out = out + up * cw[kconv - 1:kconv, :]
        tail_s[c] = up[tm - 8:]
        tail_o[0, :, c * ck:(c + 1) * ck] = up[tm - 8:]
        return out

    for c in range(nc):
        val = conv(_dot(h2, wup_ref[c]), c)
        gt = conv(_dot(h2, wup_ref[nc + c]), nc + c)
        act_s[:, c * ck:(c + 1) * ck] = (jax.nn.gelu(gt) * val).astype(bf16)
    acc = _dot(act_s[...], wdn_ref[...].reshape(nc * ck, x1.shape[-1]))
    y_o[0] = x1 + g2_ref[0] * acc


def _ffn_call(x, lru, attn, g1, sh2, sc2, g2, p, tm):
    b, s, d = x.shape
    w = lru.shape[-1]
    nc, ck, _ = p["wdn"].shape
    row = lambda n: pl.BlockSpec((1, tm, n), lambda bi, si: (bi, si, 0))
    modspec = pl.BlockSpec((1, 1, d), lambda bi, si: (bi, 0, 0))
    consts = [p["n2g"], p["wo"], p["wup"], p["fcw"], p["fcb"], p["wdn"]]
    return pl.pallas_call(
        _ffn_kernel,
        grid=(b, s // tm),
        in_specs=[row(d), row(w), row(attn.shape[-1]), modspec, modspec, modspec, modspec]
        + [_const_spec(c.shape) for c in consts],
        out_specs=[row(d), pl.BlockSpec((1, 8, 2 * nc * ck), lambda bi, si: (bi, 0, 0))],
        out_shape=[jax.ShapeDtypeStruct((b, s, d), f32), jax.ShapeDtypeStruct((b, 8, 2 * nc * ck), f32)],
        scratch_shapes=[pltpu.VMEM((2 * nc, 8, ck), f32), pltpu.VMEM((tm, nc * ck), bf16)],
        compiler_params=pltpu.CompilerParams(dimension_semantics=("arbitrary", "arbitrary"),
                                             vmem_limit_bytes=VMEM_LIMIT),
        name="prompt_ffn",
    )(x, lru, attn, g1, sh2, sc2, g2, *consts)


def _spre_kernel(x_ref, sh_ref, sc_ref, qca_ref, qsb_ref, kca_ref, ksb_ref, n1g_ref, win_ref, cbuf_ref, cw_ref,
                 cb_ref, wa_ref, ba_ref, wx_ref, bx_ref, lam_ref, h0_ref, qlg_ref, wuq_ref, msel_ref,
                 kvg_ref, wk_ref, gk_ref, log_ref,
                 lru_o, xlru_o, hnew_o, qabs_o, qr_o, kvn_o, kr_o, *, n_heads, first_pos):
    w = lam_ref.shape[-1]
    ql = qlg_ref.shape[-1]
    kl = kvg_ref.shape[-1]
    kconv = cw_ref.shape[0]
    ident = lambda v: v

    h = _rms(x_ref[...], n1g_ref[...]) * (1.0 + sc_ref[...]) + sh_ref[...]
    z = _dot(h.astype(bf16), win_ref[...])
    x_lru = z[:, 0:w]
    g_lru = z[:, w:2 * w]
    q_lat = z[:, 2 * w:2 * w + ql]
    kv_lat = z[:, 2 * w + ql:2 * w + ql + kl]
    kr_pre = z[:, 2 * w + ql + kl:2 * w + ql + kl + LANES]

    x_conv = cb_ref[...]
    for j in range(kconv - 1):
        x_conv = x_conv + cbuf_ref[j] * cw_ref[j:j + 1, :]
    x_conv = x_conv + x_lru * cw_ref[kconv - 1:kconv, :]
    xlru_o[...] = x_lru
    a, mult, ux = _lru_gates(x_conv, wa_ref, ba_ref[...], wx_ref, bx_ref[...], lam_ref[...])
    if first_pos:
        mult = jnp.ones_like(mult)
    hn = a * h0_ref[...] + mult * ux
    hnew_o[...] = hn
    lru_o[...] = _rms(hn * jax.nn.gelu(g_lru), log_ref[...]).astype(bf16)

    msel = msel_ref[...]
    qn = _rms(q_lat, qlg_ref[...]).astype(bf16)
    q = _dot(qn, wuq_ref[...])
    gk = gk_ref[...]
    for hd, o in enumerate(_q_heads(q, msel, qca_ref[...], qsb_ref[...], n_heads, ident)):
        wkh = wk_ref[:, hd * LANES:(hd + 1) * LANES]
        qabs_o[hd] = _dot_nt((o * gk).astype(bf16), wkh).astype(bf16)
        qr_o[hd] = jnp.concatenate([o[:, _R1_LO:_R1_LO + 16], o[:, _R2_LO:_R2_LO + 16]], axis=-1).astype(bf16)

    kvn_o[...] = _rms(kv_lat, kvg_ref[...])
    kr = _norm_rope(kr_pre, msel, kca_ref[...], ksb_ref[...], ident)
    kr_o[...] = jnp.concatenate([kr[:, _R1_LO:_R1_LO + 16], kr[:, _R2_LO:_R2_LO + 16]], axis=-1)


def _spre_call(x, sh1, sc1, rope_tabs, cbuf, h0, p, n_heads, first_pos):
    nb, d = x.shape
    w = p["lam"].shape[-1]
    kl = p["kvg"].shape[-1]
    out_shape = [jax.ShapeDtypeStruct((nb, w), bf16), jax.ShapeDtypeStruct((nb, w), f32),
                 jax.ShapeDtypeStruct((nb, w), f32), jax.ShapeDtypeStruct((n_heads, nb, kl), bf16),
                 jax.ShapeDtypeStruct((n_heads, nb, _ROPE), bf16), jax.ShapeDtypeStruct((nb, kl), f32),
                 jax.ShapeDtypeStruct((nb, _ROPE), f32)]
    return pl.pallas_call(
        functools.partial(_spre_kernel, n_heads=n_heads, first_pos=first_pos),
        out_shape=out_shape,
        compiler_params=pltpu.CompilerParams(vmem_limit_bytes=VMEM_LIMIT),
        name="sample_inproj",
    )(x, sh1, sc1, *rope_tabs, p["n1g"], p["win"], cbuf, p["cw"], p["cb"], p["wa"], p["ba"], p["wx"], p["bx"],
      p["lam"], h0, p["qlg"], p["wuq"], p["msel"], p["kvg"], p["wk"], p["gk"], p["log"])


def _sattn_kernel(pt_ref, lat_hbm, krt_hbm, wkt_ref, qabs_ref, qr_ref, latn_ref, krn_ref, o_ref,
                  lat_buf, kr_buf, lhs_s, latb_s, s_s, sems, *, layer, n_heads, ppsub):
    b = pl.program_id(0)
    nseq = pl.num_programs(0)
    _, npg, page, kl = lat_buf.shape
    nk = wkt_ref.shape[0]
    hp = qabs_ref.shape[0]
    n_past = npg * page
    tk = ppsub * page
    slot = lax.rem(b, 2)

    def page_copies(src_page, slot_, pg):
        return (pltpu.make_async_copy(lat_hbm.at[layer, src_page], lat_buf.at[slot_, pg], sems.at[0, slot_]),
                pltpu.make_async_copy(krt_hbm.at[layer, src_page], kr_buf.at[slot_, pg], sems.at[1, slot_]))

    def start_pages(seq, slot_, pg0, n):
        for i in range(n):
            for cp in page_copies(pt_ref[seq * npg + pg0 + i], slot_, pg0 + i):
                cp.start()

    @pl.when(b == 0)
    def _():
        def body(pg, c):
            start_pages(0, 0, pg, 1)
            return c
        lax.fori_loop(0, npg, body, 0)
        lhs_s[0:nk, :] = wkt_ref[...]

    lhs_s[nk:nk + hp, :] = qabs_ref[...]

    def wait_slot(slot_):
        pltpu.make_async_copy(lat_hbm.at[layer, pl.ds(0, npg)], lat_buf.at[slot_], sems.at[0, slot_]).wait()
        pltpu.make_async_copy(krt_hbm.at[layer, pl.ds(0, npg)], kr_buf.at[slot_], sems.at[1, slot_]).wait()

    wait_slot(slot)

    def scores(latb, krt):
        n = latb.shape[0]
        big = _dot_nt(lhs_s[...], latb)
        knt = big[0:nk]
        ssq = jnp.sum((knt * knt).reshape(n_heads, _NOPE, n), axis=1)
        rs = lax.rsqrt(ssq * (1.0 / _NOPE) + EPS)
        sr = _dot(qr_ref[...], krt)
        s8 = big[nk:nk + n_heads] * rs + sr[0:n_heads]
        return jnp.concatenate([s8, jnp.zeros((hp - n_heads, n), f32)], axis=0)

    def sub(j):
        p0 = pl.multiple_of(j * ppsub, ppsub)
        latb = lat_buf[slot, pl.ds(p0, ppsub)].reshape(tk, kl).astype(bf16)
        krt = jnp.concatenate([kr_buf[slot, p0 + i] for i in range(ppsub)], axis=-1).astype(bf16)
        return latb, scores(latb, krt)

    def softmax_update(carry, parts):
        m_run, l_run, acc = carry
        m_new = m_run
        for _, s in parts:
            m_new = jnp.maximum(m_new, jnp.max(s, axis=-1, keepdims=True))
        alpha = jnp.exp2(m_run - m_new)
        l_new = alpha * l_run
        acc = alpha * acc
        for latb, s in parts:
            pm = jnp.exp2(s - jnp.concatenate([m_new] * (s.shape[-1] // LANES), axis=-1))
            l_new = l_new + jnp.sum(pm, axis=-1, keepdims=True)
            acc = acc + _dot(pm.astype(bf16), latb)
        return m_new, l_new, acc

    nxt = jnp.minimum(b + 1, nseq - 1)
    nsub = npg // ppsub

    def stage(j, st):
        start_pages(nxt, 1 - slot, j * ppsub, ppsub)
        latb, s = sub(j)
        latb_s[st] = latb
        s_s[st] = s

    def stashed(st):
        return [(latb_s[st], s_s[st])]

    carry = (jnp.full((hp, LANES), -1e30, f32), jnp.zeros((hp, LANES), f32), jnp.zeros((hp, kl), f32))
    stage(0, 0)

    def pair(k, carry):
        stage(2 * k + 1, 1)
        carry = softmax_update(carry, stashed(0))
        stage(2 * k + 2, 0)
        return softmax_update(carry, stashed(1))

    carry = lax.fori_loop(0, (nsub - 2) // 2, pair, carry)
    stage(nsub - 1, 1)
    carry = softmax_update(carry, stashed(0))
    carry = softmax_update(carry, stashed(1))

    @pl.when(b == nseq - 1)
    def _():
        wait_slot(1 - slot)

    latn = jnp.broadcast_to(latn_ref[...], (page, kl)).astype(bf16)
    krn = jnp.broadcast_to(krn_ref[...], (krn_ref.shape[0], page)).astype(bf16)
    s_new = scores(latn, krn)
    s_new = jnp.where(lax.broadcasted_iota(jnp.int32, s_new.shape, 1) == 0, s_new, -1e30)
    _, l, acc = softmax_update(carry, [(latn, s_new)])
    o_ref[...] = acc / l


def _sattn_call(page_table, cache_lat, cache_krt, layer, wkt, qabs, qr, latn, krn, n_heads, ppsub):
    nb, npg = page_table.shape
    _, _, page, kl = cache_lat.shape
    rd = cache_krt.shape[2]
    hp = qabs.shape[1]
    nk = wkt.shape[0]
    assert npg % (2 * ppsub) == 0 and kl == LANES
    grid_spec = pltpu.PrefetchScalarGridSpec(
        num_scalar_prefetch=1,
        grid=(nb,),
        in_specs=[pl.BlockSpec(memory_space=pl.ANY),
                  pl.BlockSpec(memory_space=pl.ANY),
                  pl.BlockSpec((nk, kl), lambda bi, pt: (0, 0)),
                  pl.BlockSpec((None, hp, kl), lambda bi, pt: (bi, 0, 0)),
                  pl.BlockSpec((None, hp, rd), lambda bi, pt: (bi, 0, 0)),
                  pl.BlockSpec((None, 1, kl), lambda bi, pt: (bi, 0, 0)),
                  pl.BlockSpec((None, rd, 1), lambda bi, pt: (bi, 0, 0))],
        out_specs=pl.BlockSpec((None, hp, kl), lambda bi, pt: (bi, 0, 0)),
        scratch_shapes=[pltpu.VMEM((2, npg, page, kl), f32), pltpu.VMEM((2, npg, rd, page), f32),
                        pltpu.VMEM((nk + hp, kl), bf16), pltpu.VMEM((2, ppsub * page, kl), bf16),
                        pltpu.VMEM((2, hp, ppsub * page), f32), pltpu.SemaphoreType.DMA((2, 2))])
    return pl.pallas_call(
        functools.partial(_sattn_kernel, layer=layer, n_heads=n_heads, ppsub=ppsub),
        grid_spec=grid_spec,
        out_shape=jax.ShapeDtypeStruct((nb, hp, kl), f32),
        compiler_params=pltpu.CompilerParams(dimension_semantics=("arbitrary",), vmem_limit_bytes=VMEM_LIMIT),
        name="sample_attn",
    )(page_table.reshape(-1), cache_lat, cache_krt, wkt, qabs, qr, latn, krn)


def _spost_kernel(x_ref, lru_ref, olat_ref, g1_ref, sh2_ref, sc2_ref, g2_ref, mog_ref, n2g_ref, wv_ref, wo_ref,
                  wup_ref, fcw_ref, fcb_ref, wdn_ref, fbuf_ref, y_o, up_o, *, n_heads):
    nc = wdn_ref.shape[0]
    ck = wdn_ref.shape[1]
    kconv = fcw_ref.shape[1]
    heads = [_dot(olat_ref[hd].astype(bf16), wv_ref[:, hd * LANES:(hd + 1) * LANES]) for hd in range(n_heads)]
    attn = jnp.concatenate([heads[2 * pp] + heads[2 * pp + 1] for pp in range(n_heads // 2)], axis=-1)
    attn = _rms(attn, mog_ref[...]).astype(bf16)
    x1, h2 = _mix_and_norm2(x_ref[...], lru_ref[...], attn, wo_ref, g1_ref[...], n2g_ref[...], sc2_ref[...], sh2_ref[...])

    def conv(up, c):
        out = fcb_ref[c]
        cw = fcw_ref[c]
        for j in range(kconv - 1):
            out = out + fbuf_ref[j, :, c * ck:(c + 1) * ck] * cw[j:j + 1, :]
        up_o[:, c * ck:(c + 1) * ck] = up
        return out + up * cw[kconv - 1:kconv, :]

    acc = jnp.zeros(x1.shape, f32)
    for c in range(nc):
        val = conv(_dot(h2, wup_ref[c]), c)
        gt = conv(_dot(h2, wup_ref[nc + c]), nc + c)
        acc = acc + _dot((jax.nn.gelu(gt) * val).astype(bf16), wdn_ref[c])
    y_o[...] = x1 + g2_ref[...] * acc


def _spost_call(x, lru, olat, g1, sh2, sc2, g2, fbuf, p, n_heads):
    nb, d = x.shape
    nc, ck, _ = p["wdn"].shape
    return pl.pallas_call(
        functools.partial(_spost_kernel, n_heads=n_heads),
        out_shape=[jax.ShapeDtypeStruct((nb, d), f32), jax.ShapeDtypeStruct((nb, 2 * nc * ck), f32)],
        compiler_params=pltpu.CompilerParams(vmem_limit_bytes=VMEM_LIMIT),
        name="sample_ffn",
    )(x, lru, olat, g1, sh2, sc2, g2, p["mog"], p["n2g"], p["wv"], p["wo"], p["wup"], p["fcw"], p["fcb"], p["wdn"], fbuf)


def _take_cols(wmat, idx):
    padded = jnp.concatenate([wmat, jnp.zeros(wmat.shape[:-1] + (1,), wmat.dtype)], axis=-1)
    return jnp.take(padded, jnp.asarray(np.where(idx < 0, wmat.shape[-1], idx)), axis=-1)


def _block_diag_groups(wh):
    nh, hd, _ = wh.shape
    per = MXU_DIM // hd
    groups = []
    for g in range(nh // per):
        blk = jnp.zeros((MXU_DIM, MXU_DIM), wh.dtype)
        for j in range(per):
            blk = lax.dynamic_update_slice(blk, wh[g * per + j], (j * hd, j * hd))
        groups.append(blk)
    return jnp.stack(groups).astype(bf16)


def _prep_layer(l, n_heads, scale, w_in, lru_conv_w, lru_conv_b, lru_w_a, lru_b_a, lru_w_x, lru_b_x, lru_lambda,
                norm1_g, q_lora_norm_g, w_uq, q_nope_norm_g, q_rope_norm_g, kv_lora_norm_g, k_rope_norm_g, w_ukv,
                k_nope_norm_g, lru_out_norm_g, mla_out_norm_g, w_o, norm2_g, w_up, ffn_conv_w, ffn_conv_b, w_down):
    src = _h128_src()
    w = lru_lambda.shape[-1]
    ql = q_lora_norm_g.shape[-1]
    kl = kv_lora_norm_g.shape[-1]
    qk = _NOPE + _ROPE
    vd = w_ukv.shape[-1] // n_heads - _NOPE
    row = lambda v: v.reshape(1, -1).astype(f32)

    kr_src = np.where(src >= _NOPE, src - _NOPE, -1)
    win = w_in[l]
    base = 2 * w + ql + kl
    win_ext = jnp.concatenate([win[:, :base], _take_cols(win[:, base:], kr_src)], axis=-1).astype(bf16)

    q_idx = np.concatenate([np.where(src >= 0, src + h * qk, -1) for h in range(n_heads)])
    wuq = _take_cols(w_uq[l], q_idx).astype(bf16)
    k_src = np.where((src >= 0) & (src < _NOPE), src, -1)
    k_idx = np.concatenate([np.where(k_src >= 0, k_src + h * (_NOPE + vd), -1) for h in range(n_heads)])
    wk = _take_cols(w_ukv[l], k_idx).astype(bf16)
    v_idx = []
    vone = np.zeros((1, n_heads * LANES), np.float32)
    for h in range(n_heads):
        slab = np.full((LANES,), -1, np.int64)
        off = (h % 2) * vd
        slab[off:off + vd] = h * (_NOPE + vd) + _NOPE + np.arange(vd)
        v_idx.append(slab)
        vone[0, h * LANES + (vd - off)] = 1.0
    wv = _take_cols(w_ukv[l], np.concatenate(v_idx)).astype(bf16)
    wkt_idx = np.concatenate([h * (_NOPE + vd) + np.arange(_NOPE) for h in range(n_heads)])
    wkt = jnp.take(w_ukv[l], jnp.asarray(wkt_idx), axis=-1).T.astype(bf16)

    nope_tab = lambda g: _take_cols(g.reshape(1, -1), k_src)
    rope_tab = lambda g: _take_cols(g.reshape(1, -1), kr_src)
    is_n = (k_src >= 0).astype(np.float32)
    is_r = (kr_src >= 0).astype(np.float32)
    msel = jnp.asarray(np.outer(is_n, is_n) / _NOPE + np.outer(is_r, is_r) / _ROPE).astype(bf16)
    gq = (nope_tab(q_nope_norm_g[l]) + rope_tab(q_rope_norm_g[l])) * scale

    dff = w_down.shape[1]
    ck = MXU_DIM
    nc = dff // ck
    wup = w_up[l].reshape(w_up.shape[1], 2 * nc, ck).transpose(1, 0, 2).astype(bf16)
    fcw = ffn_conv_w[l].reshape(-1, 2 * nc, ck).transpose(1, 0, 2).astype(f32)
    fcb = ffn_conv_b[l].reshape(2 * nc, 1, ck).astype(f32)
    wdn = w_down[l].reshape(nc, ck, -1).astype(bf16)
    return dict(
        n1g=row(norm1_g[l]), win=win_ext, cw=lru_conv_w[l].astype(f32), cb=row(lru_conv_b[l]),
        wa=_block_diag_groups(lru_w_a[l]), ba=row(lru_b_a[l]), wx=_block_diag_groups(lru_w_x[l]), bx=row(lru_b_x[l]),
        lam=row(lru_lambda[l]), qlg=row(q_lora_norm_g[l]), wuq=wuq, gq=gq.astype(f32), msel=msel,
        kvg=row(kv_lora_norm_g[l]), gkr=rope_tab(k_rope_norm_g[l]).astype(f32), wk=wk,
        gk=nope_tab(k_nope_norm_g[l]).astype(f32), wv=wv, vone=jnp.asarray(vone), wkt=wkt, log=row(lru_out_norm_g[l]),
        mog=row(mla_out_norm_g[l]), wo=w_o[l].astype(bf16), n2g=row(norm2_g[l]), wup=wup, fcw=fcw, fcb=fcb, wdn=wdn)


def _rope_tables(pos):
    half = _ROPE // 2
    inv = ROPE_THETA ** (-np.arange(0, _ROPE, 2, dtype=np.float64) / _ROPE)
    ang = np.asarray(pos, np.float64)[:, None] * inv[None, :]
    cos, sin = np.cos(ang), np.sin(ang)
    n = ang.shape[0]
    cc = np.zeros((n, LANES), np.float32)
    cc[:, 0:_R1_LO] = 1.0
    cc[:, _R1_LO + half:_R1_LO + half + 16] = 1.0
    cc[:, _R1_LO:_R1_LO + half] = cos
    cc[:, _R2_LO:_R2_LO + half] = cos
    ss = np.zeros((n, LANES), np.float32)
    ss[:, _R1_LO:_R1_LO + half] = -sin
    ss[:, _R2_LO:_R2_LO + half] = sin
    return jnp.asarray(cc), jnp.asarray(ss)


def _gain_rope_tables(cc, ss, gq, gkr):
    swap = lambda g: jnp.roll(g, LANES // 2, axis=-1)
    return cc * gq, ss * swap(gq), cc * gkr, ss * swap(gkr)


def kernel(x_prompt, x_sample, cache_kv_latent, cache_k_rope, state_lru_h, state_lru_conv, state_ffn_conv,
           page_table, c_prompt, c_sample, w_ada, b_ada, norm1_g, w_in, lru_conv_w, lru_conv_b, lru_w_a, lru_b_a,
           lru_w_x, lru_b_x, lru_lambda, q_lora_norm_g, w_uq, q_nope_norm_g, q_rope_norm_g, kv_lora_norm_g,
           k_rope_norm_g, w_ukv, k_nope_norm_g, lru_out_norm_g, mla_out_norm_g, w_o, norm2_g, w_up, ffn_conv_w,
           ffn_conv_b, w_down):
    b, s, d = x_prompt.shape
    nb, ds, _ = x_sample.shape
    depth = w_in.shape[0]
    assert ds == 1 and q_nope_norm_g.shape[-1] == _NOPE and q_rope_norm_g.shape[-1] == _ROPE
    n_heads = w_uq.shape[-1] // (_NOPE + _ROPE)
    scale = float(_NOPE + _ROPE) ** -0.5 * float(np.log2(np.e))
    npg = page_table.shape[1]
    n_past = npg * cache_kv_latent.shape[2]
    ts = 64 if s % 64 == 0 else 32
    tq = min(512, s)
    tk = min(256, s)
    tm = min(512, s)
    ppsub = 16 if npg % 32 == 0 else 1
    hp = 16

    cache_krt = jnp.swapaxes(cache_k_rope, 2, 3)
    cc_p, ss_p = _rope_tables(np.arange(s))
    cc_s, ss_s = _rope_tables(n_past + np.arange(1))

    y_p = x_prompt
    y_s = x_sample.reshape(nb, d)
    c_all = jnp.concatenate([c_prompt, c_sample], axis=0)
    outs_p = [[] for _ in range(5)]
    outs_s = [[] for _ in range(5)]
    for l in range(depth):
        p = _prep_layer(l, n_heads, scale, w_in, lru_conv_w, lru_conv_b, lru_w_a, lru_b_a, lru_w_x, lru_b_x,
                        lru_lambda, norm1_g, q_lora_norm_g, w_uq, q_nope_norm_g, q_rope_norm_g, kv_lora_norm_g,
                        k_rope_norm_g, w_ukv, k_nope_norm_g, lru_out_norm_g, mla_out_norm_g, w_o, norm2_g, w_up,
                        ffn_conv_w, ffn_conv_b, w_down)
        mod = _mod_call(c_all, w_ada[l], b_ada[l])
        mp = [m_[:, None, :] for m_ in jnp.split(mod[:b], 6, axis=-1)]
        ms = jnp.split(mod[b:], 6, axis=-1)

        lru_p, q_p, k_p, v_p, kvlat_p, krope_p, hlast_p, xtail_p = _inproj_call(
            y_p, mp[0], mp[1], _gain_rope_tables(cc_p, ss_p, p["gq"], p["gkr"]), p, n_heads, ts)
        attn_p = _attn_call(q_p, k_p, v_p, p["mog"], n_heads, tq, tk)
        y_p, ftail_p = _ffn_call(y_p, lru_p, attn_p, mp[2], mp[3], mp[4], mp[5], p, tm)
        kc = lru_conv_w.shape[1]
        fk = ffn_conv_w.shape[1]
        for j, o in enumerate((kvlat_p, krope_p, hlast_p, xtail_p[:, 8 - (kc - 1):], ftail_p[:, 8 - (fk - 1):])):
            outs_p[j].append(o)

        cbuf = jnp.swapaxes(state_lru_conv[l], 0, 1)
        fbuf = jnp.swapaxes(state_ffn_conv[l], 0, 1)
        lru_s, xlru_s, hnew_s, qabs, qr, kvn_s, kr_s = _spre_call(
            y_s, ms[0], ms[1], _gain_rope_tables(cc_s, ss_s, p["gq"], p["gkr"]), cbuf, state_lru_h[l], p, n_heads,
            n_past == 0)
        pad_heads = lambda t: jnp.pad(jnp.swapaxes(t, 0, 1), ((0, 0), (0, hp - n_heads), (0, 0)))
        olat = _sattn_call(page_table, cache_kv_latent, cache_krt, l, p["wkt"], pad_heads(qabs), pad_heads(qr),
                           kvn_s[:, None, :], kr_s[:, :, None], n_heads, ppsub)
        olat = jnp.swapaxes(olat[:, :n_heads], 0, 1)
        y_s, up_s = _spost_call(y_s, lru_s, olat, ms[2], ms[3], ms[4], ms[5], fbuf, p, n_heads)
        lru_conv_new = jnp.concatenate([state_lru_conv[l][:, 1:], xlru_s[:, None, :]], axis=1)
        ffn_conv_new = jnp.concatenate([state_ffn_conv[l][:, 1:], up_s[:, None, :]], axis=1)
        for j, o in enumerate((kvn_s[:, None, :], kr_s[:, None, :], hnew_s, lru_conv_new, ffn_conv_new)):
            outs_s[j].append(o)

    return (y_p, y_s.reshape(nb, 1, d), *[jnp.stack(o) for o in outs_p], *[jnp.stack(o) for o in outs_s])
```

```python
import functools

import numpy as np
import jax
import jax.numpy as jnp
from jax import lax
from jax.experimental import pallas as pl
from jax.experimental.pallas import tpu as pltpu

f32 = jnp.float32
bf16 = jnp.bfloat16

EPS = 1e-6
LRU_C = 8.0
ROPE_THETA = 10000.0
LANES = 128
MXU_DIM = 256
VMEM_LIMIT = 56 * 1024 * 1024

_NOPE, _ROPE = 64, 32
_R1_LO, _R2_LO = 48, 112


def _h128_src():
    src = np.full((LANES,), -1, np.int32)
    src[0:48] = np.arange(0, 48)
    src[48:64] = _NOPE + np.arange(0, 16)
    src[64:80] = np.arange(48, 64)
    src[112:128] = _NOPE + 16 + np.arange(0, 16)
    return src


def _dot(a, b):
    return jnp.dot(a, b, preferred_element_type=f32)


def _dot_nt(a, b):
    return lax.dot_general(a, b, (((1,), (1,)), ((), ())), preferred_element_type=f32)


def _rms(x, g):
    ms = jnp.mean(x * x, axis=-1, keepdims=True)
    return x * lax.rsqrt(ms + EPS) * g


def _neg_expm1_2x(y):
    t = jnp.tanh(y)
    return -2.0 * t / (1.0 - t)


def _lru_gates(x_conv, wa_ref, ba, wx_ref, bx, lam):
    xb = x_conv.astype(bf16)
    ng = wa_ref.shape[0]
    ra = jnp.concatenate([_dot(xb[:, g * MXU_DIM:(g + 1) * MXU_DIM], wa_ref[g]) for g in range(ng)], axis=-1) + ba
    ia = jnp.concatenate([_dot(xb[:, g * MXU_DIM:(g + 1) * MXU_DIM], wx_ref[g]) for g in range(ng)], axis=-1) + bx
    r = jax.nn.sigmoid(ra)
    ig = jax.nn.sigmoid(ia)
    log_a = (-LRU_C) * r * jax.nn.softplus(-lam)
    a = jnp.exp(log_a)
    mult = jnp.sqrt(_neg_expm1_2x(log_a))
    return a, mult, ig * x_conv


def _group_ms(x, msel):
    return _dot((x * x).astype(bf16), msel)


def _norm_rope(x, msel, ca, sb, to3d):
    inv = lax.rsqrt(_group_ms(x, msel) + EPS)
    return to3d(inv) * (to3d(x) * ca + to3d(pltpu.roll(x, LANES // 2, 1)) * sb)


def _q_heads(q, msel, ca, sb, n_heads, to3d):
    return [_norm_rope(q[:, h * LANES:(h + 1) * LANES], msel, ca, sb, to3d) for h in range(n_heads)]


def _mod_kernel(c_ref, w_ref, b_ref, o_ref):
    c = c_ref[...]
    sc = (c * jax.nn.sigmoid(c)).astype(bf16)
    o_ref[...] = _dot(sc, w_ref[...].astype(bf16)) + b_ref[...]


def _mod_call(c_all, w_ada, b_ada):
    m, d = c_all.shape
    n = w_ada.shape[1]
    tn = 1536 if n % 1536 == 0 else n
    return pl.pallas_call(
        _mod_kernel,
        grid=(n // tn,),
        in_specs=[pl.BlockSpec((m, d), lambda j: (0, 0)),
                  pl.BlockSpec((d, tn), lambda j: (0, j)),
                  pl.BlockSpec((1, tn), lambda j: (0, j))],
        out_specs=pl.BlockSpec((m, tn), lambda j: (0, j)),
        out_shape=jax.ShapeDtypeStruct((m, n), f32),
        compiler_params=pltpu.CompilerParams(dimension_semantics=("arbitrary",), vmem_limit_bytes=VMEM_LIMIT),
        name="adaln_mod",
    )(c_all, w_ada, b_ada.reshape(1, n))


def _inproj_kernel(x_ref, sh_ref, sc_ref, qca_ref, qsb_ref, kca_ref, ksb_ref, n1g_ref, win_ref, cw_ref, cb_ref,
                   wa_ref, ba_ref, wx_ref, bx_ref, lam_ref, qlg_ref, wuq_ref, msel_ref,
                   kvg_ref, wk_ref, gk_ref, wv_ref, vone_ref, log_ref,
                   lru_o, q_o, k_o, v_o, kvlat_o, krope_o, hlast_o, xtail_o,
                   xtail_s, a_s, u_s, hs_s, h_s, *, n_heads):
    i = pl.program_id(0)
    nb, ts, d = x_ref.shape
    m = nb * ts
    w = lam_ref.shape[-1]
    ql = qlg_ref.shape[-1]
    kl = kvg_ref.shape[-1]
    kconv = cw_ref.shape[0]
    nlc = w // LANES

    @pl.when(i == 0)
    def _():
        xtail_s[...] = jnp.zeros_like(xtail_s)
        h_s[...] = jnp.zeros_like(h_s)

    def to3d(v):
        return v.reshape(nb, ts, v.shape[-1])

    x = x_ref[...]
    h = _rms(x, n1g_ref[...]) * (1.0 + sc_ref[...]) + sh_ref[...]
    z = _dot(h.reshape(m, d).astype(bf16), win_ref[...])
    x_lru = z[:, 0:w]
    g_lru = z[:, w:2 * w]
    q_lat = z[:, 2 * w:2 * w + ql]
    kv_lat = z[:, 2 * w + ql:2 * w + ql + kl]
    kr_pre = z[:, 2 * w + ql + kl:2 * w + ql + kl + LANES]

    t_idx = lax.broadcasted_iota(jnp.int32, (m, 1), 0) & (ts - 1)
    r8 = lax.broadcasted_iota(jnp.int32, (1, 8, 1), 1)
    tail2d = xtail_s[...].reshape(nb * 8, w)
    x_conv = to3d(jnp.broadcast_to(cb_ref[...], (m, w)))
    for j in range(kconv - 1):
        k = kconv - 1 - j
        rolled = to3d(pltpu.roll(x_lru, k, 0))
        prev = pltpu.roll(tail2d, nb * 8 + k - 8, 0).reshape(nb, 8, w)
        sh = jnp.concatenate([jnp.where(r8 < k, prev, rolled[:, 0:8]), rolled[:, 8:]], axis=1)
        x_conv = x_conv + sh * cw_ref[j:j + 1, :]
    x_conv = (x_conv + to3d(x_lru) * cw_ref[kconv - 1:kconv, :]).reshape(m, w)
    xtail_s[...] = to3d(x_lru)[:, ts - 8:, :]
    xtail_o[...] = to3d(x_lru)[:, ts - 8:, :]

    a, mult, ux = _lru_gates(x_conv, wa_ref, ba_ref[...], wx_ref, bx_ref[...], lam_ref[...])
    mult = jnp.where(jnp.logical_and(t_idx == 0, i == 0), 1.0, mult)
    u = mult * ux
    pitch = a_s.shape[1] // nb
    for j in range(nlc):
        for bi in range(nb):
            a_s[j, bi * pitch:bi * pitch + ts, :] = a[bi * ts:(bi + 1) * ts, j * LANES:(j + 1) * LANES]
            u_s[j, bi * pitch:bi * pitch + ts, :] = u[bi * ts:(bi + 1) * ts, j * LANES:(j + 1) * LANES]

    def scan_step(t, hc):
        out = []
        for j in range(nlc):
            hj = a_s[j, pl.ds(t, nb, stride=pitch), :] * hc[j] + u_s[j, pl.ds(t, nb, stride=pitch), :]
            hs_s[j, pl.ds(t, nb, stride=pitch), :] = hj
            out.append(hj)
        return tuple(out)

    hc = lax.fori_loop(0, ts, scan_step, tuple(h_s[j] for j in range(nlc)), unroll=8)
    for j in range(nlc):
        h_s[j] = hc[j]
    hlast_o[...] = jnp.concatenate(list(hc), axis=-1)
    hs = jnp.concatenate(
        [jnp.concatenate([hs_s[j, bi * pitch:bi * pitch + ts, :] for bi in range(nb)], axis=0) for j in range(nlc)],
        axis=-1)
    lru_out = hs * jax.nn.gelu(g_lru)
    lru_o[...] = to3d(_rms(lru_out, log_ref[...]).astype(bf16))

    msel = msel_ref[...]
    qn = _rms(q_lat, qlg_ref[...]).astype(bf16)
    q = _dot(qn, wuq_ref[...])
    for hd, o in enumerate(_q_heads(q, msel, qca_ref[...][None], qsb_ref[...][None], n_heads, to3d)):
        q_o[:, :, hd * LANES:(hd + 1) * LANES] = o.astype(bf16)

    kvn = _rms(kv_lat, kvg_ref[...])
    kvlat_o[...] = to3d(kvn)
    kr = _norm_rope(kr_pre, msel, kca_ref[...][None], ksb_ref[...][None], to3d)
    krope_o[...] = jnp.concatenate([kr[:, :, _R1_LO:_R1_LO + 16], kr[:, :, _R2_LO:_R2_LO + 16]], axis=-1)
    kvb = kvn.astype(bf16)
    kk = _dot(kvb, wk_ref[...])
    v_o[...] = to3d((_dot(kvb, wv_ref[...]) + vone_ref[...]).astype(bf16))
    gk = gk_ref[...]
    for hd in range(n_heads):
        kh = kk[:, hd * LANES:(hd + 1) * LANES]
        khn = kh * lax.rsqrt(_group_ms(kh, msel) + EPS) * gk
        k_o[:, :, hd * LANES:(hd + 1) * LANES] = (to3d(khn) + kr).astype(bf16)


def _const_spec(shape):
    nd = len(shape)
    return pl.BlockSpec(shape, lambda *_: (0,) * nd, pipeline_mode=pl.Buffered(1))


def _inproj_call(x, sh1, sc1, rope_tabs, p, n_heads, ts):
    b, s, d = x.shape
    w = p["lam"].shape[-1]
    kl = p["kvg"].shape[-1]
    hw = n_heads * LANES
    m = b * ts
    consts = [p["n1g"], p["win"], p["cw"], p["cb"], p["wa"], p["ba"], p["wx"], p["bx"], p["lam"], p["qlg"],
              p["wuq"], p["msel"], p["kvg"], p["wk"], p["gk"], p["wv"], p["vone"], p["log"]]
    in_specs = ([pl.BlockSpec((b, ts, d), lambda i: (0, i, 0)), _const_spec(sh1.shape), _const_spec(sc1.shape)]
                + [pl.BlockSpec((ts, LANES), lambda i: (i, 0))] * len(rope_tabs)
                + [_const_spec(c.shape) for c in consts])

    def tile(n, dt):
        return pl.BlockSpec((b, ts, n), lambda i: (0, i, 0)), jax.ShapeDtypeStruct((b, s, n), dt)

    outs = [tile(w, bf16), tile(hw, bf16), tile(hw, bf16), tile(hw, bf16), tile(kl, f32), tile(_ROPE, f32),
            (pl.BlockSpec((b, w), lambda i: (0, 0)), jax.ShapeDtypeStruct((b, w), f32)),
            (pl.BlockSpec((b, 8, w), lambda i: (0, 0, 0)), jax.ShapeDtypeStruct((b, 8, w), f32))]
    nlc = w // LANES
    return pl.pallas_call(
        functools.partial(_inproj_kernel, n_heads=n_heads),
        grid=(s // ts,),
        in_specs=in_specs,
        out_specs=[o[0] for o in outs],
        out_shape=[o[1] for o in outs],
        scratch_shapes=[pltpu.VMEM((b, 8, w), f32)] + [pltpu.VMEM((nlc, b * (ts + 8), LANES), f32)] * 3
        + [pltpu.VMEM((nlc, b, LANES), f32)],
        compiler_params=pltpu.CompilerParams(dimension_semantics=("arbitrary",), vmem_limit_bytes=VMEM_LIMIT),
        name="prompt_inproj",
    )(x, sh1, sc1, *rope_tabs, *consts)


def _attn_kernel(q_ref, k_ref, v_ref, g_ref, o_ref, m_s, acc_s, *, n_heads, tk):
    qi = pl.program_id(1)
    tq = q_ref.shape[1]
    nmask = tq // tk
    n_full = qi * nmask
    m_s[...] = jnp.full_like(m_s, -1e30)
    acc_s[...] = jnp.zeros_like(acc_s)

    def block(j, r0, masked):
        start = pl.multiple_of(j * tk, tk)
        if masked:
            vis = (lax.broadcasted_iota(jnp.int32, (tq - r0, tk), 1)
                   <= lax.broadcasted_iota(jnp.int32, (tq - r0, tk), 0))
        for hd in range(n_heads):
            hs = slice(hd * LANES, (hd + 1) * LANES)
            s = _dot_nt(q_ref[0, r0:, hs], k_ref[0, pl.ds(start, tk), hs])
            if masked:
                s = jnp.where(vis, s, -1e30)
            mx = m_s[hd, r0:, :]
            m_new = jnp.maximum(mx, jnp.max(s, axis=-1, keepdims=True))
            alpha = jnp.exp2(mx - m_new)
            pm = jnp.exp2(s - jnp.concatenate([m_new] * (tk // LANES), axis=-1))
            acc_s[hd, r0:, :] = alpha * acc_s[hd, r0:, :] + _dot(pm.astype(bf16), v_ref[0, pl.ds(start, tk), hs])
            m_s[hd, r0:, :] = m_new

    def full_block(j, c):
        block(j, 0, False)
        return c

    lax.fori_loop(0, n_full, full_block, 0)
    for jm in range(nmask):
        block(n_full + jm, jm * tk, True)
    lane = lax.broadcasted_iota(jnp.int32, (1, LANES), 1)
    half = LANES // 2
    heads = []
    for hd in range(n_heads):
        acc = acc_s[hd]
        lo = (hd % 2) * half
        one_lane = (half - lo)
        l = jnp.sum(jnp.where(lane == one_lane, acc, 0.0), axis=-1, keepdims=True)
        heads.append(jnp.where((lane >= lo) & (lane < lo + half), acc, 0.0) / l)
    o = jnp.concatenate([heads[2 * pp] + heads[2 * pp + 1] for pp in range(n_heads // 2)], axis=-1)
    o_ref[0] = _rms(o, g_ref[...]).astype(bf16)


def _attn_call(q, k, v, g, n_heads, tq, tk):
    b, s, hw = q.shape
    wout = g.shape[-1]
    return pl.pallas_call(
        functools.partial(_attn_kernel, n_heads=n_heads, tk=tk),
        grid=(b, s // tq),
        in_specs=[pl.BlockSpec((1, tq, hw), lambda bi, qi: (bi, qi, 0)),
                  pl.BlockSpec((1, s, hw), lambda bi, qi: (bi, 0, 0)),
                  pl.BlockSpec((1, s, hw), lambda bi, qi: (bi, 0, 0)),
                  pl.BlockSpec((1, wout), lambda bi, qi: (0, 0))],
        out_specs=pl.BlockSpec((1, tq, wout), lambda bi, qi: (bi, qi, 0)),
        out_shape=jax.ShapeDtypeStruct((b, s, wout), bf16),
        scratch_shapes=[pltpu.VMEM((n_heads, tq, LANES), f32), pltpu.VMEM((n_heads, tq, LANES), f32)],
        compiler_params=pltpu.CompilerParams(dimension_semantics=("arbitrary", "arbitrary"),
                                             vmem_limit_bytes=VMEM_LIMIT),
        name="prompt_attn",
    )(q, k, v, g)


def _mix_and_norm2(x, lru, attn, wo_ref, g1, n2g, sc2, sh2):
    w = lru.shape[-1]
    mixed = _dot(lru, wo_ref[pl.ds(0, w), :]) + _dot(attn, wo_ref[pl.ds(w, attn.shape[-1]), :])
    x1 = x + g1 * mixed
    h2 = (_rms(x1, n2g) * (1.0 + sc2) + sh2).astype(bf16)
    return x1, h2


def _ffn_kernel(x_ref, lru_ref, attn_ref, g1_ref, sh2_ref, sc2_ref, g2_ref, n2g_ref, wo_ref, wup_ref,
                fcw_ref, fcb_ref, wdn_ref, y_o, tail_o, tail_s, act_s):
    si = pl.program_id(1)
    tm = x_ref.shape[1]
    nc = wdn_ref.shape[0]
    ck = wdn_ref.shape[1]
    kconv = fcw_ref.shape[1]

    @pl.when(si == 0)
    def _():
        tail_s[...] = jnp.zeros_like(tail_s)

    x1, h2 = _mix_and_norm2(x_ref[0], lru_ref[0], attn_ref[0], wo_ref, g1_ref[0], n2g_ref[...], sc2_ref[0], sh2_ref[0])
    r8 = lax.broadcasted_iota(jnp.int32, (8, 1), 0)

    def conv(up, c):
        tail = tail_s[c]
        first = up[0:8]
        out = fcb_ref[c]
        cw = fcw_ref[c]
        for j in range(kconv - 1):
            k = kconv - 1 - j
            head = jnp.where(r8 < k, pltpu.roll(tail, k, 0), pltpu.roll(first, k, 0))
            sh = jnp.concatenate([head, pltpu.roll(up, k, 0)[8:]], axis=0)
            out = out + sh * cw[j:j + 1, :]
        out = out + up * cw[kconv - 1:kconv, :]
        tail_s[c] = up[tm - 8:]
        tail_o[0, :, c * ck:(c + 1) * ck] = up[tm - 8:]
        return out

    for c in range(nc):
        val = conv(_dot(h2, wup_ref[c]), c)
        gt = conv(_dot(h2, wup_ref[nc + c]), nc + c)
        act_s[:, c * ck:(c + 1) * ck] = (jax.nn.gelu(gt) * val).astype(bf16)
    acc = _dot(act_s[...], wdn_ref[...].reshape(nc * ck, x1.shape[-1]))
    y_o[0] = x1 + g2_ref[0] * acc


def _ffn_call(x, lru, attn, g1, sh2, sc2, g2, p, tm):
    b, s, d = x.shape
    w = lru.shape[-1]
    nc, ck, _ = p["wdn"].shape
    row = lambda n: pl.BlockSpec((1, tm, n), lambda bi, si: (bi, si, 0))
    modspec = pl.BlockSpec((1, 1, d), lambda bi, si: (bi, 0, 0))
    consts = [p["n2g"], p["wo"], p["wup"], p["fcw"], p["fcb"], p["wdn"]]
    return pl.pallas_call(
        _ffn_kernel,
        grid=(b, s // tm),
        in_specs=[row(d), row(w), row(attn.shape[-1]), modspec, modspec, modspec, modspec]
        + [_const_spec(c.shape) for c in consts],
        out_specs=[row(d), pl.BlockSpec((1, 8, 2 * nc * ck), lambda bi, si: (bi, 0, 0))],
        out_shape=[jax.ShapeDtypeStruct((b, s, d), f32), jax.ShapeDtypeStruct((b, 8, 2 * nc * ck), f32)],
        scratch_shapes=[pltpu.VMEM((2 * nc, 8, ck), f32), pltpu.VMEM((tm, nc * ck), bf16)],
        compiler_params=pltpu.CompilerParams(dimension_semantics=("arbitrary", "arbitrary"),
                                             vmem_limit_bytes=VMEM_LIMIT),
        name="prompt_ffn",
    )(x, lru, attn, g1, sh2, sc2, g2, *consts)


def _spre_kernel(x_ref, sh_ref, sc_ref, qca_ref, qsb_ref, kca_ref, ksb_ref, n1g_ref, win_ref, cbuf_ref, cw_ref,
                 cb_ref, wa_ref, ba_ref, wx_ref, bx_ref, lam_ref, h0_ref, qlg_ref, wuq_ref, msel_ref,
                 kvg_ref, wk_ref, gk_ref, log_ref,
                 lru_o, xlru_o, hnew_o, qabs_o, qr_o, kvn_o, kr_o, *, n_heads, first_pos):
    w = lam_ref.shape[-1]
    ql = qlg_ref.shape[-1]
    kl = kvg_ref.shape[-1]
    kconv = cw_ref.shape[0]
    ident = lambda v: v

    h = _rms(x_ref[...], n1g_ref[...]) * (1.0 + sc_ref[...]) + sh_ref[...]
    z = _dot(h.astype(bf16), win_ref[...])
    x_lru = z[:, 0:w]
    g_lru = z[:, w:2 * w]
    q_lat = z[:, 2 * w:2 * w + ql]
    kv_lat = z[:, 2 * w + ql:2 * w + ql + kl]
    kr_pre = z[:, 2 * w + ql + kl:2 * w + ql + kl + LANES]

    x_conv = cb_ref[...]
    for j in range(kconv - 1):
        x_conv = x_conv + cbuf_ref[j] * cw_ref[j:j + 1, :]
    x_conv = x_conv + x_lru * cw_ref[kconv - 1:kconv, :]
    xlru_o[...] = x_lru
    a, mult, ux = _lru_gates(x_conv, wa_ref, ba_ref[...], wx_ref, bx_ref[...], lam_ref[...])
    if first_pos:
        mult = jnp.ones_like(mult)
    hn = a * h0_ref[...] + mult * ux
    hnew_o[...] = hn
    lru_o[...] = _rms(hn * jax.nn.gelu(g_lru), log_ref[...]).astype(bf16)

    msel = msel_ref[...]
    qn = _rms(q_lat, qlg_ref[...]).astype(bf16)
    q = _dot(qn, wuq_ref[...])
    gk = gk_ref[...]
    for hd, o in enumerate(_q_heads(q, msel, qca_ref[...], qsb_ref[...], n_heads, ident)):
        wkh = wk_ref[:, hd * LANES:(hd + 1) * LANES]
        qabs_o[hd] = _dot_nt((o * gk).astype(bf16), wkh).astype(bf16)
        qr_o[hd] = jnp.concatenate([o[:, _R1_LO:_R1_LO + 16], o[:, _R2_LO:_R2_LO + 16]], axis=-1).astype(bf16)

    kvn_o[...] = _rms(kv_lat, kvg_ref[...])
    kr = _norm_rope(kr_pre, msel, kca_ref[...], ksb_ref[...], ident)
    kr_o[...] = jnp.concatenate([kr[:, _R1_LO:_R1_LO + 16], kr[:, _R2_LO:_R2_LO + 16]], axis=-1)


def _spre_call(x, sh1, sc1, rope_tabs, cbuf, h0, p, n_heads, first_pos):
    nb, d = x.shape
    w = p["lam"].shape[-1]
    kl = p["kvg"].shape[-1]
    out_shape = [jax.ShapeDtypeStruct((nb, w), bf16), jax.ShapeDtypeStruct((nb, w), f32),
                 jax.ShapeDtypeStruct((nb, w), f32), jax.ShapeDtypeStruct((n_heads, nb, kl), bf16),
                 jax.ShapeDtypeStruct((n_heads, nb, _ROPE), bf16), jax.ShapeDtypeStruct((nb, kl), f32),
                 jax.ShapeDtypeStruct((nb, _ROPE), f32)]
    return pl.pallas_call(
        functools.partial(_spre_kernel, n_heads=n_heads, first_pos=first_pos),
        out_shape=out_shape,
        compiler_params=pltpu.CompilerParams(vmem_limit_bytes=VMEM_LIMIT),
        name="sample_inproj",
    )(x, sh1, sc1, *rope_tabs, p["n1g"], p["win"], cbuf, p["cw"], p["cb"], p["wa"], p["ba"], p["wx"], p["bx"],
      p["lam"], h0, p["qlg"], p["wuq"], p["msel"], p["kvg"], p["wk"], p["gk"], p["log"])


def _sattn_kernel(pt_ref, lat_hbm, krt_hbm, wkt_ref, qabs_ref, qr_ref, latn_ref, krn_ref, o_ref,
                  lat_buf, kr_buf, lhs_s, latb_s, s_s, car_s, sems, *, layer, n_heads, ppsub):
    b = pl.program_id(0)
    nseq = pl.num_programs(0) - 1
    _, npg, page, kl = lat_buf.shape
    nk = wkt_ref.shape[0]
    hp = qabs_ref.shape[0]
    tk = ppsub * page
    slot = lax.rem(b, 2)

    def page_copies(src_page, slot_, pg):
        return (pltpu.make_async_copy(lat_hbm.at[layer, src_page], lat_buf.at[slot_, pg], sems.at[0, slot_]),
                pltpu.make_async_copy(krt_hbm.at[layer, src_page], kr_buf.at[slot_, pg], sems.at[1, slot_]))

    def start_pages(seq, slot_, pg0, n):
        for i in range(n):
            for cp in page_copies(pt_ref[seq * npg + pg0 + i], slot_, pg0 + i):
                cp.start()

    @pl.when(b == 0)
    def _():
        def body(pg, c):
            start_pages(0, 0, pg, 1)
            return c
        lax.fori_loop(0, npg, body, 0)
        lhs_s[0:nk, :] = wkt_ref[...]
        latb_s[1] = jnp.zeros(latb_s.shape[1:], bf16)
        s_s[1] = jnp.zeros(s_s.shape[1:], f32)
        car_s[...] = jnp.zeros_like(car_s)

    lhs_s[nk:nk + hp, :] = qabs_ref[...]

    def wait_slot(slot_):
        pltpu.make_async_copy(lat_hbm.at[layer, pl.ds(0, npg)], lat_buf.at[slot_], sems.at[0, slot_]).wait()
        pltpu.make_async_copy(krt_hbm.at[layer, pl.ds(0, npg)], kr_buf.at[slot_], sems.at[1, slot_]).wait()

    wait_slot(slot)

    def scores(latb, krt):
        n = latb.shape[0]
        big = _dot_nt(lhs_s[...], latb)
        knt = big[0:nk]
        ssq = jnp.sum((knt * knt).reshape(n_heads, _NOPE, n), axis=1)
        rs = lax.rsqrt(ssq * (1.0 / _NOPE) + EPS)
        sr = _dot(qr_ref[...], krt)
        s8 = big[nk:nk + n_heads] * rs + sr[0:n_heads]
        return jnp.concatenate([s8, jnp.zeros((hp - n_heads, n), f32)], axis=0)

    def softmax_update(carry, parts):
        m_run, l_run, acc = carry
        m_new = m_run
        for _, s in parts:
            m_new = jnp.maximum(m_new, jnp.max(s, axis=-1, keepdims=True))
        alpha = jnp.exp2(m_run - m_new)
        l_new = alpha * l_run
        acc = alpha * acc
        for latb, s in parts:
            pm = jnp.exp2(s - jnp.concatenate([m_new] * (s.shape[-1] // LANES), axis=-1))
            l_new = l_new + jnp.sum(pm, axis=-1, keepdims=True)
            acc = acc + _dot(pm.astype(bf16), latb)
        return m_new, l_new, acc

    nxt = jnp.minimum(b + 1, nseq - 1)
    nsub = npg // ppsub

    def stage(j, st, with_new_token=False):
        start_pages(nxt, 1 - slot, j * ppsub, ppsub)
        p0 = pl.multiple_of(j * ppsub, ppsub)
        lat = [lat_buf[slot, pl.ds(p0, ppsub)].reshape(tk, kl).astype(bf16)]
        krt = [kr_buf[slot, p0 + i].astype(bf16) for i in range(ppsub)]
        if with_new_token:
            lat.append(jnp.broadcast_to(latn_ref[...], (page, kl)).astype(bf16))
            krt.append(jnp.broadcast_to(krn_ref[...], (krn_ref.shape[0], page)).astype(bf16))
        latb = jnp.concatenate(lat, axis=0)
        s = scores(latb, jnp.concatenate(krt, axis=-1))
        n = latb.shape[0]
        if with_new_token:
            s = jnp.where(lax.broadcasted_iota(jnp.int32, s.shape, 1) <= tk, s, -1e30)
        latb_s[st, 0:n, :] = latb
        s_s[st, :, 0:n] = s

    def stashed(st, n=tk):
        return [(latb_s[st, 0:n, :], s_s[st, :, 0:n])]

    stage(0, 0)
    _, l_prev, acc_prev = softmax_update((car_s[0], car_s[1], car_s[2]), stashed(1, tk + page))
    o_ref[...] = acc_prev / l_prev

    def pair(k, carry):
        stage(2 * k + 1, 1)
        carry = softmax_update(carry, stashed(0))
        stage(2 * k + 2, 0)
        return softmax_update(carry, stashed(1))

    carry = (jnp.full((hp, LANES), -1e30, f32), jnp.zeros((hp, LANES), f32), jnp.zeros((hp, kl), f32))
    carry = lax.fori_loop(0, (nsub - 2) // 2, pair, carry)
    stage(nsub - 1, 1, with_new_token=True)
    carry = softmax_update(carry, stashed(0))
    for i in range(3):
        car_s[i] = carry[i]

    @pl.when(b == nseq)
    def _():
        wait_slot(1 - slot)


def _sattn_call(page_table, cache_lat, cache_krt, layer, wkt, qabs, qr, latn, krn, n_heads, ppsub):
    nb, npg = page_table.shape
    _, _, page, kl = cache_lat.shape
    rd = cache_krt.shape[2]
    hp = qabs.shape[1]
    nk = wkt.shape[0]
    assert npg % (2 * ppsub) == 0 and kl == LANES
    cur = lambda bi, pt: (jnp.minimum(bi, nb - 1), 0, 0)
    grid_spec = pltpu.PrefetchScalarGridSpec(
        num_scalar_prefetch=1,
        grid=(nb + 1,),
        in_specs=[pl.BlockSpec(memory_space=pl.ANY),
                  pl.BlockSpec(memory_space=pl.ANY),
                  pl.BlockSpec((nk, kl), lambda bi, pt: (0, 0)),
                  pl.BlockSpec((None, hp, kl), cur),
                  pl.BlockSpec((None, hp, rd), cur),
                  pl.BlockSpec((None, 1, kl), cur),
                  pl.BlockSpec((None, rd, 1), cur)],
        out_specs=pl.BlockSpec((None, hp, kl), lambda bi, pt: (jnp.maximum(bi - 1, 0), 0, 0)),
        scratch_shapes=[pltpu.VMEM((2, npg, page, kl), f32), pltpu.VMEM((2, npg, rd, page), f32),
                        pltpu.VMEM((nk + hp, kl), bf16), pltpu.VMEM((2, (ppsub + 1) * page, kl), bf16),
                        pltpu.VMEM((2, hp, (ppsub + 1) * page), f32), pltpu.VMEM((3, hp, LANES), f32),
                        pltpu.SemaphoreType.DMA((2, 2))])
    return pl.pallas_call(
        functools.partial(_sattn_kernel, layer=layer, n_heads=n_heads, ppsub=ppsub),
        grid_spec=grid_spec,
        out_shape=jax.ShapeDtypeStruct((nb, hp, kl), f32),
        compiler_params=pltpu.CompilerParams(dimension_semantics=("arbitrary",), vmem_limit_bytes=VMEM_LIMIT),
        name="sample_attn",
    )(page_table.reshape(-1), cache_lat, cache_krt, wkt, qabs, qr, latn, krn)


def _spost_kernel(x_ref, lru_ref, olat_ref, g1_ref, sh2_ref, sc2_ref, g2_ref, mog_ref, n2g_ref, wv_ref, wo_ref,
                  wup_ref, fcw_ref, fcb_ref, wdn_ref, fbuf_ref, y_o, up_o, *, n_heads):
    nc = wdn_ref.shape[0]
    ck = wdn_ref.shape[1]
    kconv = fcw_ref.shape[1]
    heads = [_dot(olat_ref[hd].astype(bf16), wv_ref[:, hd * LANES:(hd + 1) * LANES]) for hd in range(n_heads)]
    attn = jnp.concatenate([heads[2 * pp] + heads[2 * pp + 1] for pp in range(n_heads // 2)], axis=-1)
    attn = _rms(attn, mog_ref[...]).astype(bf16)
    x1, h2 = _mix_and_norm2(x_ref[...], lru_ref[...], attn, wo_ref, g1_ref[...], n2g_ref[...], sc2_ref[...], sh2_ref[...])

    def conv(up, c):
        out = fcb_ref[c]
        cw = fcw_ref[c]
        for j in range(kconv - 1):
            out = out + fbuf_ref[j, :, c * ck:(c + 1) * ck] * cw[j:j + 1, :]
        up_o[:, c * ck:(c + 1) * ck] = up
        return out + up * cw[kconv - 1:kconv, :]

    acc = jnp.zeros(x1.shape, f32)
    for c in range(nc):
        val = conv(_dot(h2, wup_ref[c]), c)
        gt = conv(_dot(h2, wup_ref[nc + c]), nc + c)
        acc = acc + _dot((jax.nn.gelu(gt) * val).astype(bf16), wdn_ref[c])
    y_o[...] = x1 + g2_ref[...] * acc


def _spost_call(x, lru, olat, g1, sh2, sc2, g2, fbuf, p, n_heads):
    nb, d = x.shape
    nc, ck, _ = p["wdn"].shape
    return pl.pallas_call(
        functools.partial(_spost_kernel, n_heads=n_heads),
        out_shape=[jax.ShapeDtypeStruct((nb, d), f32), jax.ShapeDtypeStruct((nb, 2 * nc * ck), f32)],
        compiler_params=pltpu.CompilerParams(vmem_limit_bytes=VMEM_LIMIT),
        name="sample_ffn",
    )(x, lru, olat, g1, sh2, sc2, g2, p["mog"], p["n2g"], p["wv"], p["wo"], p["wup"], p["fcw"], p["fcb"], p["wdn"], fbuf)


def _take_cols(wmat, idx):
    padded = jnp.concatenate([wmat, jnp.zeros(wmat.shape[:-1] + (1,), wmat.dtype)], axis=-1)
    return jnp.take(padded, jnp.asarray(np.where(idx < 0, wmat.shape[-1], idx)), axis=-1)


def _block_diag_groups(wh):
    nh, hd, _ = wh.shape
    per = MXU_DIM // hd
    groups = []
    for g in range(nh // per):
        blk = jnp.zeros((MXU_DIM, MXU_DIM), wh.dtype)
        for j in range(per):
            blk = lax.dynamic_update_slice(blk, wh[g * per + j], (j * hd, j * hd))
        groups.append(blk)
    return jnp.stack(groups).astype(bf16)


def _prep_layer(l, n_heads, scale, w_in, lru_conv_w, lru_conv_b, lru_w_a, lru_b_a, lru_w_x, lru_b_x, lru_lambda,
                norm1_g, q_lora_norm_g, w_uq, q_nope_norm_g, q_rope_norm_g, kv_lora_norm_g, k_rope_norm_g, w_ukv,
                k_nope_norm_g, lru_out_norm_g, mla_out_norm_g, w_o, norm2_g, w_up, ffn_conv_w, ffn_conv_b, w_down):
    src = _h128_src()
    w = lru_lambda.shape[-1]
    ql = q_lora_norm_g.shape[-1]
    kl = kv_lora_norm_g.shape[-1]
    qk = _NOPE + _ROPE
    vd = w_ukv.shape[-1] // n_heads - _NOPE
    row = lambda v: v.reshape(1, -1).astype(f32)

    kr_src = np.where(src >= _NOPE, src - _NOPE, -1)
    win = w_in[l]
    base = 2 * w + ql + kl
    win_ext = jnp.concatenate([win[:, :base], _take_cols(win[:, base:], kr_src)], axis=-1).astype(bf16)

    q_idx = np.concatenate([np.where(src >= 0, src + h * qk, -1) for h in range(n_heads)])
    wuq = _take_cols(w_uq[l], q_idx).astype(bf16)
    k_src = np.where((src >= 0) & (src < _NOPE), src, -1)
    k_idx = np.concatenate([np.where(k_src >= 0, k_src + h * (_NOPE + vd), -1) for h in range(n_heads)])
    wk = _take_cols(w_ukv[l], k_idx).astype(bf16)
    v_idx = []
    vone = np.zeros((1, n_heads * LANES), np.float32)
    for h in range(n_heads):
        slab = np.full((LANES,), -1, np.int64)
        off = (h % 2) * vd
        slab[off:off + vd] = h * (_NOPE + vd) + _NOPE + np.arange(vd)
        v_idx.append(slab)
        vone[0, h * LANES + (vd - off)] = 1.0
    wv = _take_cols(w_ukv[l], np.concatenate(v_idx)).astype(bf16)
    wkt_idx = np.concatenate([h * (_NOPE + vd) + np.arange(_NOPE) for h in range(n_heads)])
    wkt = jnp.take(w_ukv[l], jnp.asarray(wkt_idx), axis=-1).T.astype(bf16)

    nope_tab = lambda g: _take_cols(g.reshape(1, -1), k_src)
    rope_tab = lambda g: _take_cols(g.reshape(1, -1), kr_src)
    is_n = (k_src >= 0).astype(np.float32)
    is_r = (kr_src >= 0).astype(np.float32)
    msel = jnp.asarray(np.outer(is_n, is_n) / _NOPE + np.outer(is_r, is_r) / _ROPE).astype(bf16)
    gq = (nope_tab(q_nope_norm_g[l]) + rope_tab(q_rope_norm_g[l])) * scale

    dff = w_down.shape[1]
    ck = MXU_DIM
    nc = dff // ck
    wup = w_up[l].reshape(w_up.shape[1], 2 * nc, ck).transpose(1, 0, 2).astype(bf16)
    fcw = ffn_conv_w[l].reshape(-1, 2 * nc, ck).transpose(1, 0, 2).astype(f32)
    fcb = ffn_conv_b[l].reshape(2 * nc, 1, ck).astype(f32)
    wdn = w_down[l].reshape(nc, ck, -1).astype(bf16)
    return dict(
        n1g=row(norm1_g[l]), win=win_ext, cw=lru_conv_w[l].astype(f32), cb=row(lru_conv_b[l]),
        wa=_block_diag_groups(lru_w_a[l]), ba=row(lru_b_a[l]), wx=_block_diag_groups(lru_w_x[l]), bx=row(lru_b_x[l]),
        lam=row(lru_lambda[l]), qlg=row(q_lora_norm_g[l]), wuq=wuq, gq=gq.astype(f32), msel=msel,
        kvg=row(kv_lora_norm_g[l]), gkr=rope_tab(k_rope_norm_g[l]).astype(f32), wk=wk,
        gk=nope_tab(k_nope_norm_g[l]).astype(f32), wv=wv, vone=jnp.asarray(vone), wkt=wkt, log=row(lru_out_norm_g[l]),
        mog=row(mla_out_norm_g[l]), wo=w_o[l].astype(bf16), n2g=row(norm2_g[l]), wup=wup, fcw=fcw, fcb=fcb, wdn=wdn)


def _rope_tables(pos):
    half = _ROPE // 2
    inv = ROPE_THETA ** (-np.arange(0, _ROPE, 2, dtype=np.float64) / _ROPE)
    ang = np.asarray(pos, np.float64)[:, None] * inv[None, :]
    cos, sin = np.cos(ang), np.sin(ang)
    n = ang.shape[0]
    cc = np.zeros((n, LANES), np.float32)
    cc[:, 0:_R1_LO] = 1.0
    cc[:, _R1_LO + half:_R1_LO + half + 16] = 1.0
    cc[:, _R1_LO:_R1_LO + half] = cos
    cc[:, _R2_LO:_R2_LO + half] = cos
    ss = np.zeros((n, LANES), np.float32)
    ss[:, _R1_LO:_R1_LO + half] = -sin
    ss[:, _R2_LO:_R2_LO + half] = sin
    return jnp.asarray(cc), jnp.asarray(ss)


def _gain_rope_tables(cc, ss, gq, gkr):
    swap = lambda g: jnp.roll(g, LANES // 2, axis=-1)
    return cc * gq, ss * swap(gq), cc * gkr, ss * swap(gkr)


def kernel(x_prompt, x_sample, cache_kv_latent, cache_k_rope, state_lru_h, state_lru_conv, state_ffn_conv,
           page_table, c_prompt, c_sample, w_ada, b_ada, norm1_g, w_in, lru_conv_w, lru_conv_b, lru_w_a, lru_b_a,
           lru_w_x, lru_b_x, lru_lambda, q_lora_norm_g, w_uq, q_nope_norm_g, q_rope_norm_g, kv_lora_norm_g,
           k_rope_norm_g, w_ukv, k_nope_norm_g, lru_out_norm_g, mla_out_norm_g, w_o, norm2_g, w_up, ffn_conv_w,
           ffn_conv_b, w_down):
    b, s, d = x_prompt.shape
    nb, ds, _ = x_sample.shape
    depth = w_in.shape[0]
    assert ds == 1 and q_nope_norm_g.shape[-1] == _NOPE and q_rope_norm_g.shape[-1] == _ROPE
    n_heads = w_uq.shape[-1] // (_NOPE + _ROPE)
    scale = float(_NOPE + _ROPE) ** -0.5 * float(np.log2(np.e))
    npg = page_table.shape[1]
    n_past = npg * cache_kv_latent.shape[2]
    ts = 64 if s % 64 == 0 else 32
    tq = min(1024, s)
    tk = min(256, s)
    tm = min(512, s)
    ppsub = 16 if npg % 32 == 0 else 1
    hp = 16

    cache_krt = jnp.swapaxes(cache_k_rope, 2, 3)
    cc_p, ss_p = _rope_tables(np.arange(s))
    cc_s, ss_s = _rope_tables(n_past + np.arange(1))

    y_p = x_prompt
    y_s = x_sample.reshape(nb, d)
    c_all = jnp.concatenate([c_prompt, c_sample], axis=0)
    outs_p = [[] for _ in range(5)]
    outs_s = [[] for _ in range(5)]
    for l in range(depth):
        p = _prep_layer(l, n_heads, scale, w_in, lru_conv_w, lru_conv_b, lru_w_a, lru_b_a, lru_w_x, lru_b_x,
                        lru_lambda, norm1_g, q_lora_norm_g, w_uq, q_nope_norm_g, q_rope_norm_g, kv_lora_norm_g,
                        k_rope_norm_g, w_ukv, k_nope_norm_g, lru_out_norm_g, mla_out_norm_g, w_o, norm2_g, w_up,
                        ffn_conv_w, ffn_conv_b, w_down)
        mod = _mod_call(c_all, w_ada[l], b_ada[l])
        mp = [m_[:, None, :] for m_ in jnp.split(mod[:b], 6, axis=-1)]
        ms = jnp.split(mod[b:], 6, axis=-1)

        lru_p, q_p, k_p, v_p, kvlat_p, krope_p, hlast_p, xtail_p = _inproj_call(
            y_p, mp[0], mp[1], _gain_rope_tables(cc_p, ss_p, p["gq"], p["gkr"]), p, n_heads, ts)
        attn_p = _attn_call(q_p, k_p, v_p, p["mog"], n_heads, tq, tk)
        y_p, ftail_p = _ffn_call(y_p, lru_p, attn_p, mp[2], mp[3], mp[4], mp[5], p, tm)
        kc = lru_conv_w.shape[1]
        fk = ffn_conv_w.shape[1]
        for j, o in enumerate((kvlat_p, krope_p, hlast_p, xtail_p[:, 8 - (kc - 1):], ftail_p[:, 8 - (fk - 1):])):
            outs_p[j].append(o)

        cbuf = jnp.swapaxes(state_lru_conv[l], 0, 1)
        fbuf = jnp.swapaxes(state_ffn_conv[l], 0, 1)
        lru_s, xlru_s, hnew_s, qabs, qr, kvn_s, kr_s = _spre_call(
            y_s, ms[0], ms[1], _gain_rope_tables(cc_s, ss_s, p["gq"], p["gkr"]), cbuf, state_lru_h[l], p, n_heads,
            n_past == 0)
        pad_heads = lambda t: jnp.pad(jnp.swapaxes(t, 0, 1), ((0, 0), (0, hp - n_heads), (0, 0)))
        olat = _sattn_call(page_table, cache_kv_latent, cache_krt, l, p["wkt"], pad_heads(qabs), pad_heads(qr),
                           kvn_s[:, None, :], kr_s[:, :, None], n_heads, ppsub)
        olat = jnp.swapaxes(olat[:, :n_heads], 0, 1)
        y_s, up_s = _spost_call(y_s, lru_s, olat, ms[2], ms[3], ms[4], ms[5], fbuf, p, n_heads)
        lru_conv_new = jnp.concatenate([state_lru_conv[l][:, 1:], xlru_s[:, None, :]], axis=1)
        ffn_conv_new = jnp.concatenate([state_ffn_conv[l][:, 1:], up_s[:, None, :]], axis=1)
        for j, o in enumerate((kvn_s[:, None, :], kr_s[:, None, :], hnew_s, lru_conv_new, ffn_conv_new)):
            outs_s[j].append(o)

    return (y_p, y_s.reshape(nb, 1, d), *[jnp.stack(o) for o in outs_p], *[jnp.stack(o) for o in outs_s])
```

```python
import functools

import numpy as np
import jax
import jax.numpy as jnp
from jax import lax
from jax.experimental import pallas as pl
from jax.experimental.pallas import tpu as pltpu

f32 = jnp.float32
bf16 = jnp.bfloat16

EPS = 1e-6
LRU_C = 8.0
ROPE_THETA = 10000.0
LANES = 128
MXU_DIM = 256
VMEM_LIMIT = 56 * 1024 * 1024

_NOPE, _ROPE = 64, 32
_R1_LO, _R2_LO = 48, 112


def _h128_src():
    src = np.full((LANES,), -1, np.int32)
    src[0:48] = np.arange(0, 48)
    src[48:64] = _NOPE + np.arange(0, 16)
    src[64:80] = np.arange(48, 64)
    src[112:128] = _NOPE + 16 + np.arange(0, 16)
    return src


def _dot(a, b):
    return jnp.dot(a, b, preferred_element_type=f32)


def _dot_nt(a, b):
    return lax.dot_general(a, b, (((1,), (1,)), ((), ())), preferred_element_type=f32)


def _rms(x, g):
    ms = jnp.mean(x * x, axis=-1, keepdims=True)
    return x * lax.rsqrt(ms + EPS) * g


def _neg_expm1_2x(y):
    t = jnp.tanh(y)
    return -2.0 * t / (1.0 - t)


def _lru_gates(x_conv, wa_ref, ba, wx_ref, bx, lam):
    xb = x_conv.astype(bf16)
    ng = wa_ref.shape[0]
    ra = jnp.concatenate([_dot(xb[:, g * MXU_DIM:(g + 1) * MXU_DIM], wa_ref[g]) for g in range(ng)], axis=-1) + ba
    ia = jnp.concatenate([_dot(xb[:, g * MXU_DIM:(g + 1) * MXU_DIM], wx_ref[g]) for g in range(ng)], axis=-1) + bx
    r = jax.nn.sigmoid(ra)
    ig = jax.nn.sigmoid(ia)
    log_a = r * ((-LRU_C) * jax.nn.softplus(-lam))
    a = jnp.exp(log_a)
    mult = jnp.sqrt(_neg_expm1_2x(log_a))
    return a, mult, ig * x_conv


def _group_ms(x, msel):
    return _dot((x * x).astype(bf16), msel)


def _norm_rope(x, msel, ca, sb, to3d):
    inv = lax.rsqrt(_group_ms(x, msel) + EPS)
    return to3d(inv) * (to3d(x) * ca + to3d(pltpu.roll(x, LANES // 2, 1)) * sb)


def _q_heads(q, msel, ca, sb, n_heads, to3d):
    return [_norm_rope(q[:, h * LANES:(h + 1) * LANES], msel, ca, sb, to3d) for h in range(n_heads)]


def _mod_kernel(c_ref, w_ref, b_ref, o_ref):
    c = c_ref[...]
    sc = (c * jax.nn.sigmoid(c)).astype(bf16)
    o_ref[...] = _dot(sc, w_ref[...].astype(bf16)) + b_ref[...]


def _mod_call(c_all, w_ada, b_ada):
    m, d = c_all.shape
    n = w_ada.shape[1]
    tn = 1536 if n % 1536 == 0 else n
    return pl.pallas_call(
        _mod_kernel,
        grid=(n // tn,),
        in_specs=[pl.BlockSpec((m, d), lambda j: (0, 0)),
                  pl.BlockSpec((d, tn), lambda j: (0, j)),
                  pl.BlockSpec((1, tn), lambda j: (0, j))],
        out_specs=pl.BlockSpec((m, tn), lambda j: (0, j)),
        out_shape=jax.ShapeDtypeStruct((m, n), f32),
        compiler_params=pltpu.CompilerParams(dimension_semantics=("arbitrary",), vmem_limit_bytes=VMEM_LIMIT),
        name="adaln_mod",
    )(c_all, w_ada, b_ada.reshape(1, n))


def _inproj_kernel(x_ref, sh_ref, sc_ref, qca_ref, qsb_ref, kca_ref, ksb_ref, win_ref, cw_ref, cb_ref,
                   wa_ref, ba_ref, wx_ref, bx_ref, lam_ref, qlg_ref, wuq_ref, msel_ref,
                   kvg_ref, wk_ref, wv_ref, vone_ref, log_ref,
                   lru_o, q_o, k_o, v_o, kvlat_o, krope_o, hlast_o, xtail_o,
                   xtail_s, a_s, u_s, hs_s, h_s, *, n_heads):
    i = pl.program_id(0)
    nb, ts, d = x_ref.shape
    m = nb * ts
    w = lam_ref.shape[-1]
    ql = qlg_ref.shape[-1]
    kl = kvg_ref.shape[-1]
    kconv = cw_ref.shape[0]
    nlc = w // LANES

    @pl.when(i == 0)
    def _():
        xtail_s[...] = jnp.zeros_like(xtail_s)
        h_s[...] = jnp.zeros_like(h_s)

    def to3d(v):
        return v.reshape(nb, ts, v.shape[-1])

    x = x_ref[...]
    h = _rms(x, sc_ref[...]) + sh_ref[...]
    z = _dot(h.reshape(m, d).astype(bf16), win_ref[...])
    x_lru = z[:, 0:w]
    g_lru = z[:, w:2 * w]
    q_lat = z[:, 2 * w:2 * w + ql]
    kv_lat = z[:, 2 * w + ql:2 * w + ql + kl]
    kr_pre = z[:, 2 * w + ql + kl:2 * w + ql + kl + LANES]

    t_idx = lax.broadcasted_iota(jnp.int32, (m, 1), 0) & (ts - 1)
    r8 = lax.broadcasted_iota(jnp.int32, (1, 8, 1), 1)
    tail2d = xtail_s[...].reshape(nb * 8, w)
    x_conv = to3d(jnp.broadcast_to(cb_ref[...], (m, w)))
    for j in range(kconv - 1):
        k = kconv - 1 - j
        rolled = to3d(pltpu.roll(x_lru, k, 0))
        prev = pltpu.roll(tail2d, nb * 8 + k - 8, 0).reshape(nb, 8, w)
        sh = jnp.concatenate([jnp.where(r8 < k, prev, rolled[:, 0:8]), rolled[:, 8:]], axis=1)
        x_conv = x_conv + sh * cw_ref[j:j + 1, :]
    x_conv = (x_conv + to3d(x_lru) * cw_ref[kconv - 1:kconv, :]).reshape(m, w)
    xtail_s[...] = to3d(x_lru)[:, ts - 8:, :]
    xtail_o[...] = to3d(x_lru)[:, ts - 8:, :]

    a, mult, ux = _lru_gates(x_conv, wa_ref, ba_ref[...], wx_ref, bx_ref[...], lam_ref[...])
    mult = jnp.where(jnp.logical_and(t_idx == 0, i == 0), 1.0, mult)
    u = mult * ux
    pitch = a_s.shape[1] // nb
    for j in range(nlc):
        for bi in range(nb):
            a_s[j, bi * pitch:bi * pitch + ts, :] = a[bi * ts:(bi + 1) * ts, j * LANES:(j + 1) * LANES]
            u_s[j, bi * pitch:bi * pitch + ts, :] = u[bi * ts:(bi + 1) * ts, j * LANES:(j + 1) * LANES]

    def scan_step(t, hc):
        out = []
        for j in range(nlc):
            hj = a_s[j, pl.ds(t, nb, stride=pitch), :] * hc[j] + u_s[j, pl.ds(t, nb, stride=pitch), :]
            hs_s[j, pl.ds(t, nb, stride=pitch), :] = hj
            out.append(hj)
        return tuple(out)

    hc = lax.fori_loop(0, ts, scan_step, tuple(h_s[j] for j in range(nlc)), unroll=8)
    for j in range(nlc):
        h_s[j] = hc[j]
    hlast_o[...] = jnp.concatenate(list(hc), axis=-1)
    hs = jnp.concatenate(
        [jnp.concatenate([hs_s[j, bi * pitch:bi * pitch + ts, :] for bi in range(nb)], axis=0) for j in range(nlc)],
        axis=-1)
    lru_out = hs * jax.nn.gelu(g_lru)
    lru_o[...] = to3d(_rms(lru_out, log_ref[...]).astype(bf16))

    msel = msel_ref[...]
    qn = _rms(q_lat, qlg_ref[...]).astype(bf16)
    q = _dot(qn, wuq_ref[...])
    for hd, o in enumerate(_q_heads(q, msel, qca_ref[...][None], qsb_ref[...][None], n_heads, to3d)):
        q_o[:, :, hd * LANES:(hd + 1) * LANES] = o.astype(bf16)

    kvn = _rms(kv_lat, kvg_ref[...])
    kvlat_o[...] = to3d(kvn)
    kr = _norm_rope(kr_pre, msel, kca_ref[...][None], ksb_ref[...][None], to3d)
    krope_o[...] = jnp.concatenate([kr[:, :, _R1_LO:_R1_LO + 16], kr[:, :, _R2_LO:_R2_LO + 16]], axis=-1)
    kvb = kvn.astype(bf16)
    kk = _dot(kvb, wk_ref[...])
    v_o[...] = to3d((_dot(kvb, wv_ref[...]) + vone_ref[...]).astype(bf16))
    for hd in range(n_heads):
        kh = kk[:, hd * LANES:(hd + 1) * LANES]
        khn = kh * lax.rsqrt(_group_ms(kh, msel) + EPS)
        k_o[:, :, hd * LANES:(hd + 1) * LANES] = (to3d(khn) + kr).astype(bf16)


def _const_spec(shape):
    nd = len(shape)
    return pl.BlockSpec(shape, lambda *_: (0,) * nd, pipeline_mode=pl.Buffered(1))


def _inproj_call(x, sh1, sc1, rope_tabs, p, n_heads, ts):
    b, s, d = x.shape
    w = p["lam"].shape[-1]
    kl = p["kvg"].shape[-1]
    hw = n_heads * LANES
    m = b * ts
    consts = [p["win"], p["cw"], p["cb"], p["wa"], p["ba"], p["wx"], p["bx"], p["lam"], p["qlg"],
              p["wuq"], p["msel"], p["kvg"], p["wk"], p["wv"], p["vone"], p["log"]]
    in_specs = ([pl.BlockSpec((b, ts, d), lambda i: (0, i, 0)), _const_spec(sh1.shape), _const_spec(sc1.shape)]
                + [pl.BlockSpec((ts, LANES), lambda i: (i, 0))] * len(rope_tabs)
                + [_const_spec(c.shape) for c in consts])

    def tile(n, dt):
        return pl.BlockSpec((b, ts, n), lambda i: (0, i, 0)), jax.ShapeDtypeStruct((b, s, n), dt)

    outs = [tile(w, bf16), tile(hw, bf16), tile(hw, bf16), tile(hw, bf16), tile(kl, f32), tile(_ROPE, f32),
            (pl.BlockSpec((b, w), lambda i: (0, 0)), jax.ShapeDtypeStruct((b, w), f32)),
            (pl.BlockSpec((b, 8, w), lambda i: (0, 0, 0)), jax.ShapeDtypeStruct((b, 8, w), f32))]
    nlc = w // LANES
    return pl.pallas_call(
        functools.partial(_inproj_kernel, n_heads=n_heads),
        grid=(s // ts,),
        in_specs=in_specs,
        out_specs=[o[0] for o in outs],
        out_shape=[o[1] for o in outs],
        scratch_shapes=[pltpu.VMEM((b, 8, w), f32)] + [pltpu.VMEM((nlc, b * (ts + 8), LANES), f32)] * 3
        + [pltpu.VMEM((nlc, b, LANES), f32)],
        compiler_params=pltpu.CompilerParams(dimension_semantics=("arbitrary",), vmem_limit_bytes=VMEM_LIMIT),
        name="prompt_inproj",
    )(x, sh1, sc1, *rope_tabs, *consts)


def _attn_kernel(q_ref, k_ref, v_ref, g_ref, o_ref, m_s, acc_s, *, n_heads, tk):
    qi = pl.program_id(1)
    tq = q_ref.shape[1]
    nmask = tq // tk
    n_full = qi * nmask
    m_s[...] = jnp.full_like(m_s, -1e30)
    acc_s[...] = jnp.zeros_like(acc_s)

    def block(j, r0, masked):
        start = pl.multiple_of(j * tk, tk)
        if masked:
            vis = (lax.broadcasted_iota(jnp.int32, (tq - r0, tk), 1)
                   <= lax.broadcasted_iota(jnp.int32, (tq - r0, tk), 0))
        for hd in range(n_heads):
            hs = slice(hd * LANES, (hd + 1) * LANES)
            s = _dot_nt(q_ref[0, r0:, hs], k_ref[0, pl.ds(start, tk), hs])
            if masked:
                s = jnp.where(vis, s, -1e30)
            mx = m_s[hd, r0:, :]
            m_new = jnp.maximum(mx, jnp.max(s, axis=-1, keepdims=True))
            alpha = jnp.exp2(mx - m_new)
            pm = jnp.exp2(s - jnp.concatenate([m_new] * (tk // LANES), axis=-1))
            acc_s[hd, r0:, :] = alpha * acc_s[hd, r0:, :] + _dot(pm.astype(bf16), v_ref[0, pl.ds(start, tk), hs])
            m_s[hd, r0:, :] = m_new

    def full_block(j, c):
        block(j, 0, False)
        return c

    lax.fori_loop(0, n_full, full_block, 0)
    for jm in range(nmask):
        block(n_full + jm, jm * tk, True)
    lane = lax.broadcasted_iota(jnp.int32, (1, LANES), 1)
    half = LANES // 2
    heads = []
    for hd in range(n_heads):
        acc = acc_s[hd]
        lo = (hd % 2) * half
        one_lane = (half - lo)
        l = jnp.sum(jnp.where(lane == one_lane, acc, 0.0), axis=-1, keepdims=True)
        heads.append(jnp.where((lane >= lo) & (lane < lo + half), acc, 0.0) / l)
    o = jnp.concatenate([heads[2 * pp] + heads[2 * pp + 1] for pp in range(n_heads // 2)], axis=-1)
    o_ref[0] = _rms(o, g_ref[...]).astype(bf16)


def _attn_call(q, k, v, g, n_heads, tq, tk):
    b, s, hw = q.shape
    wout = g.shape[-1]
    return pl.pallas_call(
        functools.partial(_attn_kernel, n_heads=n_heads, tk=tk),
        grid=(b, s // tq),
        in_specs=[pl.BlockSpec((1, tq, hw), lambda bi, qi: (bi, qi, 0)),
                  pl.BlockSpec((1, s, hw), lambda bi, qi: (bi, 0, 0)),
                  pl.BlockSpec((1, s, hw), lambda bi, qi: (bi, 0, 0)),
                  pl.BlockSpec((1, wout), lambda bi, qi: (0, 0))],
        out_specs=pl.BlockSpec((1, tq, wout), lambda bi, qi: (bi, qi, 0)),
        out_shape=jax.ShapeDtypeStruct((b, s, wout), bf16),
        scratch_shapes=[pltpu.VMEM((n_heads, tq, LANES), f32), pltpu.VMEM((n_heads, tq, LANES), f32)],
        compiler_params=pltpu.CompilerParams(dimension_semantics=("arbitrary", "arbitrary"),
                                             vmem_limit_bytes=VMEM_LIMIT),
        name="prompt_attn",
    )(q, k, v, g)


def _mix_and_norm2(x, lru, attn, wo_ref, g1, gs2, sh2):
    w = lru.shape[-1]
    mixed = _dot(lru, wo_ref[pl.ds(0, w), :]) + _dot(attn, wo_ref[pl.ds(w, attn.shape[-1]), :])
    x1 = x + g1 * mixed
    h2 = (_rms(x1, gs2) + sh2).astype(bf16)
    return x1, h2


def _up_cols(wup_ref, c, ck):
    return wup_ref[:, c * ck:(c + 1) * ck]


def _ffn_kernel(x_ref, lru_ref, attn_ref, g1_ref, sh2_ref, gs2_ref, g2_ref, wo_ref, wup_ref,
                fcw_ref, fcb_ref, wdn_ref, y_o, tail_o, tail_s, act_s):
    si = pl.program_id(1)
    tm = x_ref.shape[1]
    nc = fcw_ref.shape[0] // 2
    ck = fcw_ref.shape[2]
    kconv = fcw_ref.shape[1]

    @pl.when(si == 0)
    def _():
        tail_s[...] = jnp.zeros_like(tail_s)

    x1, h2 = _mix_and_norm2(x_ref[0], lru_ref[0], attn_ref[0], wo_ref, g1_ref[0], gs2_ref[0], sh2_ref[0])
    r8 = lax.broadcasted_iota(jnp.int32, (8, 1), 0)

    def conv(up, c):
        tail = tail_s[c]
        first = up[0:8]
        out = fcb_ref[c]
        cw = fcw_ref[c]
        for j in range(kconv - 1):
            k = kconv - 1 - j
            head = jnp.where(r8 < k, pltpu.roll(tail, k, 0), pltpu.roll(first, k, 0))
            sh = jnp.concatenate([head, pltpu.roll(up, k, 0)[8:]], axis=0)
            out = out + sh * cw[j:j + 1, :]
        out = out + up * cw[kconv - 1:kconv, :]
        tail_s[c] = up[tm - 8:]
        tail_o[0, :, c * ck:(c + 1) * ck] = up[tm - 8:]
        return out

    for c in range(nc):
        val = conv(_dot(h2, _up_cols(wup_ref, c, ck)), c)
        gt = conv(_dot(h2, _up_cols(wup_ref, nc + c, ck)), nc + c)
        act_s[:, c * ck:(c + 1) * ck] = (jax.nn.gelu(gt) * val).astype(bf16)
    acc = _dot(act_s[...], wdn_ref[...])
    y_o[0] = x1 + g2_ref[0] * acc


def _ffn_call(x, lru, attn, g1, sh2, gs2, g2, p, tm):
    b, s, d = x.shape
    w = lru.shape[-1]
    nc2, _, ck = p["fcw"].shape
    nc = nc2 // 2
    row = lambda n: pl.BlockSpec((1, tm, n), lambda bi, si: (bi, si, 0))
    modspec = pl.BlockSpec((1, 1, d), lambda bi, si: (bi, 0, 0))
    consts = [p["wo"], p["wup"], p["fcw"], p["fcb"], p["wdn"]]
    return pl.pallas_call(
        _ffn_kernel,
        grid=(b, s // tm),
        in_specs=[row(d), row(w), row(attn.shape[-1]), modspec, modspec, modspec, modspec]
        + [_const_spec(c.shape) for c in consts],
        out_specs=[row(d), pl.BlockSpec((1, 8, 2 * nc * ck), lambda bi, si: (bi, 0, 0))],
        out_shape=[jax.ShapeDtypeStruct((b, s, d), f32), jax.ShapeDtypeStruct((b, 8, 2 * nc * ck), f32)],
        scratch_shapes=[pltpu.VMEM((2 * nc, 8, ck), f32), pltpu.VMEM((tm, nc * ck), bf16)],
        compiler_params=pltpu.CompilerParams(dimension_semantics=("arbitrary", "arbitrary"),
                                             vmem_limit_bytes=VMEM_LIMIT),
        name="prompt_ffn",
    )(x, lru, attn, g1, sh2, gs2, g2, *consts)


def _spre_kernel(x_ref, sh_ref, sc_ref, qca_ref, qsb_ref, kca_ref, ksb_ref, win_ref, cbuf_ref, cw_ref,
                 cb_ref, wa_ref, ba_ref, wx_ref, bx_ref, lam_ref, h0_ref, qlg_ref, wuq_ref, msel_ref,
                 kvg_ref, wk_ref, log_ref,
                 lru_o, xlru_o, hnew_o, qabs_o, qr_o, kvn_o, kr_o, *, n_heads, first_pos):
    w = lam_ref.shape[-1]
    ql = qlg_ref.shape[-1]
    kl = kvg_ref.shape[-1]
    kconv = cw_ref.shape[0]
    ident = lambda v: v

    h = _rms(x_ref[...], sc_ref[...]) + sh_ref[...]
    z = _dot(h.astype(bf16), win_ref[...])
    x_lru = z[:, 0:w]
    g_lru = z[:, w:2 * w]
    q_lat = z[:, 2 * w:2 * w + ql]
    kv_lat = z[:, 2 * w + ql:2 * w + ql + kl]
    kr_pre = z[:, 2 * w + ql + kl:2 * w + ql + kl + LANES]

    x_conv = cb_ref[...]
    for j in range(kconv - 1):
        x_conv = x_conv + cbuf_ref[j] * cw_ref[j:j + 1, :]
    x_conv = x_conv + x_lru * cw_ref[kconv - 1:kconv, :]
    xlru_o[...] = x_lru
    a, mult, ux = _lru_gates(x_conv, wa_ref, ba_ref[...], wx_ref, bx_ref[...], lam_ref[...])
    if first_pos:
        mult = jnp.ones_like(mult)
    hn = a * h0_ref[...] + mult * ux
    hnew_o[...] = hn
    lru_o[...] = _rms(hn * jax.nn.gelu(g_lru), log_ref[...]).astype(bf16)

    msel = msel_ref[...]
    qn = _rms(q_lat, qlg_ref[...]).astype(bf16)
    q = _dot(qn, wuq_ref[...])
    for hd, o in enumerate(_q_heads(q, msel, qca_ref[...], qsb_ref[...], n_heads, ident)):
        wkh = wk_ref[:, hd * LANES:(hd + 1) * LANES]
        qabs_o[hd] = _dot_nt(o.astype(bf16), wkh).astype(bf16)
        qr_o[hd] = jnp.concatenate([o[:, _R1_LO:_R1_LO + 16], o[:, _R2_LO:_R2_LO + 16]], axis=-1).astype(bf16)

    kvn_o[...] = _rms(kv_lat, kvg_ref[...])
    kr = _norm_rope(kr_pre, msel, kca_ref[...], ksb_ref[...], ident)
    kr_o[...] = jnp.concatenate([kr[:, _R1_LO:_R1_LO + 16], kr[:, _R2_LO:_R2_LO + 16]], axis=-1)


def _spre_call(x, sh1, sc1, rope_tabs, cbuf, h0, p, n_heads, first_pos):
    nb, d = x.shape
    w = p["lam"].shape[-1]
    kl = p["kvg"].shape[-1]
    out_shape = [jax.ShapeDtypeStruct((nb, w), bf16), jax.ShapeDtypeStruct((nb, w), f32),
                 jax.ShapeDtypeStruct((nb, w), f32), jax.ShapeDtypeStruct((n_heads, nb, kl), bf16),
                 jax.ShapeDtypeStruct((n_heads, nb, _ROPE), bf16), jax.ShapeDtypeStruct((nb, kl), f32),
                 jax.ShapeDtypeStruct((nb, _ROPE), f32)]
    return pl.pallas_call(
        functools.partial(_spre_kernel, n_heads=n_heads, first_pos=first_pos),
        out_shape=out_shape,
        compiler_params=pltpu.CompilerParams(vmem_limit_bytes=VMEM_LIMIT),
        name="sample_inproj",
    )(x, sh1, sc1, *rope_tabs, p["win"], cbuf, p["cw"], p["cb"], p["wa"], p["ba"], p["wx"], p["bx"],
      p["lam"], h0, p["qlg"], p["wuq"], p["msel"], p["kvg"], p["wk"], p["log"])


def _sattn_kernel(pt_ref, lat_hbm, krt_hbm, wkt_ref, qabs_ref, qr_ref, latn_ref, krn_ref, o_ref,
                  lat_buf, kr_buf, lhs_s, latb_s, s_s, car_s, sems, *, layer, n_heads, ppsub):
    b = pl.program_id(0)
    nseq = pl.num_programs(0) - 1
    _, npg, page, kl = lat_buf.shape
    nk = wkt_ref.shape[0]
    hp = qabs_ref.shape[0]
    tk = ppsub * page
    slot = lax.rem(b, 2)

    def page_copies(src_page, slot_, pg):
        return (pltpu.make_async_copy(lat_hbm.at[layer, src_page], lat_buf.at[slot_, pg], sems.at[0, slot_]),
                pltpu.make_async_copy(krt_hbm.at[layer, src_page], kr_buf.at[slot_, pg], sems.at[1, slot_]))

    def start_pages(seq, slot_, pg0, n):
        for i in range(n):
            for cp in page_copies(pt_ref[seq * npg + pg0 + i], slot_, pg0 + i):
                cp.start()

    @pl.when(b == 0)
    def _():
        def body(pg, c):
            start_pages(0, 0, pg, 1)
            return c
        lax.fori_loop(0, npg, body, 0)
        lhs_s[0:nk, :] = wkt_ref[...]
        latb_s[1] = jnp.zeros(latb_s.shape[1:], bf16)
        s_s[1] = jnp.zeros(s_s.shape[1:], f32)
        car_s[...] = jnp.zeros_like(car_s)

    lhs_s[nk:nk + hp, :] = qabs_ref[...]

    def wait_slot(slot_):
        pltpu.make_async_copy(lat_hbm.at[layer, pl.ds(0, npg)], lat_buf.at[slot_], sems.at[0, slot_]).wait()
        pltpu.make_async_copy(krt_hbm.at[layer, pl.ds(0, npg)], kr_buf.at[slot_], sems.at[1, slot_]).wait()

    wait_slot(slot)

    def scores(latb, krt):
        n = latb.shape[0]
        big = _dot_nt(lhs_s[...], latb)
        knt = big[0:nk]
        ssq = jnp.sum((knt * knt).reshape(n_heads, _NOPE, n), axis=1)
        rs = lax.rsqrt(ssq * (1.0 / _NOPE) + EPS)
        sr = _dot(qr_ref[...], krt)
        s8 = big[nk:nk + n_heads] * rs + sr[0:n_heads]
        return jnp.concatenate([s8, jnp.zeros((hp - n_heads, n), f32)], axis=0)

    def softmax_update(carry, parts):
        m_run, l_run, acc = carry
        m_new = m_run
        for _, s in parts:
            m_new = jnp.maximum(m_new, jnp.max(s, axis=-1, keepdims=True))
        alpha = jnp.exp2(m_run - m_new)
        l_new = alpha * l_run
        acc = alpha * acc
        for latb, s in parts:
            pm = jnp.exp2(s - jnp.concatenate([m_new] * (s.shape[-1] // LANES), axis=-1))
            l_new = l_new + jnp.sum(pm, axis=-1, keepdims=True)
            acc = acc + _dot(pm.astype(bf16), latb)
        return m_new, l_new, acc

    nxt = jnp.minimum(b + 1, nseq - 1)
    nsub = npg // ppsub

    def stage(j, st, with_new_token=False):
        start_pages(nxt, 1 - slot, j * ppsub, ppsub)
        p0 = pl.multiple_of(j * ppsub, ppsub)
        lat = [lat_buf[slot, pl.ds(p0, ppsub)].reshape(tk, kl).astype(bf16)]
        krt = [kr_buf[slot, p0 + i].astype(bf16) for i in range(ppsub)]
        if with_new_token:
            lat.append(jnp.broadcast_to(latn_ref[...], (page, kl)).astype(bf16))
            krt.append(jnp.broadcast_to(krn_ref[...], (krn_ref.shape[0], page)).astype(bf16))
        latb = jnp.concatenate(lat, axis=0)
        s = scores(latb, jnp.concatenate(krt, axis=-1))
        n = latb.shape[0]
        if with_new_token:
            s = jnp.where(lax.broadcasted_iota(jnp.int32, s.shape, 1) <= tk, s, -1e30)
        latb_s[st, 0:n, :] = latb
        s_s[st, :, 0:n] = s

    def stashed(st, n=tk):
        return [(latb_s[st, 0:n, :], s_s[st, :, 0:n])]

    stage(0, 0)
    _, l_prev, acc_prev = softmax_update((car_s[0], car_s[1], car_s[2]), stashed(1, tk + page))
    o_ref[...] = acc_prev / l_prev

    def pair(k, carry):
        stage(2 * k + 1, 1)
        carry = softmax_update(carry, stashed(0))
        stage(2 * k + 2, 0)
        return softmax_update(carry, stashed(1))

    carry = (jnp.full((hp, LANES), -1e30, f32), jnp.zeros((hp, LANES), f32), jnp.zeros((hp, kl), f32))
    carry = lax.fori_loop(0, (nsub - 2) // 2, pair, carry)
    stage(nsub - 1, 1, with_new_token=True)
    carry = softmax_update(carry, stashed(0))
    for i in range(3):
        car_s[i] = carry[i]

    @pl.when(b == nseq)
    def _():
        wait_slot(1 - slot)


def _sattn_call(page_table, cache_lat, cache_krt, layer, wkt, qabs, qr, latn, krn, n_heads, ppsub):
    nb, npg = page_table.shape
    _, _, page, kl = cache_lat.shape
    rd = cache_krt.shape[2]
    hp = qabs.shape[1]
    nk = wkt.shape[0]
    assert npg % (2 * ppsub) == 0 and kl == LANES
    cur = lambda bi, pt: (jnp.minimum(bi, nb - 1), 0, 0)
    grid_spec = pltpu.PrefetchScalarGridSpec(
        num_scalar_prefetch=1,
        grid=(nb + 1,),
        in_specs=[pl.BlockSpec(memory_space=pl.ANY),
                  pl.BlockSpec(memory_space=pl.ANY),
                  pl.BlockSpec((nk, kl), lambda bi, pt: (0, 0)),
                  pl.BlockSpec((None, hp, kl), cur),
                  pl.BlockSpec((None, hp, rd), cur),
                  pl.BlockSpec((None, 1, kl), cur),
                  pl.BlockSpec((None, rd, 1), cur)],
        out_specs=pl.BlockSpec((None, hp, kl), lambda bi, pt: (jnp.maximum(bi - 1, 0), 0, 0)),
        scratch_shapes=[pltpu.VMEM((2, npg, page, kl), f32), pltpu.VMEM((2, npg, rd, page), f32),
                        pltpu.VMEM((nk + hp, kl), bf16), pltpu.VMEM((2, (ppsub + 1) * page, kl), bf16),
                        pltpu.VMEM((2, hp, (ppsub + 1) * page), f32), pltpu.VMEM((3, hp, LANES), f32),
                        pltpu.SemaphoreType.DMA((2, 2))])
    return pl.pallas_call(
        functools.partial(_sattn_kernel, layer=layer, n_heads=n_heads, ppsub=ppsub),
        grid_spec=grid_spec,
        out_shape=jax.ShapeDtypeStruct((nb, hp, kl), f32),
        compiler_params=pltpu.CompilerParams(dimension_semantics=("arbitrary",), vmem_limit_bytes=VMEM_LIMIT),
        name="sample_attn",
    )(page_table.reshape(-1), cache_lat, cache_krt, wkt, qabs, qr, latn, krn)


def _spost_kernel(x_ref, lru_ref, olat_ref, g1_ref, sh2_ref, gs2_ref, g2_ref, mog_ref, wv_ref, wo_ref,
                  wup_ref, fcw_ref, fcb_ref, wdn_ref, fbuf_ref, y_o, up_o, *, n_heads):
    nc = fcw_ref.shape[0] // 2
    ck = fcw_ref.shape[2]
    kconv = fcw_ref.shape[1]
    heads = [_dot(olat_ref[hd].astype(bf16), wv_ref[:, hd * LANES:(hd + 1) * LANES]) for hd in range(n_heads)]
    attn = jnp.concatenate([heads[2 * pp] + heads[2 * pp + 1] for pp in range(n_heads // 2)], axis=-1)
    attn = _rms(attn, mog_ref[...]).astype(bf16)
    x1, h2 = _mix_and_norm2(x_ref[...], lru_ref[...], attn, wo_ref, g1_ref[...], gs2_ref[...], sh2_ref[...])

    def conv(up, c):
        out = fcb_ref[c]
        cw = fcw_ref[c]
        for j in range(kconv - 1):
            out = out + fbuf_ref[j, :, c * ck:(c + 1) * ck] * cw[j:j + 1, :]
        up_o[:, c * ck:(c + 1) * ck] = up
        return out + up * cw[kconv - 1:kconv, :]

    acc = jnp.zeros(x1.shape, f32)
    for c in range(nc):
        val = conv(_dot(h2, _up_cols(wup_ref, c, ck)), c)
        gt = conv(_dot(h2, _up_cols(wup_ref, nc + c, ck)), nc + c)
        acc = acc + _dot((jax.nn.gelu(gt) * val).astype(bf16), wdn_ref[c * ck:(c + 1) * ck, :])
    y_o[...] = x1 + g2_ref[...] * acc


def _spost_call(x, lru, olat, g1, sh2, gs2, g2, fbuf, p, n_heads):
    nb, d = x.shape
    return pl.pallas_call(
        functools.partial(_spost_kernel, n_heads=n_heads),
        out_shape=[jax.ShapeDtypeStruct((nb, d), f32), jax.ShapeDtypeStruct((nb, p["wup"].shape[1]), f32)],
        compiler_params=pltpu.CompilerParams(vmem_limit_bytes=VMEM_LIMIT),
        name="sample_ffn",
    )(x, lru, olat, g1, sh2, gs2, g2, p["mog"], p["wv"], p["wo"], p["wup"], p["fcw"], p["fcb"], p["wdn"], fbuf)


def _take_cols(wmat, idx):
    padded = jnp.concatenate([wmat, jnp.zeros(wmat.shape[:-1] + (1,), wmat.dtype)], axis=-1)
    return jnp.take(padded, jnp.asarray(np.where(idx < 0, wmat.shape[-1], idx)), axis=-1)


def _block_diag_groups(wh):
    nh, hd, _ = wh.shape
    per = MXU_DIM // hd
    groups = []
    for g in range(nh // per):
        blk = jnp.zeros((MXU_DIM, MXU_DIM), wh.dtype)
        for j in range(per):
            blk = lax.dynamic_update_slice(blk, wh[g * per + j], (j * hd, j * hd))
        groups.append(blk)
    return jnp.stack(groups).astype(bf16)


def _prep_layer(l, n_heads, scale, w_in, lru_conv_w, lru_conv_b, lru_w_a, lru_b_a, lru_w_x, lru_b_x, lru_lambda,
                norm1_g, q_lora_norm_g, w_uq, q_nope_norm_g, q_rope_norm_g, kv_lora_norm_g, k_rope_norm_g, w_ukv,
                k_nope_norm_g, lru_out_norm_g, mla_out_norm_g, w_o, norm2_g, w_up, ffn_conv_w, ffn_conv_b, w_down):
    src = _h128_src()
    w = lru_lambda.shape[-1]
    ql = q_lora_norm_g.shape[-1]
    kl = kv_lora_norm_g.shape[-1]
    qk = _NOPE + _ROPE
    vd = w_ukv.shape[-1] // n_heads - _NOPE
    row = lambda v: v.reshape(1, -1).astype(f32)

    kr_src = np.where(src >= _NOPE, src - _NOPE, -1)
    win = w_in[l]
    base = 2 * w + ql + kl
    win_ext = jnp.concatenate([win[:, :base], _take_cols(win[:, base:], kr_src)], axis=-1).astype(bf16)

    q_idx = np.concatenate([np.where(src >= 0, src + h * qk, -1) for h in range(n_heads)])
    wuq = _take_cols(w_uq[l], q_idx).astype(bf16)
    k_src = np.where((src >= 0) & (src < _NOPE), src, -1)
    k_idx = np.concatenate([np.where(k_src >= 0, k_src + h * (_NOPE + vd), -1) for h in range(n_heads)])
    wk = _take_cols(w_ukv[l], k_idx).astype(bf16)
    v_idx = []
    vone = np.zeros((1, n_heads * LANES), np.float32)
    for h in range(n_heads):
        slab = np.full((LANES,), -1, np.int64)
        off = (h % 2) * vd
        slab[off:off + vd] = h * (_NOPE + vd) + _NOPE + np.arange(vd)
        v_idx.append(slab)
        vone[0, h * LANES + (vd - off)] = 1.0
    wv = _take_cols(w_ukv[l], np.concatenate(v_idx)).astype(bf16)
    wkt_idx = np.concatenate([h * (_NOPE + vd) + np.arange(_NOPE) for h in range(n_heads)])
    wkt = jnp.take(w_ukv[l], jnp.asarray(wkt_idx), axis=-1).T.astype(bf16)

    nope_tab = lambda g: _take_cols(g.reshape(1, -1), k_src)
    rope_tab = lambda g: _take_cols(g.reshape(1, -1), kr_src)
    is_n = (k_src >= 0).astype(np.float32)
    is_r = (kr_src >= 0).astype(np.float32)
    msel = jnp.asarray(np.outer(is_n, is_n) / _NOPE + np.outer(is_r, is_r) / _ROPE).astype(bf16)
    gk_full = nope_tab(k_nope_norm_g[l]) + jnp.asarray(1.0 - is_n).reshape(1, LANES)
    gq = (nope_tab(q_nope_norm_g[l]) + rope_tab(q_rope_norm_g[l])) * scale * gk_full

    dff = w_down.shape[1]
    ck = MXU_DIM
    nc = dff // ck
    fcw = ffn_conv_w[l].reshape(-1, 2 * nc, ck).transpose(1, 0, 2).astype(f32)
    fcb = ffn_conv_b[l].reshape(2 * nc, 1, ck).astype(f32)
    return dict(
        n1g=row(norm1_g[l]), win=win_ext, cw=lru_conv_w[l].astype(f32), cb=row(lru_conv_b[l]),
        wa=_block_diag_groups(lru_w_a[l]), ba=row(lru_b_a[l]), wx=_block_diag_groups(lru_w_x[l]), bx=row(lru_b_x[l]),
        lam=row(lru_lambda[l]), qlg=row(q_lora_norm_g[l]), wuq=wuq, gq=gq.astype(f32), msel=msel,
        kvg=row(kv_lora_norm_g[l]), gkr=rope_tab(k_rope_norm_g[l]).astype(f32), wk=wk,
        wv=wv, vone=jnp.asarray(vone), wkt=wkt, log=row(lru_out_norm_g[l]),
        mog=row(mla_out_norm_g[l]), wo=w_o[l].astype(bf16), n2g=row(norm2_g[l]), wup=w_up[l].astype(bf16),
        fcw=fcw, fcb=fcb, wdn=w_down[l].astype(bf16))


def _rope_tables(pos):
    half = _ROPE // 2
    inv = ROPE_THETA ** (-np.arange(0, _ROPE, 2, dtype=np.float64) / _ROPE)
    ang = np.asarray(pos, np.float64)[:, None] * inv[None, :]
    cos, sin = np.cos(ang), np.sin(ang)
    n = ang.shape[0]
    cc = np.zeros((n, LANES), np.float32)
    cc[:, 0:_R1_LO] = 1.0
    cc[:, _R1_LO + half:_R1_LO + half + 16] = 1.0
    cc[:, _R1_LO:_R1_LO + half] = cos
    cc[:, _R2_LO:_R2_LO + half] = cos
    ss = np.zeros((n, LANES), np.float32)
    ss[:, _R1_LO:_R1_LO + half] = -sin
    ss[:, _R2_LO:_R2_LO + half] = sin
    return jnp.asarray(cc), jnp.asarray(ss)


def _gain_rope_tables(cc, ss, gq, gkr):
    swap = lambda g: jnp.roll(g, LANES // 2, axis=-1)
    return cc * gq, ss * swap(gq), cc * gkr, ss * swap(gkr)


def kernel(x_prompt, x_sample, cache_kv_latent, cache_k_rope, state_lru_h, state_lru_conv, state_ffn_conv,
           page_table, c_prompt, c_sample, w_ada, b_ada, norm1_g, w_in, lru_conv_w, lru_conv_b, lru_w_a, lru_b_a,
           lru_w_x, lru_b_x, lru_lambda, q_lora_norm_g, w_uq, q_nope_norm_g, q_rope_norm_g, kv_lora_norm_g,
           k_rope_norm_g, w_ukv, k_nope_norm_g, lru_out_norm_g, mla_out_norm_g, w_o, norm2_g, w_up, ffn_conv_w,
           ffn_conv_b, w_down):
    b, s, d = x_prompt.shape
    nb, ds, _ = x_sample.shape
    depth = w_in.shape[0]
    assert ds == 1 and q_nope_norm_g.shape[-1] == _NOPE and q_rope_norm_g.shape[-1] == _ROPE
    n_heads = w_uq.shape[-1] // (_NOPE + _ROPE)
    scale = float(_NOPE + _ROPE) ** -0.5 * float(np.log2(np.e))
    npg = page_table.shape[1]
    n_past = npg * cache_kv_latent.shape[2]
    ts = 64 if s % 64 == 0 else 32
    tq = min(2048, s)
    tk = min(256, s)
    tm = min(512, s)
    ppsub = 16 if npg % 32 == 0 else 1
    hp = 16

    cache_krt = jnp.swapaxes(cache_k_rope, 2, 3)
    cc_p, ss_p = _rope_tables(np.arange(s))
    cc_s, ss_s = _rope_tables(n_past + np.arange(1))

    y_p = x_prompt
    y_s = x_sample.reshape(nb, d)
    c_all = jnp.concatenate([c_prompt, c_sample], axis=0)
    outs_p = [[] for _ in range(5)]
    outs_s = [[] for _ in range(5)]
    for l in range(depth):
        p = _prep_layer(l, n_heads, scale, w_in, lru_conv_w, lru_conv_b, lru_w_a, lru_b_a, lru_w_x, lru_b_x,
                        lru_lambda, norm1_g, q_lora_norm_g, w_uq, q_nope_norm_g, q_rope_norm_g, kv_lora_norm_g,
                        k_rope_norm_g, w_ukv, k_nope_norm_g, lru_out_norm_g, mla_out_norm_g, w_o, norm2_g, w_up,
                        ffn_conv_w, ffn_conv_b, w_down)
        mod = _mod_call(c_all, w_ada[l], b_ada[l])
        sh1, sc1, g1, sh2, sc2, g2 = jnp.split(mod, 6, axis=-1)
        mods = [sh1, p["n1g"] * (1.0 + sc1), g1, sh2, p["n2g"] * (1.0 + sc2), g2]
        mp = [m_[:b, None, :] for m_ in mods]
        ms = [m_[b:] for m_ in mods]

        lru_p, q_p, k_p, v_p, kvlat_p, krope_p, hlast_p, xtail_p = _inproj_call(
            y_p, mp[0], mp[1], _gain_rope_tables(cc_p, ss_p, p["gq"], p["gkr"]), p, n_heads, ts)
        attn_p = _attn_call(q_p, k_p, v_p, p["mog"], n_heads, tq, tk)
        y_p, ftail_p = _ffn_call(y_p, lru_p, attn_p, mp[2], mp[3], mp[4], mp[5], p, tm)
        kc = lru_conv_w.shape[1]
        fk = ffn_conv_w.shape[1]
        for j, o in enumerate((kvlat_p, krope_p, hlast_p, xtail_p[:, 8 - (kc - 1):], ftail_p[:, 8 - (fk - 1):])):
            outs_p[j].append(o)

        cbuf = jnp.swapaxes(state_lru_conv[l], 0, 1)
        fbuf = jnp.swapaxes(state_ffn_conv[l], 0, 1)
        lru_s, xlru_s, hnew_s, qabs, qr, kvn_s, kr_s = _spre_call(
            y_s, ms[0], ms[1], _gain_rope_tables(cc_s, ss_s, p["gq"], p["gkr"]), cbuf, state_lru_h[l], p, n_heads,
            n_past == 0)
        pad_heads = lambda t: jnp.pad(jnp.swapaxes(t, 0, 1), ((0, 0), (0, hp - n_heads), (0, 0)))
        olat = _sattn_call(page_table, cache_kv_latent, cache_krt, l, p["wkt"], pad_heads(qabs), pad_heads(qr),
                           kvn_s[:, None, :], kr_s[:, :, None], n_heads, ppsub)
        olat = jnp.swapaxes(olat[:, :n_heads], 0, 1)
        y_s, up_s = _spost_call(y_s, lru_s, olat, ms[2], ms[3], ms[4], ms[5], fbuf, p, n_heads)
        lru_conv_new = jnp.concatenate([state_lru_conv[l][:, 1:], xlru_s[:, None, :]], axis=1)
        ffn_conv_new = jnp.concatenate([state_ffn_conv[l][:, 1:], up_s[:, None, :]], axis=1)
        for j, o in enumerate((kvn_s[:, None, :], kr_s[:, None, :], hnew_s, lru_conv_new, ffn_conv_new)):
            outs_s[j].append(o)

    return (y_p, y_s.reshape(nb, 1, d), *[jnp.stack(o) for o in outs_p], *[jnp.stack(o) for o in outs_s])
```

```python
import functools

import numpy as np
import jax
import jax.numpy as jnp
from jax import lax
from jax.experimental import pallas as pl
from jax.experimental.pallas import tpu as pltpu

f32 = jnp.float32
bf16 = jnp.bfloat16

EPS = 1e-6
LRU_C = 8.0
ROPE_THETA = 10000.0
LANES = 128
MXU_DIM = 256
VMEM_LIMIT = 56 * 1024 * 1024

_NOPE, _ROPE = 64, 32
_R1_LO, _R2_LO = 48, 112


def _h128_src():
    src = np.full((LANES,), -1, np.int32)
    src[0:48] = np.arange(0, 48)
    src[48:64] = _NOPE + np.arange(0, 16)
    src[64:80] = np.arange(48, 64)
    src[112:128] = _NOPE + 16 + np.arange(0, 16)
    return src


def _dot(a, b):
    return jnp.dot(a, b, preferred_element_type=f32)


def _dot_nt(a, b):
    return lax.dot_general(a, b, (((1,), (1,)), ((), ())), preferred_element_type=f32)


def _rms(x, g):
    ms = jnp.mean(x * x, axis=-1, keepdims=True)
    return x * lax.rsqrt(ms + EPS) * g


_GELU_C0 = float(np.sqrt(2.0 / np.pi))
_GELU_C1 = _GELU_C0 * 0.044715


def _gelu_x2(x):
    return x * (1.0 + jnp.tanh(x * (_GELU_C0 + _GELU_C1 * (x * x))))


def _neg_expm1_2x(y):
    t = jnp.tanh(y)
    return -2.0 * t / (1.0 - t)


def _lru_gates(x_conv, wa_ref, ba, wx_ref, bx, lam):
    xb = x_conv.astype(bf16)
    ng = wa_ref.shape[0]
    ra = jnp.concatenate([_dot(xb[:, g * MXU_DIM:(g + 1) * MXU_DIM], wa_ref[g]) for g in range(ng)], axis=-1) + ba
    ia = jnp.concatenate([_dot(xb[:, g * MXU_DIM:(g + 1) * MXU_DIM], wx_ref[g]) for g in range(ng)], axis=-1) + bx
    r = jax.nn.sigmoid(ra)
    ig = jax.nn.sigmoid(ia)
    log_a = r * ((-LRU_C) * jax.nn.softplus(-lam))
    a = jnp.exp(log_a)
    mult = jnp.sqrt(_neg_expm1_2x(log_a))
    return a, mult, ig * x_conv


def _group_ms(x, msel):
    return _dot((x * x).astype(bf16), msel)


def _norm_rope(x, msel, ca, sb, to3d):
    inv = lax.rsqrt(_group_ms(x, msel) + EPS)
    return to3d(inv) * (to3d(x) * ca + to3d(pltpu.roll(x, LANES // 2, 1)) * sb)


def _q_heads(q, msel, ca, sb, n_heads, to3d):
    return [_norm_rope(q[:, h * LANES:(h + 1) * LANES], msel, ca, sb, to3d) for h in range(n_heads)]


def _mod_kernel(c_ref, w_ref, b_ref, o_ref):
    c = c_ref[...]
    sc = (c * jax.nn.sigmoid(c)).astype(bf16)
    o_ref[...] = _dot(sc, w_ref[...].astype(bf16)) + b_ref[...]


def _mod_call(c_all, w_ada, b_ada):
    m, d = c_all.shape
    n = w_ada.shape[1]
    tn = 1536 if n % 1536 == 0 else n
    return pl.pallas_call(
        _mod_kernel,
        grid=(n // tn,),
        in_specs=[pl.BlockSpec((m, d), lambda j: (0, 0)),
                  pl.BlockSpec((d, tn), lambda j: (0, j)),
                  pl.BlockSpec((1, tn), lambda j: (0, j))],
        out_specs=pl.BlockSpec((m, tn), lambda j: (0, j)),
        out_shape=jax.ShapeDtypeStruct((m, n), f32),
        compiler_params=pltpu.CompilerParams(dimension_semantics=("arbitrary",), vmem_limit_bytes=VMEM_LIMIT),
        name="adaln_mod",
    )(c_all, w_ada, b_ada.reshape(1, n))


def _inproj_kernel(x_ref, sh_ref, sc_ref, qca_ref, qsb_ref, kca_ref, ksb_ref, win_ref, cw_ref, cb_ref,
                   wa_ref, ba_ref, wx_ref, bx_ref, lam_ref, qlg_ref, wuq_ref, msel_ref,
                   kvg_ref, wk_ref, wv_ref, vone_ref, log_ref,
                   lru_o, q_o, k_o, v_o, kvlat_o, krope_o, hlast_o, xtail_o,
                   xtail_s, a_s, u_s, hs_s, h_s, *, n_heads):
    i = pl.program_id(0)
    nb, ts, d = x_ref.shape
    m = nb * ts
    w = lam_ref.shape[-1]
    ql = qlg_ref.shape[-1]
    kl = kvg_ref.shape[-1]
    kconv = cw_ref.shape[0]
    nlc = w // LANES

    @pl.when(i == 0)
    def _():
        xtail_s[...] = jnp.zeros_like(xtail_s)
        h_s[...] = jnp.zeros_like(h_s)

    def to3d(v):
        return v.reshape(nb, ts, v.shape[-1])

    x = x_ref[...]
    h = _rms(x, sc_ref[...]) + sh_ref[...]
    z = _dot(h.reshape(m, d).astype(bf16), win_ref[...])
    x_lru = z[:, 0:w]
    g_lru = z[:, w:2 * w]
    q_lat = z[:, 2 * w:2 * w + ql]
    kv_lat = z[:, 2 * w + ql:2 * w + ql + kl]
    kr_pre = z[:, 2 * w + ql + kl:2 * w + ql + kl + LANES]

    t_idx = lax.broadcasted_iota(jnp.int32, (m, 1), 0) & (ts - 1)
    r8 = lax.broadcasted_iota(jnp.int32, (1, 8, 1), 1)
    tail2d = xtail_s[...].reshape(nb * 8, w)
    x_conv = to3d(jnp.broadcast_to(cb_ref[...], (m, w)))
    for j in range(kconv - 1):
        k = kconv - 1 - j
        rolled = to3d(pltpu.roll(x_lru, k, 0))
        prev = pltpu.roll(tail2d, nb * 8 + k - 8, 0).reshape(nb, 8, w)
        sh = jnp.concatenate([jnp.where(r8 < k, prev, rolled[:, 0:8]), rolled[:, 8:]], axis=1)
        x_conv = x_conv + sh * cw_ref[j:j + 1, :]
    x_conv = (x_conv + to3d(x_lru) * cw_ref[kconv - 1:kconv, :]).reshape(m, w)
    xtail_s[...] = to3d(x_lru)[:, ts - 8:, :]
    xtail_o[...] = to3d(x_lru)[:, ts - 8:, :]

    a, mult, ux = _lru_gates(x_conv, wa_ref, ba_ref[...], wx_ref, bx_ref[...], lam_ref[...])
    mult = jnp.where(jnp.logical_and(t_idx == 0, i == 0), 1.0, mult)
    u = mult * ux
    pitch = a_s.shape[1] // nb
    for j in range(nlc):
        for bi in range(nb):
            a_s[j, bi * pitch:bi * pitch + ts, :] = a[bi * ts:(bi + 1) * ts, j * LANES:(j + 1) * LANES]
            u_s[j, bi * pitch:bi * pitch + ts, :] = u[bi * ts:(bi + 1) * ts, j * LANES:(j + 1) * LANES]

    def scan_step(t, hc):
        out = []
        for j in range(nlc):
            hj = a_s[j, pl.ds(t, nb, stride=pitch), :] * hc[j] + u_s[j, pl.ds(t, nb, stride=pitch), :]
            hs_s[j, pl.ds(t, nb, stride=pitch), :] = hj
            out.append(hj)
        return tuple(out)

    hc = lax.fori_loop(0, ts, scan_step, tuple(h_s[j] for j in range(nlc)), unroll=8)
    for j in range(nlc):
        h_s[j] = hc[j]
    hlast_o[...] = jnp.concatenate(list(hc), axis=-1)
    hs = jnp.concatenate(
        [jnp.concatenate([hs_s[j, bi * pitch:bi * pitch + ts, :] for bi in range(nb)], axis=0) for j in range(nlc)],
        axis=-1)
    lru_out = hs * jax.nn.gelu(g_lru)
    lru_o[...] = to3d(_rms(lru_out, log_ref[...]).astype(bf16))

    msel = msel_ref[...]
    qn = _rms(q_lat, qlg_ref[...]).astype(bf16)
    q = _dot(qn, wuq_ref[...])
    for hd, o in enumerate(_q_heads(q, msel, qca_ref[...][None], qsb_ref[...][None], n_heads, to3d)):
        q_o[:, :, hd * LANES:(hd + 1) * LANES] = o.astype(bf16)

    kvn = _rms(kv_lat, kvg_ref[...])
    kvlat_o[...] = to3d(kvn)
    kr = _norm_rope(kr_pre, msel, kca_ref[...][None], ksb_ref[...][None], to3d)
    krope_o[...] = jnp.concatenate([kr[:, :, _R1_LO:_R1_LO + 16], kr[:, :, _R2_LO:_R2_LO + 16]], axis=-1)
    kvb = kvn.astype(bf16)
    kk = _dot(kvb, wk_ref[...])
    v_o[...] = to3d((_dot(kvb, wv_ref[...]) + vone_ref[...]).astype(bf16))
    for hd in range(n_heads):
        kh = kk[:, hd * LANES:(hd + 1) * LANES]
        khn = kh * lax.rsqrt(_group_ms(kh, msel) + EPS)
        k_o[:, :, hd * LANES:(hd + 1) * LANES] = (to3d(khn) + kr).astype(bf16)


def _const_spec(shape):
    nd = len(shape)
    return pl.BlockSpec(shape, lambda *_: (0,) * nd, pipeline_mode=pl.Buffered(1))


def _inproj_call(x, sh1, sc1, rope_tabs, p, n_heads, ts):
    b, s, d = x.shape
    w = p["lam"].shape[-1]
    kl = p["kvg"].shape[-1]
    hw = n_heads * LANES
    m = b * ts
    consts = [p["win"], p["cw"], p["cb"], p["wa"], p["ba"], p["wx"], p["bx"], p["lam"], p["qlg"],
              p["wuq"], p["msel"], p["kvg"], p["wk"], p["wv"], p["vone"], p["log"]]
    in_specs = ([pl.BlockSpec((b, ts, d), lambda i: (0, i, 0)), _const_spec(sh1.shape), _const_spec(sc1.shape)]
                + [pl.BlockSpec((ts, LANES), lambda i: (i, 0))] * len(rope_tabs)
                + [_const_spec(c.shape) for c in consts])

    def tile(n, dt):
        return pl.BlockSpec((b, ts, n), lambda i: (0, i, 0)), jax.ShapeDtypeStruct((b, s, n), dt)

    outs = [tile(w, bf16), tile(hw, bf16), tile(hw, bf16), tile(hw, bf16), tile(kl, f32), tile(_ROPE, f32),
            (pl.BlockSpec((b, w), lambda i: (0, 0)), jax.ShapeDtypeStruct((b, w), f32)),
            (pl.BlockSpec((b, 8, w), lambda i: (0, 0, 0)), jax.ShapeDtypeStruct((b, 8, w), f32))]
    nlc = w // LANES
    return pl.pallas_call(
        functools.partial(_inproj_kernel, n_heads=n_heads),
        grid=(s // ts,),
        in_specs=in_specs,
        out_specs=[o[0] for o in outs],
        out_shape=[o[1] for o in outs],
        scratch_shapes=[pltpu.VMEM((b, 8, w), f32)] + [pltpu.VMEM((nlc, b * (ts + 8), LANES), f32)] * 3
        + [pltpu.VMEM((nlc, b, LANES), f32)],
        compiler_params=pltpu.CompilerParams(dimension_semantics=("arbitrary",), vmem_limit_bytes=VMEM_LIMIT),
        name="prompt_inproj",
    )(x, sh1, sc1, *rope_tabs, *consts)


def _attn_kernel(q_ref, k_ref, v_ref, g_ref, o_ref, m_s, acc_s, *, n_heads, tk):
    qi = pl.program_id(1)
    tq = q_ref.shape[1]
    nmask = tq // tk
    n_full = qi * nmask
    m_s[...] = jnp.full_like(m_s, -1e30)
    acc_s[...] = jnp.zeros_like(acc_s)

    def block(j, r0, masked):
        start = pl.multiple_of(j * tk, tk)
        if masked:
            vis = (lax.broadcasted_iota(jnp.int32, (tq - r0, tk), 1)
                   <= lax.broadcasted_iota(jnp.int32, (tq - r0, tk), 0))
        for hd in range(n_heads):
            hs = slice(hd * LANES, (hd + 1) * LANES)
            s = _dot_nt(q_ref[0, r0:, hs], k_ref[0, pl.ds(start, tk), hs])
            if masked:
                s = jnp.where(vis, s, -1e30)
            mx = m_s[hd, r0:, :]
            m_new = jnp.maximum(mx, jnp.max(s, axis=-1, keepdims=True))
            alpha = jnp.exp2(mx - m_new)
            pm = jnp.exp2(s - jnp.concatenate([m_new] * (tk // LANES), axis=-1))
            acc_s[hd, r0:, :] = alpha * acc_s[hd, r0:, :] + _dot(pm.astype(bf16), v_ref[0, pl.ds(start, tk), hs])
            m_s[hd, r0:, :] = m_new

    def full_block(j, c):
        block(j, 0, False)
        return c

    lax.fori_loop(0, n_full, full_block, 0)
    for jm in range(nmask):
        block(n_full + jm, jm * tk, True)
    lane = lax.broadcasted_iota(jnp.int32, (1, LANES), 1)
    half = LANES // 2
    heads = []
    for hd in range(n_heads):
        acc = acc_s[hd]
        lo = (hd % 2) * half
        one_lane = (half - lo)
        l = jnp.sum(jnp.where(lane == one_lane, acc, 0.0), axis=-1, keepdims=True)
        heads.append(jnp.where((lane >= lo) & (lane < lo + half), acc, 0.0) / l)
    o = jnp.concatenate([heads[2 * pp] + heads[2 * pp + 1] for pp in range(n_heads // 2)], axis=-1)
    o_ref[0] = _rms(o, g_ref[...]).astype(bf16)


def _attn_call(q, k, v, g, n_heads, tq, tk):
    b, s, hw = q.shape
    wout = g.shape[-1]
    return pl.pallas_call(
        functools.partial(_attn_kernel, n_heads=n_heads, tk=tk),
        grid=(b, s // tq),
        in_specs=[pl.BlockSpec((1, tq, hw), lambda bi, qi: (bi, qi, 0)),
                  pl.BlockSpec((1, s, hw), lambda bi, qi: (bi, 0, 0)),
                  pl.BlockSpec((1, s, hw), lambda bi, qi: (bi, 0, 0)),
                  pl.BlockSpec((1, wout), lambda bi, qi: (0, 0))],
        out_specs=pl.BlockSpec((1, tq, wout), lambda bi, qi: (bi, qi, 0)),
        out_shape=jax.ShapeDtypeStruct((b, s, wout), bf16),
        scratch_shapes=[pltpu.VMEM((n_heads, tq, LANES), f32), pltpu.VMEM((n_heads, tq, LANES), f32)],
        compiler_params=pltpu.CompilerParams(dimension_semantics=("arbitrary", "arbitrary"),
                                             vmem_limit_bytes=VMEM_LIMIT),
        name="prompt_attn",
    )(q, k, v, g)


def _mix_and_norm2(x, lru, attn, wo_ref, g1, gs2, sh2):
    w = lru.shape[-1]
    mixed = _dot(lru, wo_ref[pl.ds(0, w), :]) + _dot(attn, wo_ref[pl.ds(w, attn.shape[-1]), :])
    x1 = x + g1 * mixed
    h2 = (_rms(x1, gs2) + sh2).astype(bf16)
    return x1, h2


def _up_cols(wup_ref, c, ck):
    return wup_ref[:, c * ck:(c + 1) * ck]


def _ffn_kernel(x_ref, lru_ref, attn_ref, g1_ref, sh2_ref, gs2_ref, g2_ref, wo_ref, wup_ref,
                fcw_ref, fcb_ref, wdn_ref, y_o, tail_o, tail_s, act_s):
    si = pl.program_id(1)
    tm = x_ref.shape[1]
    nc = fcw_ref.shape[0] // 2
    ck = fcw_ref.shape[2]
    kconv = fcw_ref.shape[1]

    @pl.when(si == 0)
    def _():
        tail_s[...] = jnp.zeros_like(tail_s)

    x1, h2 = _mix_and_norm2(x_ref[0], lru_ref[0], attn_ref[0], wo_ref, g1_ref[0], gs2_ref[0], sh2_ref[0])
    r8 = lax.broadcasted_iota(jnp.int32, (8, 1), 0)

    def conv(up, c):
        tail = tail_s[c]
        first = up[0:8]
        out = fcb_ref[c]
        cw = fcw_ref[c]
        for j in range(kconv - 1):
            k = kconv - 1 - j
            head = jnp.where(r8 < k, pltpu.roll(tail, k, 0), pltpu.roll(first, k, 0))
            sh = jnp.concatenate([head, pltpu.roll(up, k, 0)[8:]], axis=0)
            out = out + sh * cw[j:j + 1, :]
        out = out + up * cw[kconv - 1:kconv, :]
        tail_s[c] = up[tm - 8:]
        tail_o[0, :, c * ck:(c + 1) * ck] = up[tm - 8:]
        return out

    for c in range(nc):
        val = conv(_dot(h2, _up_cols(wup_ref, c, ck)), c)
        gt = conv(_dot(h2, _up_cols(wup_ref, nc + c, ck)), nc + c)
        act_s[:, c * ck:(c + 1) * ck] = (_gelu_x2(gt) * val).astype(bf16)
    acc = _dot(act_s[...], wdn_ref[...])
    y_o[0] = x1 + g2_ref[0] * acc


def _ffn_call(x, lru, attn, g1, sh2, gs2, g2, p, tm):
    b, s, d = x.shape
    w = lru.shape[-1]
    nc2, _, ck = p["fcw"].shape
    nc = nc2 // 2
    row = lambda n: pl.BlockSpec((1, tm, n), lambda bi, si: (bi, si, 0))
    modspec = pl.BlockSpec((1, 1, d), lambda bi, si: (bi, 0, 0))
    consts = [p["wo"], p["wup"], p["fcw"], p["fcb"], p["wdn"]]
    return pl.pallas_call(
        _ffn_kernel,
        grid=(b, s // tm),
        in_specs=[row(d), row(w), row(attn.shape[-1]), modspec, modspec, modspec, modspec]
        + [_const_spec(c.shape) for c in consts],
        out_specs=[row(d), pl.BlockSpec((1, 8, 2 * nc * ck), lambda bi, si: (bi, 0, 0))],
        out_shape=[jax.ShapeDtypeStruct((b, s, d), f32), jax.ShapeDtypeStruct((b, 8, 2 * nc * ck), f32)],
        scratch_shapes=[pltpu.VMEM((2 * nc, 8, ck), f32), pltpu.VMEM((tm, nc * ck), bf16)],
        compiler_params=pltpu.CompilerParams(dimension_semantics=("arbitrary", "arbitrary"),
                                             vmem_limit_bytes=VMEM_LIMIT),
        name="prompt_ffn",
    )(x, lru, attn, g1, sh2, gs2, g2, *consts)


def _spre_kernel(x_ref, sh_ref, sc_ref, qca_ref, qsb_ref, kca_ref, ksb_ref, win_ref, cbuf_ref, cw_ref,
                 cb_ref, wa_ref, ba_ref, wx_ref, bx_ref, lam_ref, h0_ref, qlg_ref, wuq_ref, msel_ref,
                 kvg_ref, wk_ref, log_ref,
                 lru_o, xlru_o, hnew_o, qabs_o, qr_o, kvn_o, kr_o, *, n_heads, first_pos):
    w = lam_ref.shape[-1]
    ql = qlg_ref.shape[-1]
    kl = kvg_ref.shape[-1]
    kconv = cw_ref.shape[0]
    ident = lambda v: v

    h = _rms(x_ref[...], sc_ref[...]) + sh_ref[...]
    z = _dot(h.astype(bf16), win_ref[...])
    x_lru = z[:, 0:w]
    g_lru = z[:, w:2 * w]
    q_lat = z[:, 2 * w:2 * w + ql]
    kv_lat = z[:, 2 * w + ql:2 * w + ql + kl]
    kr_pre = z[:, 2 * w + ql + kl:2 * w + ql + kl + LANES]

    x_conv = cb_ref[...]
    for j in range(kconv - 1):
        x_conv = x_conv + cbuf_ref[j] * cw_ref[j:j + 1, :]
    x_conv = x_conv + x_lru * cw_ref[kconv - 1:kconv, :]
    xlru_o[...] = x_lru
    a, mult, ux = _lru_gates(x_conv, wa_ref, ba_ref[...], wx_ref, bx_ref[...], lam_ref[...])
    if first_pos:
        mult = jnp.ones_like(mult)
    hn = a * h0_ref[...] + mult * ux
    hnew_o[...] = hn
    lru_o[...] = _rms(hn * jax.nn.gelu(g_lru), log_ref[...]).astype(bf16)

    msel = msel_ref[...]
    qn = _rms(q_lat, qlg_ref[...]).astype(bf16)
    q = _dot(qn, wuq_ref[...])
    for hd, o in enumerate(_q_heads(q, msel, qca_ref[...], qsb_ref[...], n_heads, ident)):
        wkh = wk_ref[:, hd * LANES:(hd + 1) * LANES]
        qabs_o[hd] = _dot_nt(o.astype(bf16), wkh).astype(bf16)
        qr_o[hd] = jnp.concatenate([o[:, _R1_LO:_R1_LO + 16], o[:, _R2_LO:_R2_LO + 16]], axis=-1).astype(bf16)

    kvn_o[...] = _rms(kv_lat, kvg_ref[...])
    kr = _norm_rope(kr_pre, msel, kca_ref[...], ksb_ref[...], ident)
    kr_o[...] = jnp.concatenate([kr[:, _R1_LO:_R1_LO + 16], kr[:, _R2_LO:_R2_LO + 16]], axis=-1)


def _spre_call(x, sh1, sc1, rope_tabs, cbuf, h0, p, n_heads, first_pos):
    nb, d = x.shape
    w = p["lam"].shape[-1]
    kl = p["kvg"].shape[-1]
    out_shape = [jax.ShapeDtypeStruct((nb, w), bf16), jax.ShapeDtypeStruct((nb, w), f32),
                 jax.ShapeDtypeStruct((nb, w), f32), jax.ShapeDtypeStruct((n_heads, nb, kl), bf16),
                 jax.ShapeDtypeStruct((n_heads, nb, _ROPE), bf16), jax.ShapeDtypeStruct((nb, kl), f32),
                 jax.ShapeDtypeStruct((nb, _ROPE), f32)]
    return pl.pallas_call(
        functools.partial(_spre_kernel, n_heads=n_heads, first_pos=first_pos),
        out_shape=out_shape,
        compiler_params=pltpu.CompilerParams(vmem_limit_bytes=VMEM_LIMIT),
        name="sample_inproj",
    )(x, sh1, sc1, *rope_tabs, p["win"], cbuf, p["cw"], p["cb"], p["wa"], p["ba"], p["wx"], p["bx"],
      p["lam"], h0, p["qlg"], p["wuq"], p["msel"], p["kvg"], p["wk"], p["log"])


def _sattn_kernel(pt_ref, lat_hbm, krt_hbm, wkt_ref, qabs_ref, qr_ref, latn_ref, krn_ref, o_ref,
                  lat_buf, kr_buf, lhs_s, latb_s, s_s, car_s, sems, *, layer, n_heads, ppsub):
    b = pl.program_id(0)
    nseq = pl.num_programs(0) - 1
    _, npg, page, kl = lat_buf.shape
    nk = wkt_ref.shape[0]
    hp = qabs_ref.shape[0]
    tk = ppsub * page
    slot = lax.rem(b, 2)

    def page_copies(src_page, slot_, pg):
        return (pltpu.make_async_copy(lat_hbm.at[layer, src_page], lat_buf.at[slot_, pg], sems.at[0, slot_]),
                pltpu.make_async_copy(krt_hbm.at[layer, src_page], kr_buf.at[slot_, pg], sems.at[1, slot_]))

    def start_pages(seq, slot_, pg0, n):
        for i in range(n):
            for cp in page_copies(pt_ref[seq * npg + pg0 + i], slot_, pg0 + i):
                cp.start()

    @pl.when(b == 0)
    def _():
        def body(pg, c):
            start_pages(0, 0, pg, 1)
            return c
        lax.fori_loop(0, npg, body, 0)
        lhs_s[0:nk, :] = wkt_ref[...]
        latb_s[1] = jnp.zeros(latb_s.shape[1:], bf16)
        s_s[1] = jnp.zeros(s_s.shape[1:], f32)
        car_s[...] = jnp.zeros_like(car_s)

    lhs_s[nk:nk + hp, :] = qabs_ref[...]

    def wait_slot(slot_):
        pltpu.make_async_copy(lat_hbm.at[layer, pl.ds(0, npg)], lat_buf.at[slot_], sems.at[0, slot_]).wait()
        pltpu.make_async_copy(krt_hbm.at[layer, pl.ds(0, npg)], kr_buf.at[slot_], sems.at[1, slot_]).wait()

    wait_slot(slot)

    def scores(latb, krt):
        n = latb.shape[0]
        big = _dot_nt(lhs_s[...], latb)
        knt = big[0:nk]
        ssq = jnp.sum((knt * knt).reshape(n_heads, _NOPE, n), axis=1)
        rs = lax.rsqrt(ssq * (1.0 / _NOPE) + EPS)
        sr = _dot(qr_ref[...], krt)
        s8 = big[nk:nk + n_heads] * rs + sr[0:n_heads]
        return jnp.concatenate([s8, jnp.zeros((hp - n_heads, n), f32)], axis=0)

    def softmax_update(carry, parts):
        m_run, l_run, acc = carry
        m_new = m_run
        for _, s in parts:
            m_new = jnp.maximum(m_new, jnp.max(s, axis=-1, keepdims=True))
        alpha = jnp.exp2(m_run - m_new)
        l_new = alpha * l_run
        acc = alpha * acc
        for latb, s in parts:
            pm = jnp.exp2(s - jnp.concatenate([m_new] * (s.shape[-1] // LANES), axis=-1))
            l_new = l_new + jnp.sum(pm, axis=-1, keepdims=True)
            acc = acc + _dot(pm.astype(bf16), latb)
        return m_new, l_new, acc

    nxt = jnp.minimum(b + 1, nseq - 1)
    nsub = npg // ppsub

    def stage(j, st, with_new_token=False):
        start_pages(nxt, 1 - slot, j * ppsub, ppsub)
        p0 = pl.multiple_of(j * ppsub, ppsub)
        lat = [lat_buf[slot, pl.ds(p0, ppsub)].reshape(tk, kl).astype(bf16)]
        krt = [kr_buf[slot, p0 + i].astype(bf16) for i in range(ppsub)]
        if with_new_token:
            lat.append(jnp.broadcast_to(latn_ref[...], (page, kl)).astype(bf16))
            krt.append(jnp.broadcast_to(krn_ref[...], (krn_ref.shape[0], page)).astype(bf16))
        latb = jnp.concatenate(lat, axis=0)
        s = scores(latb, jnp.concatenate(krt, axis=-1))
        n = latb.shape[0]
        if with_new_token:
            s = jnp.where(lax.broadcasted_iota(jnp.int32, s.shape, 1) <= tk, s, -1e30)
        latb_s[st, 0:n, :] = latb
        s_s[st, :, 0:n] = s

    def stashed(st, n=tk):
        return [(latb_s[st, 0:n, :], s_s[st, :, 0:n])]

    stage(0, 0)
    _, l_prev, acc_prev = softmax_update((car_s[0], car_s[1], car_s[2]), stashed(1, tk + page))
    o_ref[...] = acc_prev / l_prev

    spi = 6 if (nsub - 2) % 6 == 0 else 2

    def body(k, carry):
        for u in range(0, spi, 2):
            stage(spi * k + u + 1, 1)
            carry = softmax_update(carry, stashed(0))
            stage(spi * k + u + 2, 0)
            carry = softmax_update(carry, stashed(1))
        return carry

    carry = (jnp.full((hp, LANES), -1e30, f32), jnp.zeros((hp, LANES), f32), jnp.zeros((hp, kl), f32))
    carry = lax.fori_loop(0, (nsub - 2) // spi, body, carry)
    stage(nsub - 1, 1, with_new_token=True)
    carry = softmax_update(carry, stashed(0))
    for i in range(3):
        car_s[i] = carry[i]

    @pl.when(b == nseq)
    def _():
        wait_slot(1 - slot)


def _sattn_call(page_table, cache_lat, cache_krt, layer, wkt, qabs, qr, latn, krn, n_heads, ppsub):
    nb, npg = page_table.shape
    _, _, page, kl = cache_lat.shape
    rd = cache_krt.shape[2]
    hp = qabs.shape[1]
    nk = wkt.shape[0]
    assert npg % (2 * ppsub) == 0 and kl == LANES
    cur = lambda bi, pt: (jnp.minimum(bi, nb - 1), 0, 0)
    grid_spec = pltpu.PrefetchScalarGridSpec(
        num_scalar_prefetch=1,
        grid=(nb + 1,),
        in_specs=[pl.BlockSpec(memory_space=pl.ANY),
                  pl.BlockSpec(memory_space=pl.ANY),
                  pl.BlockSpec((nk, kl), lambda bi, pt: (0, 0)),
                  pl.BlockSpec((None, hp, kl), cur),
                  pl.BlockSpec((None, hp, rd), cur),
                  pl.BlockSpec((None, 1, kl), cur),
                  pl.BlockSpec((None, rd, 1), cur)],
        out_specs=pl.BlockSpec((None, hp, kl), lambda bi, pt: (jnp.maximum(bi - 1, 0), 0, 0)),
        scratch_shapes=[pltpu.VMEM((2, npg, page, kl), f32), pltpu.VMEM((2, npg, rd, page), f32),
                        pltpu.VMEM((nk + hp, kl), bf16), pltpu.VMEM((2, (ppsub + 1) * page, kl), bf16),
                        pltpu.VMEM((2, hp, (ppsub + 1) * page), f32), pltpu.VMEM((3, hp, LANES), f32),
                        pltpu.SemaphoreType.DMA((2, 2))])
    return pl.pallas_call(
        functools.partial(_sattn_kernel, layer=layer, n_heads=n_heads, ppsub=ppsub),
        grid_spec=grid_spec,
        out_shape=jax.ShapeDtypeStruct((nb, hp, kl), f32),
        compiler_params=pltpu.CompilerParams(dimension_semantics=("arbitrary",), vmem_limit_bytes=VMEM_LIMIT),
        name="sample_attn",
    )(page_table.reshape(-1), cache_lat, cache_krt, wkt, qabs, qr, latn, krn)


def _spost_kernel(x_ref, lru_ref, olat_ref, g1_ref, sh2_ref, gs2_ref, g2_ref, mog_ref, wv_ref, wo_ref,
                  wup_ref, fcw_ref, fcb_ref, wdn_ref, fbuf_ref, y_o, up_o, *, n_heads):
    nc = fcw_ref.shape[0] // 2
    ck = fcw_ref.shape[2]
    kconv = fcw_ref.shape[1]
    heads = [_dot(olat_ref[hd].astype(bf16), wv_ref[:, hd * LANES:(hd + 1) * LANES]) for hd in range(n_heads)]
    attn = jnp.concatenate([heads[2 * pp] + heads[2 * pp + 1] for pp in range(n_heads // 2)], axis=-1)
    attn = _rms(attn, mog_ref[...]).astype(bf16)
    x1, h2 = _mix_and_norm2(x_ref[...], lru_ref[...], attn, wo_ref, g1_ref[...], gs2_ref[...], sh2_ref[...])

    def conv(up, c):
        out = fcb_ref[c]
        cw = fcw_ref[c]
        for j in range(kconv - 1):
            out = out + fbuf_ref[j, :, c * ck:(c + 1) * ck] * cw[j:j + 1, :]
        up_o[:, c * ck:(c + 1) * ck] = up
        return out + up * cw[kconv - 1:kconv, :]

    acc = jnp.zeros(x1.shape, f32)
    for c in range(nc):
        val = conv(_dot(h2, _up_cols(wup_ref, c, ck)), c)
        gt = conv(_dot(h2, _up_cols(wup_ref, nc + c, ck)), nc + c)
        acc = acc + _dot((_gelu_x2(gt) * val).astype(bf16), wdn_ref[c * ck:(c + 1) * ck, :])
    y_o[...] = x1 + g2_ref[...] * acc


def _spost_call(x, lru, olat, g1, sh2, gs2, g2, fbuf, p, n_heads):
    nb, d = x.shape
    return pl.pallas_call(
        functools.partial(_spost_kernel, n_heads=n_heads),
        out_shape=[jax.ShapeDtypeStruct((nb, d), f32), jax.ShapeDtypeStruct((nb, p["wup"].shape[1]), f32)],
        compiler_params=pltpu.CompilerParams(vmem_limit_bytes=VMEM_LIMIT),
        name="sample_ffn",
    )(x, lru, olat, g1, sh2, gs2, g2, p["mog"], p["wv"], p["wo"], p["wup"], p["fcw"], p["fcb"], p["wdn"], fbuf)


def _take_cols(wmat, idx):
    padded = jnp.concatenate([wmat, jnp.zeros(wmat.shape[:-1] + (1,), wmat.dtype)], axis=-1)
    return jnp.take(padded, jnp.asarray(np.where(idx < 0, wmat.shape[-1], idx)), axis=-1)


def _block_diag_groups(wh):
    nh, hd, _ = wh.shape
    per = MXU_DIM // hd
    groups = []
    for g in range(nh // per):
        blk = jnp.zeros((MXU_DIM, MXU_DIM), wh.dtype)
        for j in range(per):
            blk = lax.dynamic_update_slice(blk, wh[g * per + j], (j * hd, j * hd))
        groups.append(blk)
    return jnp.stack(groups).astype(bf16)


def _prep_layer(l, n_heads, scale, w_in, lru_conv_w, lru_conv_b, lru_w_a, lru_b_a, lru_w_x, lru_b_x, lru_lambda,
                norm1_g, q_lora_norm_g, w_uq, q_nope_norm_g, q_rope_norm_g, kv_lora_norm_g, k_rope_norm_g, w_ukv,
                k_nope_norm_g, lru_out_norm_g, mla_out_norm_g, w_o, norm2_g, w_up, ffn_conv_w, ffn_conv_b, w_down):
    src = _h128_src()
    w = lru_lambda.shape[-1]
    ql = q_lora_norm_g.shape[-1]
    kl = kv_lora_norm_g.shape[-1]
    qk = _NOPE + _ROPE
    vd = w_ukv.shape[-1] // n_heads - _NOPE
    row = lambda v: v.reshape(1, -1).astype(f32)

    kr_src = np.where(src >= _NOPE, src - _NOPE, -1)
    win = w_in[l]
    base = 2 * w + ql + kl
    win_ext = jnp.concatenate([win[:, :base], _take_cols(win[:, base:], kr_src)], axis=-1).astype(bf16)

    q_idx = np.concatenate([np.where(src >= 0, src + h * qk, -1) for h in range(n_heads)])
    wuq = _take_cols(w_uq[l], q_idx).astype(bf16)
    k_src = np.where((src >= 0) & (src < _NOPE), src, -1)
    k_idx = np.concatenate([np.where(k_src >= 0, k_src + h * (_NOPE + vd), -1) for h in range(n_heads)])
    wk = _take_cols(w_ukv[l], k_idx).astype(bf16)
    v_idx = []
    vone = np.zeros((1, n_heads * LANES), np.float32)
    for h in range(n_heads):
        slab = np.full((LANES,), -1, np.int64)
        off = (h % 2) * vd
        slab[off:off + vd] = h * (_NOPE + vd) + _NOPE + np.arange(vd)
        v_idx.append(slab)
        vone[0, h * LANES + (vd - off)] = 1.0
    wv = _take_cols(w_ukv[l], np.concatenate(v_idx)).astype(bf16)
    wkt_idx = np.concatenate([h * (_NOPE + vd) + np.arange(_NOPE) for h in range(n_heads)])
    wkt = jnp.take(w_ukv[l], jnp.asarray(wkt_idx), axis=-1).T.astype(bf16)

    nope_tab = lambda g: _take_cols(g.reshape(1, -1), k_src)
    rope_tab = lambda g: _take_cols(g.reshape(1, -1), kr_src)
    is_n = (k_src >= 0).astype(np.float32)
    is_r = (kr_src >= 0).astype(np.float32)
    msel = jnp.asarray(np.outer(is_n, is_n) / _NOPE + np.outer(is_r, is_r) / _ROPE).astype(bf16)
    gk_full = nope_tab(k_nope_norm_g[l]) + jnp.asarray(1.0 - is_n).reshape(1, LANES)
    gq = (nope_tab(q_nope_norm_g[l]) + rope_tab(q_rope_norm_g[l])) * scale * gk_full

    dff = w_down.shape[1]
    ck = MXU_DIM
    nc = dff // ck
    half_val = jnp.concatenate([jnp.full((nc, 1, 1), 0.5, f32), jnp.ones((nc, 1, 1), f32)], axis=0)
    fcw = ffn_conv_w[l].reshape(-1, 2 * nc, ck).transpose(1, 0, 2).astype(f32) * half_val
    fcb = ffn_conv_b[l].reshape(2 * nc, 1, ck).astype(f32) * half_val
    return dict(
        n1g=row(norm1_g[l]), win=win_ext, cw=lru_conv_w[l].astype(f32), cb=row(lru_conv_b[l]),
        wa=_block_diag_groups(lru_w_a[l]), ba=row(lru_b_a[l]), wx=_block_diag_groups(lru_w_x[l]), bx=row(lru_b_x[l]),
        lam=row(lru_lambda[l]), qlg=row(q_lora_norm_g[l]), wuq=wuq, gq=gq.astype(f32), msel=msel,
        kvg=row(kv_lora_norm_g[l]), gkr=rope_tab(k_rope_norm_g[l]).astype(f32), wk=wk,
        wv=wv, vone=jnp.asarray(vone), wkt=wkt, log=row(lru_out_norm_g[l]),
        mog=row(mla_out_norm_g[l]), wo=w_o[l].astype(bf16), n2g=row(norm2_g[l]), wup=w_up[l].astype(bf16),
        fcw=fcw, fcb=fcb, wdn=w_down[l].astype(bf16))


def _rope_tables(pos):
    half = _ROPE // 2
    inv = ROPE_THETA ** (-np.arange(0, _ROPE, 2, dtype=np.float64) / _ROPE)
    ang = np.asarray(pos, np.float64)[:, None] * inv[None, :]
    cos, sin = np.cos(ang), np.sin(ang)
    n = ang.shape[0]
    cc = np.zeros((n, LANES), np.float32)
    cc[:, 0:_R1_LO] = 1.0
    cc[:, _R1_LO + half:_R1_LO + half + 16] = 1.0
    cc[:, _R1_LO:_R1_LO + half] = cos
    cc[:, _R2_LO:_R2_LO + half] = cos
    ss = np.zeros((n, LANES), np.float32)
    ss[:, _R1_LO:_R1_LO + half] = -sin
    ss[:, _R2_LO:_R2_LO + half] = sin
    return jnp.asarray(cc), jnp.asarray(ss)


def _gain_rope_tables(cc, ss, gq, gkr):
    swap = lambda g: jnp.roll(g, LANES // 2, axis=-1)
    return cc * gq, ss * swap(gq), cc * gkr, ss * swap(gkr)


def kernel(x_prompt, x_sample, cache_kv_latent, cache_k_rope, state_lru_h, state_lru_conv, state_ffn_conv,
           page_table, c_prompt, c_sample, w_ada, b_ada, norm1_g, w_in, lru_conv_w, lru_conv_b, lru_w_a, lru_b_a,
           lru_w_x, lru_b_x, lru_lambda, q_lora_norm_g, w_uq, q_nope_norm_g, q_rope_norm_g, kv_lora_norm_g,
           k_rope_norm_g, w_ukv, k_nope_norm_g, lru_out_norm_g, mla_out_norm_g, w_o, norm2_g, w_up, ffn_conv_w,
           ffn_conv_b, w_down):
    b, s, d = x_prompt.shape
    nb, ds, _ = x_sample.shape
    depth = w_in.shape[0]
    assert ds == 1 and q_nope_norm_g.shape[-1] == _NOPE and q_rope_norm_g.shape[-1] == _ROPE
    n_heads = w_uq.shape[-1] // (_NOPE + _ROPE)
    scale = float(_NOPE + _ROPE) ** -0.5 * float(np.log2(np.e))
    npg = page_table.shape[1]
    n_past = npg * cache_kv_latent.shape[2]
    ts = 64 if s % 64 == 0 else 32
    tq = min(2048, s)
    tk = min(256, s)
    tm = min(512, s)
    ppsub = 16 if npg % 32 == 0 else 1
    hp = 16

    cache_krt = jnp.swapaxes(cache_k_rope, 2, 3)
    cc_p, ss_p = _rope_tables(np.arange(s))
    cc_s, ss_s = _rope_tables(n_past + np.arange(1))

    y_p = x_prompt
    y_s = x_sample.reshape(nb, d)
    c_all = jnp.concatenate([c_prompt, c_sample], axis=0)
    outs_p = [[] for _ in range(5)]
    outs_s = [[] for _ in range(5)]
    for l in range(depth):
        p = _prep_layer(l, n_heads, scale, w_in, lru_conv_w, lru_conv_b, lru_w_a, lru_b_a, lru_w_x, lru_b_x,
                        lru_lambda, norm1_g, q_lora_norm_g, w_uq, q_nope_norm_g, q_rope_norm_g, kv_lora_norm_g,
                        k_rope_norm_g, w_ukv, k_nope_norm_g, lru_out_norm_g, mla_out_norm_g, w_o, norm2_g, w_up,
                        ffn_conv_w, ffn_conv_b, w_down)
        mod = _mod_call(c_all, w_ada[l], b_ada[l])
        sh1, sc1, g1, sh2, sc2, g2 = jnp.split(mod, 6, axis=-1)
        mods = [sh1, p["n1g"] * (1.0 + sc1), g1, sh2, p["n2g"] * (1.0 + sc2), g2]
        mp = [m_[:b, None, :] for m_ in mods]
        ms = [m_[b:] for m_ in mods]

        lru_p, q_p, k_p, v_p, kvlat_p, krope_p, hlast_p, xtail_p = _inproj_call(
            y_p, mp[0], mp[1], _gain_rope_tables(cc_p, ss_p, p["gq"], p["gkr"]), p, n_heads, ts)
        attn_p = _attn_call(q_p, k_p, v_p, p["mog"], n_heads, tq, tk)
        y_p, ftail_p = _ffn_call(y_p, lru_p, attn_p, mp[2], mp[3], mp[4], mp[5], p, tm)
        kc = lru_conv_w.shape[1]
        fk = ffn_conv_w.shape[1]
        for j, o in enumerate((kvlat_p, krope_p, hlast_p, xtail_p[:, 8 - (kc - 1):], ftail_p[:, 8 - (fk - 1):])):
            outs_p[j].append(o)

        cbuf = jnp.swapaxes(state_lru_conv[l], 0, 1)
        fbuf = jnp.swapaxes(state_ffn_conv[l], 0, 1)
        lru_s, xlru_s, hnew_s, qabs, qr, kvn_s, kr_s = _spre_call(
            y_s, ms[0], ms[1], _gain_rope_tables(cc_s, ss_s, p["gq"], p["gkr"]), cbuf, state_lru_h[l], p, n_heads,
            n_past == 0)
        pad_heads = lambda t: jnp.pad(jnp.swapaxes(t, 0, 1), ((0, 0), (0, hp - n_heads), (0, 0)))
        olat = _sattn_call(page_table, cache_kv_latent, cache_krt, l, p["wkt"], pad_heads(qabs), pad_heads(qr),
                           kvn_s[:, None, :], kr_s[:, :, None], n_heads, ppsub)
        olat = jnp.swapaxes(olat[:, :n_heads], 0, 1)
        y_s, up_s = _spost_call(y_s, lru_s, olat, ms[2], ms[3], ms[4], ms[5], fbuf, p, n_heads)
        lru_conv_new = jnp.concatenate([state_lru_conv[l][:, 1:], xlru_s[:, None, :]], axis=1)
        ffn_conv_new = jnp.concatenate([state_ffn_conv[l][:, 1:], up_s[:, None, :]], axis=1)
        for j, o in enumerate((kvn_s[:, None, :], kr_s[:, None, :], hnew_s, lru_conv_new, ffn_conv_new)):
            outs_s[j].append(o)

    return (y_p, y_s.reshape(nb, 1, d), *[jnp.stack(o) for o in outs_p], *[jnp.stack(o) for o in outs_s])
```

```python
import functools

import numpy as np
import jax
import jax.numpy as jnp
from jax import lax
from jax.experimental import pallas as pl
from jax.experimental.pallas import tpu as pltpu

f32 = jnp.float32
bf16 = jnp.bfloat16

EPS = 1e-6
LRU_C = 8.0
ROPE_THETA = 10000.0
LANES = 128
MXU_DIM = 256
BF16_ROWS = 16
VMEM_LIMIT = 56 * 1024 * 1024

_NOPE, _ROPE = 64, 32
_R1_LO, _R2_LO = 48, 112


def _h128_src():
    src = np.full((LANES,), -1, np.int32)
    src[0:48] = np.arange(0, 48)
    src[48:64] = _NOPE + np.arange(0, 16)
    src[64:80] = np.arange(48, 64)
    src[112:128] = _NOPE + 16 + np.arange(0, 16)
    return src


def _dot(a, b):
    return jnp.dot(a, b, preferred_element_type=f32)


def _dot_nt(a, b):
    return lax.dot_general(a, b, (((1,), (1,)), ((), ())), preferred_element_type=f32)


def _rms(x, g):
    ms = jnp.mean(x * x, axis=-1, keepdims=True)
    return x * lax.rsqrt(ms + EPS) * g


_GELU_C0 = float(np.sqrt(2.0 / np.pi))
_GELU_C1 = _GELU_C0 * 0.044715


def _gelu_x2(x):
    return x * (1.0 + jnp.tanh(x * (_GELU_C0 + _GELU_C1 * (x * x))))


def _neg_expm1_2x(y):
    t = jnp.tanh(y)
    return -2.0 * t / (1.0 - t)


def _lru_gates(x_conv, wa_ref, ba, wx_ref, bx, lam):
    xb = x_conv.astype(bf16)
    ng = wa_ref.shape[0]
    ra = jnp.concatenate([_dot(xb[:, g * MXU_DIM:(g + 1) * MXU_DIM], wa_ref[g]) for g in range(ng)], axis=-1) + ba
    ia = jnp.concatenate([_dot(xb[:, g * MXU_DIM:(g + 1) * MXU_DIM], wx_ref[g]) for g in range(ng)], axis=-1) + bx
    r = jax.nn.sigmoid(ra)
    ig = jax.nn.sigmoid(ia)
    log_a = r * ((-LRU_C) * jax.nn.softplus(-lam))
    a = jnp.exp(log_a)
    mult = jnp.sqrt(_neg_expm1_2x(log_a))
    return a, mult, ig * x_conv


def _group_ms(x, msel):
    return _dot((x * x).astype(bf16), msel)


def _norm_rope(x, msel, ca, sb, to3d):
    inv = lax.rsqrt(_group_ms(x, msel) + EPS)
    return to3d(inv) * (to3d(x) * ca + to3d(pltpu.roll(x, LANES // 2, 1)) * sb)


def _q_heads(q, msel, ca, sb, n_heads, to3d):
    return [_norm_rope(q[:, h * LANES:(h + 1) * LANES], msel, ca, sb, to3d) for h in range(n_heads)]


def _mod_kernel(c_ref, w_ref, b_ref, o_ref):
    c = c_ref[...]
    sc = (c * jax.nn.sigmoid(c)).astype(bf16)
    o_ref[...] = _dot(sc, w_ref[...].astype(bf16)) + b_ref[...]


def _mod_call(c_all, w_ada, b_ada):
    m, d = c_all.shape
    n = w_ada.shape[1]
    tn = n // 4 if n % (4 * LANES) == 0 else n
    return pl.pallas_call(
        _mod_kernel,
        grid=(n // tn,),
        in_specs=[pl.BlockSpec((m, d), lambda j: (0, 0)),
                  pl.BlockSpec((d, tn), lambda j: (0, j)),
                  pl.BlockSpec((1, tn), lambda j: (0, j))],
        out_specs=pl.BlockSpec((m, tn), lambda j: (0, j)),
        out_shape=jax.ShapeDtypeStruct((m, n), f32),
        compiler_params=pltpu.CompilerParams(dimension_semantics=("arbitrary",), vmem_limit_bytes=VMEM_LIMIT),
        name="adaln_mod",
    )(c_all, w_ada, b_ada.reshape(1, n))


def _inproj_kernel(x_ref, sh_ref, sc_ref, qca_ref, qsb_ref, kca_ref, ksb_ref, win_ref, cw_ref, cb_ref,
                   wa_ref, ba_ref, wx_ref, bx_ref, lam_ref, qlg_ref, wuq_ref, msel_ref,
                   kvg_ref, wk_ref, wv_ref, vone_ref, log_ref,
                   lru_o, q_o, k_o, v_o, kvlat_o, krope_o, hlast_o, xtail_o,
                   xtail_s, a_s, u_s, hs_s, h_s, *, n_heads):
    i = pl.program_id(0)
    nb, ts, d = x_ref.shape
    m = nb * ts
    w = lam_ref.shape[-1]
    ql = qlg_ref.shape[-1]
    kl = kvg_ref.shape[-1]
    kconv = cw_ref.shape[0]
    nlc = w // LANES

    @pl.when(i == 0)
    def _():
        xtail_s[...] = jnp.zeros_like(xtail_s)
        h_s[...] = jnp.zeros_like(h_s)

    def to3d(v):
        return v.reshape(nb, ts, v.shape[-1])

    x = x_ref[...]
    h = _rms(x, sc_ref[...]) + sh_ref[...]
    z = _dot(h.reshape(m, d).astype(bf16), win_ref[...])
    x_lru = z[:, 0:w]
    g_lru = z[:, w:2 * w]
    q_lat = z[:, 2 * w:2 * w + ql]
    kv_lat = z[:, 2 * w + ql:2 * w + ql + kl]
    kr_pre = z[:, 2 * w + ql + kl:2 * w + ql + kl + LANES]

    t_idx = lax.broadcasted_iota(jnp.int32, (m, 1), 0) & (ts - 1)
    r8 = lax.broadcasted_iota(jnp.int32, (1, 8, 1), 1)
    tail2d = xtail_s[...].reshape(nb * 8, w)
    x_conv = to3d(jnp.broadcast_to(cb_ref[...], (m, w)))
    for j in range(kconv - 1):
        k = kconv - 1 - j
        rolled = to3d(pltpu.roll(x_lru, k, 0))
        prev = pltpu.roll(tail2d, nb * 8 + k - 8, 0).reshape(nb, 8, w)
        sh = jnp.concatenate([jnp.where(r8 < k, prev, rolled[:, 0:8]), rolled[:, 8:]], axis=1)
        x_conv = x_conv + sh * cw_ref[j:j + 1, :]
    x_conv = (x_conv + to3d(x_lru) * cw_ref[kconv - 1:kconv, :]).reshape(m, w)
    xtail_s[...] = to3d(x_lru)[:, ts - 8:, :]
    xtail_o[...] = to3d(x_lru)[:, ts - 8:, :]

    a, mult, ux = _lru_gates(x_conv, wa_ref, ba_ref[...], wx_ref, bx_ref[...], lam_ref[...])
    mult = jnp.where(jnp.logical_and(t_idx == 0, i == 0), 1.0, mult)
    u = mult * ux
    pitch = a_s.shape[1] // nb
    for j in range(nlc):
        for bi in range(nb):
            a_s[j, bi * pitch:bi * pitch + ts, :] = a[bi * ts:(bi + 1) * ts, j * LANES:(j + 1) * LANES]
            u_s[j, bi * pitch:bi * pitch + ts, :] = u[bi * ts:(bi + 1) * ts, j * LANES:(j + 1) * LANES]

    def scan_step(t, hc):
        out = []
        for j in range(nlc):
            hj = a_s[j, pl.ds(t, nb, stride=pitch), :] * hc[j] + u_s[j, pl.ds(t, nb, stride=pitch), :]
            hs_s[j, pl.ds(t, nb, stride=pitch), :] = hj
            out.append(hj)
        return tuple(out)

    hc = lax.fori_loop(0, ts, scan_step, tuple(h_s[j] for j in range(nlc)), unroll=8)
    for j in range(nlc):
        h_s[j] = hc[j]
    hlast_o[...] = jnp.concatenate(list(hc), axis=-1)
    hs = jnp.concatenate(
        [jnp.concatenate([hs_s[j, bi * pitch:bi * pitch + ts, :] for bi in range(nb)], axis=0) for j in range(nlc)],
        axis=-1)
    lru_out = hs * jax.nn.gelu(g_lru)
    lru_o[...] = to3d(_rms(lru_out, log_ref[...]).astype(bf16))

    msel = msel_ref[...]
    qn = _rms(q_lat, qlg_ref[...]).astype(bf16)
    q = _dot(qn, wuq_ref[...])
    for hd, o in enumerate(_q_heads(q, msel, qca_ref[...][None], qsb_ref[...][None], n_heads, to3d)):
        q_o[:, :, hd * LANES:(hd + 1) * LANES] = o.astype(bf16)

    kvn = _rms(kv_lat, kvg_ref[...])
    kvlat_o[...] = to3d(kvn)
    kr = _norm_rope(kr_pre, msel, kca_ref[...][None], ksb_ref[...][None], to3d)
    krope_o[...] = jnp.concatenate([kr[:, :, _R1_LO:_R1_LO + 16], kr[:, :, _R2_LO:_R2_LO + 16]], axis=-1)
    kvb = kvn.astype(bf16)
    kk = _dot(kvb, wk_ref[...])
    v_o[...] = to3d((_dot(kvb, wv_ref[...]) + vone_ref[...]).astype(bf16))
    for hd in range(n_heads):
        kh = kk[:, hd * LANES:(hd + 1) * LANES]
        khn = kh * lax.rsqrt(_group_ms(kh, msel) + EPS)
        k_o[:, :, hd * LANES:(hd + 1) * LANES] = (to3d(khn) + kr).astype(bf16)


def _const_spec(shape):
    nd = len(shape)
    return pl.BlockSpec(shape, lambda *_: (0,) * nd, pipeline_mode=pl.Buffered(1))


def _inproj_call(x, sh1, sc1, rope_tabs, p, n_heads, ts):
    b, s, d = x.shape
    w = p["lam"].shape[-1]
    kl = p["kvg"].shape[-1]
    hw = n_heads * LANES
    m = b * ts
    consts = [p["win"], p["cw"], p["cb"], p["wa"], p["ba"], p["wx"], p["bx"], p["lam"], p["qlg"],
              p["wuq"], p["msel"], p["kvg"], p["wk"], p["wv"], p["vone"], p["log"]]
    in_specs = ([pl.BlockSpec((b, ts, d), lambda i: (0, i, 0)), _const_spec(sh1.shape), _const_spec(sc1.shape)]
                + [pl.BlockSpec((ts, LANES), lambda i: (i, 0))] * len(rope_tabs)
                + [_const_spec(c.shape) for c in consts])

    def tile(n, dt):
        return pl.BlockSpec((b, ts, n), lambda i: (0, i, 0)), jax.ShapeDtypeStruct((b, s, n), dt)

    outs = [tile(w, bf16), tile(hw, bf16), tile(hw, bf16), tile(hw, bf16), tile(kl, f32), tile(_ROPE, f32),
            (pl.BlockSpec((b, w), lambda i: (0, 0)), jax.ShapeDtypeStruct((b, w), f32)),
            (pl.BlockSpec((b, 8, w), lambda i: (0, 0, 0)), jax.ShapeDtypeStruct((b, 8, w), f32))]
    nlc = w // LANES
    return pl.pallas_call(
        functools.partial(_inproj_kernel, n_heads=n_heads),
        grid=(s // ts,),
        in_specs=in_specs,
        out_specs=[o[0] for o in outs],
        out_shape=[o[1] for o in outs],
        scratch_shapes=[pltpu.VMEM((b, 8, w), f32)] + [pltpu.VMEM((nlc, b * (ts + 8), LANES), f32)] * 3
        + [pltpu.VMEM((nlc, b, LANES), f32)],
        compiler_params=pltpu.CompilerParams(dimension_semantics=("arbitrary",), vmem_limit_bytes=VMEM_LIMIT),
        name="prompt_inproj",
    )(x, sh1, sc1, *rope_tabs, *consts)


def _attn_kernel(q_ref, k_ref, v_ref, g_ref, o_ref, m_s, acc_s, *, n_heads, tk):
    qi = pl.program_id(1)
    tq = q_ref.shape[1]
    nmask = tq // tk
    n_full = qi * nmask
    m_s[...] = jnp.full_like(m_s, -1e30)
    acc_s[...] = jnp.zeros_like(acc_s)

    def block(j, r0, masked):
        start = pl.multiple_of(j * tk, tk)
        if masked:
            vis = (lax.broadcasted_iota(jnp.int32, (tq - r0, tk), 1)
                   <= lax.broadcasted_iota(jnp.int32, (tq - r0, tk), 0))
        for hd in range(n_heads):
            hs = slice(hd * LANES, (hd + 1) * LANES)
            s = _dot_nt(q_ref[0, r0:, hs], k_ref[0, pl.ds(start, tk), hs])
            if masked:
                s = jnp.where(vis, s, -1e30)
            mx = m_s[hd, r0:, :]
            m_new = jnp.maximum(mx, jnp.max(s, axis=-1, keepdims=True))
            alpha = jnp.exp2(mx - m_new)
            pm = jnp.exp2(s - jnp.concatenate([m_new] * (tk // LANES), axis=-1))
            acc_s[hd, r0:, :] = alpha * acc_s[hd, r0:, :] + _dot(pm.astype(bf16), v_ref[0, pl.ds(start, tk), hs])
            m_s[hd, r0:, :] = m_new

    def full_block(j, c):
        block(j, 0, False)
        return c

    lax.fori_loop(0, n_full, full_block, 0)
    for jm in range(nmask):
        block(n_full + jm, jm * tk, True)
    lane = lax.broadcasted_iota(jnp.int32, (1, LANES), 1)
    half = LANES // 2
    heads = []
    for hd in range(n_heads):
        acc = acc_s[hd]
        lo = (hd % 2) * half
        one_lane = (half - lo)
        l = jnp.sum(jnp.where(lane == one_lane, acc, 0.0), axis=-1, keepdims=True)
        heads.append(jnp.where((lane >= lo) & (lane < lo + half), acc, 0.0) / l)
    o = jnp.concatenate([heads[2 * pp] + heads[2 * pp + 1] for pp in range(n_heads // 2)], axis=-1)
    o_ref[0] = _rms(o, g_ref[...]).astype(bf16)


def _attn_call(q, k, v, g, n_heads, tq, tk):
    b, s, hw = q.shape
    wout = g.shape[-1]
    return pl.pallas_call(
        functools.partial(_attn_kernel, n_heads=n_heads, tk=tk),
        grid=(b, s // tq),
        in_specs=[pl.BlockSpec((1, tq, hw), lambda bi, qi: (bi, qi, 0)),
                  pl.BlockSpec((1, s, hw), lambda bi, qi: (bi, 0, 0)),
                  pl.BlockSpec((1, s, hw), lambda bi, qi: (bi, 0, 0)),
                  pl.BlockSpec((1, wout), lambda bi, qi: (0, 0))],
        out_specs=pl.BlockSpec((1, tq, wout), lambda bi, qi: (bi, qi, 0)),
        out_shape=jax.ShapeDtypeStruct((b, s, wout), bf16),
        scratch_shapes=[pltpu.VMEM((n_heads, tq, LANES), f32), pltpu.VMEM((n_heads, tq, LANES), f32)],
        compiler_params=pltpu.CompilerParams(dimension_semantics=("arbitrary", "arbitrary"),
                                             vmem_limit_bytes=VMEM_LIMIT),
        name="prompt_attn",
    )(q, k, v, g)


def _mix_and_norm2(x, lru, attn, wo_ref, g1, gs2, sh2):
    w = lru.shape[-1]
    mixed = _dot(lru, wo_ref[pl.ds(0, w), :]) + _dot(attn, wo_ref[pl.ds(w, attn.shape[-1]), :])
    x1 = x + g1 * mixed
    h2 = (_rms(x1, gs2) + sh2).astype(bf16)
    return x1, h2


def _up_cols(wup_ref, c, ck):
    return wup_ref[:, c * ck:(c + 1) * ck]


def _ffn_kernel(x_ref, lru_ref, attn_ref, g1_ref, sh2_ref, gs2_ref, g2_ref, wo_ref, wup_ref,
                fcw_ref, fcb_ref, wdn_ref, y_o, tail_o, tail_s, act_s):
    si = pl.program_id(1)
    tm = x_ref.shape[1]
    nc = fcw_ref.shape[0] // 2
    ck = fcw_ref.shape[2]
    kconv = fcw_ref.shape[1]

    @pl.when(si == 0)
    def _():
        tail_s[...] = jnp.zeros_like(tail_s)

    x1, h2 = _mix_and_norm2(x_ref[0], lru_ref[0], attn_ref[0], wo_ref, g1_ref[0], gs2_ref[0], sh2_ref[0])
    r8 = lax.broadcasted_iota(jnp.int32, (8, 1), 0)

    def conv(up, c):
        tail = tail_s[c]
        first = up[0:8]
        out = fcb_ref[c]
        cw = fcw_ref[c]
        for j in range(kconv - 1):
            k = kconv - 1 - j
            head = jnp.where(r8 < k, pltpu.roll(tail, k, 0), pltpu.roll(first, k, 0))
            sh = jnp.concatenate([head, pltpu.roll(up, k, 0)[8:]], axis=0)
            out = out + sh * cw[j:j + 1, :]
        out = out + up * cw[kconv - 1:kconv, :]
        tail_s[c] = up[tm - 8:]
        tail_o[0, :, c * ck:(c + 1) * ck] = up[tm - 8:]
        return out

    for c in range(nc):
        val = conv(_dot(h2, _up_cols(wup_ref, c, ck)), c)
        gt = conv(_dot(h2, _up_cols(wup_ref, nc + c, ck)), nc + c)
        act_s[:, c * ck:(c + 1) * ck] = (_gelu_x2(gt) * val).astype(bf16)
    acc = _dot(act_s[...], wdn_ref[...])
    y_o[0] = x1 + g2_ref[0] * acc


def _ffn_call(x, lru, attn, g1, sh2, gs2, g2, p, tm):
    b, s, d = x.shape
    w = lru.shape[-1]
    nc2, _, ck = p["fcw"].shape
    nc = nc2 // 2
    row = lambda n: pl.BlockSpec((1, tm, n), lambda bi, si: (bi, si, 0))
    modspec = pl.BlockSpec((1, 1, d), lambda bi, si: (bi, 0, 0))
    consts = [p["wo"], p["wup"], p["fcw"], p["fcb"], p["wdn"]]
    return pl.pallas_call(
        _ffn_kernel,
        grid=(b, s // tm),
        in_specs=[row(d), row(w), row(attn.shape[-1]), modspec, modspec, modspec, modspec]
        + [_const_spec(c.shape) for c in consts],
        out_specs=[row(d), pl.BlockSpec((1, 8, 2 * nc * ck), lambda bi, si: (bi, 0, 0))],
        out_shape=[jax.ShapeDtypeStruct((b, s, d), f32), jax.ShapeDtypeStruct((b, 8, 2 * nc * ck), f32)],
        scratch_shapes=[pltpu.VMEM((2 * nc, 8, ck), f32), pltpu.VMEM((tm, nc * ck), bf16)],
        compiler_params=pltpu.CompilerParams(dimension_semantics=("arbitrary", "arbitrary"),
                                             vmem_limit_bytes=VMEM_LIMIT),
        name="prompt_ffn",
    )(x, lru, attn, g1, sh2, gs2, g2, *consts)


def _spre_kernel(x_ref, sh_ref, sc_ref, qca_ref, qsb_ref, kca_ref, ksb_ref, win_ref, cbuf_ref, cw_ref,
                 cb_ref, wa_ref, ba_ref, wx_ref, bx_ref, lam_ref, h0_ref, qlg_ref, wuq_ref, msel_ref,
                 kvg_ref, wk_ref, log_ref,
                 lru_o, xlru_o, hnew_o, qabs_o, qr_o, kvn_o, kr_o, *, n_heads, first_pos):
    w = lam_ref.shape[-1]
    ql = qlg_ref.shape[-1]
    kl = kvg_ref.shape[-1]
    kconv = cw_ref.shape[0]
    ident = lambda v: v

    h = _rms(x_ref[...], sc_ref[...]) + sh_ref[...]
    z = _dot(h.astype(bf16), win_ref[...])
    x_lru = z[:, 0:w]
    g_lru = z[:, w:2 * w]
    q_lat = z[:, 2 * w:2 * w + ql]
    kv_lat = z[:, 2 * w + ql:2 * w + ql + kl]
    kr_pre = z[:, 2 * w + ql + kl:2 * w + ql + kl + LANES]

    x_conv = cb_ref[...]
    for j in range(kconv - 1):
        x_conv = x_conv + cbuf_ref[j] * cw_ref[j:j + 1, :]
    x_conv = x_conv + x_lru * cw_ref[kconv - 1:kconv, :]
    xlru_o[...] = x_lru
    a, mult, ux = _lru_gates(x_conv, wa_ref, ba_ref[...], wx_ref, bx_ref[...], lam_ref[...])
    if first_pos:
        mult = jnp.ones_like(mult)
    hn = a * h0_ref[...] + mult * ux
    hnew_o[...] = hn
    lru_o[...] = _rms(hn * jax.nn.gelu(g_lru), log_ref[...]).astype(bf16)

    msel = msel_ref[...]
    qn = _rms(q_lat, qlg_ref[...]).astype(bf16)
    q = _dot(qn, wuq_ref[...])
    for hd, o in enumerate(_q_heads(q, msel, qca_ref[...], qsb_ref[...], n_heads, ident)):
        wkh = wk_ref[:, hd * LANES:(hd + 1) * LANES]
        qabs_o[hd] = _dot_nt(o.astype(bf16), wkh).astype(bf16)
        qr_o[hd] = jnp.concatenate([o[:, _R1_LO:_R1_LO + 16], o[:, _R2_LO:_R2_LO + 16]], axis=-1).astype(bf16)

    kvn_o[...] = _rms(kv_lat, kvg_ref[...])
    kr = _norm_rope(kr_pre, msel, kca_ref[...], ksb_ref[...], ident)
    kr_o[...] = jnp.concatenate([kr[:, _R1_LO:_R1_LO + 16], kr[:, _R2_LO:_R2_LO + 16]], axis=-1)


def _spre_call(x, sh1, sc1, rope_tabs, cbuf, h0, p, n_heads, first_pos):
    nb, d = x.shape
    w = p["lam"].shape[-1]
    kl = p["kvg"].shape[-1]
    out_shape = [jax.ShapeDtypeStruct((nb, w), bf16), jax.ShapeDtypeStruct((nb, w), f32),
                 jax.ShapeDtypeStruct((nb, w), f32), jax.ShapeDtypeStruct((n_heads, nb, kl), bf16),
                 jax.ShapeDtypeStruct((n_heads, nb, _ROPE), bf16), jax.ShapeDtypeStruct((nb, kl), f32),
                 jax.ShapeDtypeStruct((nb, _ROPE), f32)]
    return pl.pallas_call(
        functools.partial(_spre_kernel, n_heads=n_heads, first_pos=first_pos),
        out_shape=out_shape,
        compiler_params=pltpu.CompilerParams(vmem_limit_bytes=VMEM_LIMIT),
        name="sample_inproj",
    )(x, sh1, sc1, *rope_tabs, p["win"], cbuf, p["cw"], p["cb"], p["wa"], p["ba"], p["wx"], p["bx"],
      p["lam"], h0, p["qlg"], p["wuq"], p["msel"], p["kvg"], p["wk"], p["log"])


def _sattn_kernel(pt_ref, lat_hbm, krt_hbm, wkt_ref, qabs_ref, qr_ref, latn_ref, krn_ref, o_ref,
                  lat_buf, kr_buf, lhs_s, latb_s, s_s, car_s, sems, *, layer, n_heads, ppsub):
    b = pl.program_id(0)
    nseq = pl.num_programs(0) - 1
    _, npg, page, kl = lat_buf.shape
    nk = wkt_ref.shape[0]
    hp = qabs_ref.shape[0]
    tk = ppsub * page
    slot = lax.rem(b, 2)

    def page_copies(src_page, slot_, pg):
        return (pltpu.make_async_copy(lat_hbm.at[layer, src_page], lat_buf.at[slot_, pg], sems.at[0, slot_]),
                pltpu.make_async_copy(krt_hbm.at[layer, src_page], kr_buf.at[slot_, pg], sems.at[1, slot_]))

    def start_pages(seq, slot_, pg0, n):
        for i in range(n):
            for cp in page_copies(pt_ref[seq * npg + pg0 + i], slot_, pg0 + i):
                cp.start()

    @pl.when(b == 0)
    def _():
        def body(pg, c):
            start_pages(0, 0, pg, 1)
            return c
        lax.fori_loop(0, npg, body, 0)
        lhs_s[0:nk, :] = wkt_ref[...]
        latb_s[1] = jnp.zeros(latb_s.shape[1:], bf16)
        s_s[1] = jnp.zeros(s_s.shape[1:], f32)
        car_s[...] = jnp.zeros_like(car_s)

    lhs_s[nk:nk + hp, :] = qabs_ref[...]

    def wait_slot(slot_):
        pltpu.make_async_copy(lat_hbm.at[layer, pl.ds(0, npg)], lat_buf.at[slot_], sems.at[0, slot_]).wait()
        pltpu.make_async_copy(krt_hbm.at[layer, pl.ds(0, npg)], kr_buf.at[slot_], sems.at[1, slot_]).wait()

    wait_slot(slot)

    def scores(latb, krt):
        n = latb.shape[0]
        big = _dot_nt(lhs_s[...], latb)
        knt = big[0:nk]
        ssq = jnp.sum((knt * knt).reshape(n_heads, _NOPE, n), axis=1)
        rs = lax.rsqrt(ssq * (1.0 / _NOPE) + EPS)
        sr = _dot(qr_ref[...], krt)
        s8 = big[nk:nk + n_heads] * rs + sr[0:n_heads]
        return jnp.concatenate([s8, jnp.zeros((hp - n_heads, n), f32)], axis=0)

    def softmax_update(carry, parts):
        m_run, l_run, acc = carry
        m_new = m_run
        for _, s in parts:
            m_new = jnp.maximum(m_new, jnp.max(s, axis=-1, keepdims=True))
        alpha = jnp.exp2(m_run - m_new)
        l_new = alpha * l_run
        acc = alpha * acc
        for latb, s in parts:
            pm = jnp.exp2(s - jnp.concatenate([m_new] * (s.shape[-1] // LANES), axis=-1))
            l_new = l_new + jnp.sum(pm, axis=-1, keepdims=True)
            acc = acc + _dot(pm.astype(bf16), latb)
        return m_new, l_new, acc

    nxt = jnp.minimum(b + 1, nseq - 1)
    nsub = npg // ppsub
    spi = 6 if (nsub - 2) % 6 == 0 else 2
    trips = (nsub - 2) // spi

    def request(j):
        if trips != 1:
            start_pages(nxt, 1 - slot, j * ppsub, ppsub)
        elif 2 * j < nsub:
            start_pages(nxt, 1 - slot, 2 * j * ppsub, 2 * ppsub)

    def stage(j, st, with_new_token=False):
        request(j)
        p0 = j * ppsub if isinstance(j, int) else pl.multiple_of(j * ppsub, ppsub)
        lat = [lat_buf[slot, pl.ds(p0, ppsub)].reshape(tk, kl).astype(bf16)]
        krt = [kr_buf[slot, p0 + i].astype(bf16) for i in range(ppsub)]
        if with_new_token:
            lat.append(jnp.broadcast_to(latn_ref[...], (page, kl)).astype(bf16))
            krt.append(jnp.broadcast_to(krn_ref[...], (krn_ref.shape[0], page)).astype(bf16))
        latb = jnp.concatenate(lat, axis=0)
        s = scores(latb, jnp.concatenate(krt, axis=-1))
        n = latb.shape[0]
        if with_new_token:
            s = jnp.where(lax.broadcasted_iota(jnp.int32, s.shape, 1) <= tk, s, -1e30)
        latb_s[st, 0:n, :] = latb
        s_s[st, :, 0:n] = s

    def stashed(st, n=tk):
        return [(latb_s[st, 0:n, :], s_s[st, :, 0:n])]

    stage(0, 0)
    _, l_prev, acc_prev = softmax_update((car_s[0], car_s[1], car_s[2]), stashed(1, tk + page))
    o_ref[...] = acc_prev / l_prev

    def body(k, carry):
        for u in range(0, spi, 2):
            stage(spi * k + u + 1, 1)
            carry = softmax_update(carry, stashed(0))
            stage(spi * k + u + 2, 0)
            carry = softmax_update(carry, stashed(1))
        return carry

    carry = (jnp.full((hp, LANES), -1e30, f32), jnp.zeros((hp, LANES), f32), jnp.zeros((hp, kl), f32))
    carry = body(0, carry) if trips == 1 else lax.fori_loop(0, trips, body, carry)
    stage(nsub - 1, 1, with_new_token=True)
    carry = softmax_update(carry, stashed(0))
    for i in range(3):
        car_s[i] = carry[i]

    @pl.when(b == nseq)
    def _():
        wait_slot(1 - slot)


def _sattn_call(page_table, cache_lat, cache_krt, layer, wkt, qabs, qr, latn, krn, n_heads, ppsub):
    nb, npg = page_table.shape
    _, _, page, kl = cache_lat.shape
    rd = cache_krt.shape[2]
    hp = qabs.shape[1]
    nk = wkt.shape[0]
    assert npg % (2 * ppsub) == 0 and kl == LANES
    cur = lambda bi, pt: (jnp.minimum(bi, nb - 1), 0, 0)
    grid_spec = pltpu.PrefetchScalarGridSpec(
        num_scalar_prefetch=1,
        grid=(nb + 1,),
        in_specs=[pl.BlockSpec(memory_space=pl.ANY),
                  pl.BlockSpec(memory_space=pl.ANY),
                  pl.BlockSpec((nk, kl), lambda bi, pt: (0, 0)),
                  pl.BlockSpec((None, hp, kl), cur),
                  pl.BlockSpec((None, hp, rd), cur),
                  pl.BlockSpec((None, 1, kl), cur),
                  pl.BlockSpec((None, rd, 1), cur)],
        out_specs=pl.BlockSpec((None, hp, kl), lambda bi, pt: (jnp.maximum(bi - 1, 0), 0, 0)),
        scratch_shapes=[pltpu.VMEM((2, npg, page, kl), f32), pltpu.VMEM((2, npg, rd, page), f32),
                        pltpu.VMEM((nk + hp, kl), bf16), pltpu.VMEM((2, (ppsub + 1) * page, kl), bf16),
                        pltpu.VMEM((2, hp, (ppsub + 1) * page), f32), pltpu.VMEM((3, hp, LANES), f32),
                        pltpu.SemaphoreType.DMA((2, 2))])
    return pl.pallas_call(
        functools.partial(_sattn_kernel, layer=layer, n_heads=n_heads, ppsub=ppsub),
        grid_spec=grid_spec,
        out_shape=jax.ShapeDtypeStruct((nb, hp, kl), f32),
        compiler_params=pltpu.CompilerParams(dimension_semantics=("arbitrary",), vmem_limit_bytes=VMEM_LIMIT),
        name="sample_attn",
    )(page_table.reshape(-1), cache_lat, cache_krt, wkt, qabs, qr, latn, krn)


def _spost_kernel(x_ref, lru_ref, olat_ref, g1_ref, sh2_ref, gs2_ref, g2_ref, mog_ref, wv_ref, wo_ref,
                  wup_ref, fcw_ref, fcb_ref, wdn_ref, fbuf_ref, y_o, up_o, *, n_heads):
    nc = fcw_ref.shape[0] // 2
    ck = fcw_ref.shape[2]
    kconv = fcw_ref.shape[1]
    heads = [_dot(olat_ref[hd].astype(bf16), wv_ref[:, hd * LANES:(hd + 1) * LANES]) for hd in range(n_heads)]
    attn = jnp.concatenate([heads[2 * pp] + heads[2 * pp + 1] for pp in range(n_heads // 2)], axis=-1)
    attn = _rms(attn, mog_ref[...]).astype(bf16)
    x1, h2 = _mix_and_norm2(x_ref[...], lru_ref[...], attn, wo_ref, g1_ref[...], gs2_ref[...], sh2_ref[...])

    def conv(up, c):
        out = fcb_ref[c]
        cw = fcw_ref[c]
        for j in range(kconv - 1):
            out = out + fbuf_ref[j, :, c * ck:(c + 1) * ck] * cw[j:j + 1, :]
        up_o[:, c * ck:(c + 1) * ck] = up
        return out + up * cw[kconv - 1:kconv, :]

    acc = jnp.zeros(x1.shape, f32)
    for c in range(nc):
        val = conv(_dot(h2, _up_cols(wup_ref, c, ck)), c)
        gt = conv(_dot(h2, _up_cols(wup_ref, nc + c, ck)), nc + c)
        acc = acc + _dot((_gelu_x2(gt) * val).astype(bf16), wdn_ref[c * ck:(c + 1) * ck, :])
    y_o[...] = x1 + g2_ref[...] * acc


def _spost_call(x, lru, olat, g1, sh2, gs2, g2, fbuf, p, n_heads):
    nb, d = x.shape
    return pl.pallas_call(
        functools.partial(_spost_kernel, n_heads=n_heads),
        out_shape=[jax.ShapeDtypeStruct((nb, d), f32), jax.ShapeDtypeStruct((nb, p["wup"].shape[1]), f32)],
        compiler_params=pltpu.CompilerParams(vmem_limit_bytes=VMEM_LIMIT),
        name="sample_ffn",
    )(x, lru, olat, g1, sh2, gs2, g2, p["mog"], p["wv"], p["wo"], p["wup"], p["fcw"], p["fcb"], p["wdn"], fbuf)


def _take_cols(wmat, idx):
    padded = jnp.concatenate([wmat, jnp.zeros(wmat.shape[:-1] + (1,), wmat.dtype)], axis=-1)
    return jnp.take(padded, jnp.asarray(np.where(idx < 0, wmat.shape[-1], idx)), axis=-1)


def _block_diag_groups(wh):
    nh, hd, _ = wh.shape
    per = MXU_DIM // hd
    groups = []
    for g in range(nh // per):
        blk = jnp.zeros((MXU_DIM, MXU_DIM), wh.dtype)
        for j in range(per):
            blk = lax.dynamic_update_slice(blk, wh[g * per + j], (j * hd, j * hd))
        groups.append(blk)
    return jnp.stack(groups).astype(bf16)


def _prep_layer(l, n_heads, scale, w_in, lru_conv_w, lru_conv_b, lru_w_a, lru_b_a, lru_w_x, lru_b_x, lru_lambda,
                norm1_g, q_lora_norm_g, w_uq, q_nope_norm_g, q_rope_norm_g, kv_lora_norm_g, k_rope_norm_g, w_ukv,
                k_nope_norm_g, lru_out_norm_g, mla_out_norm_g, w_o, norm2_g, w_up, ffn_conv_w, ffn_conv_b, w_down):
    src = _h128_src()
    w = lru_lambda.shape[-1]
    ql = q_lora_norm_g.shape[-1]
    kl = kv_lora_norm_g.shape[-1]
    qk = _NOPE + _ROPE
    vd = w_ukv.shape[-1] // n_heads - _NOPE
    row = lambda v: v.reshape(1, -1).astype(f32)

    kr_src = np.where(src >= _NOPE, src - _NOPE, -1)
    win = w_in[l]
    base = 2 * w + ql + kl
    win_ext = jnp.concatenate([win[:, :base], _take_cols(win[:, base:], kr_src)], axis=-1).astype(bf16)

    q_idx = np.concatenate([np.where(src >= 0, src + h * qk, -1) for h in range(n_heads)])
    wuq = _take_cols(w_uq[l], q_idx).astype(bf16)
    k_src = np.where((src >= 0) & (src < _NOPE), src, -1)
    k_idx = np.concatenate([np.where(k_src >= 0, k_src + h * (_NOPE + vd), -1) for h in range(n_heads)])
    wk = _take_cols(w_ukv[l], k_idx).astype(bf16)
    v_idx = []
    vone = np.zeros((1, n_heads * LANES), np.float32)
    for h in range(n_heads):
        slab = np.full((LANES,), -1, np.int64)
        off = (h % 2) * vd
        slab[off:off + vd] = h * (_NOPE + vd) + _NOPE + np.arange(vd)
        v_idx.append(slab)
        vone[0, h * LANES + (vd - off)] = 1.0
    wv = _take_cols(w_ukv[l], np.concatenate(v_idx)).astype(bf16)
    wkt_idx = np.concatenate([h * (_NOPE + vd) + np.arange(_NOPE) for h in range(n_heads)])
    wkt = jnp.take(w_ukv[l], jnp.asarray(wkt_idx), axis=-1).T.astype(bf16)

    nope_tab = lambda g: _take_cols(g.reshape(1, -1), k_src)
    rope_tab = lambda g: _take_cols(g.reshape(1, -1), kr_src)
    is_n = (k_src >= 0).astype(np.float32)
    is_r = (kr_src >= 0).astype(np.float32)
    msel = jnp.asarray(np.outer(is_n, is_n) / _NOPE + np.outer(is_r, is_r) / _ROPE).astype(bf16)
    gk_full = nope_tab(k_nope_norm_g[l]) + jnp.asarray(1.0 - is_n).reshape(1, LANES)
    gq = (nope_tab(q_nope_norm_g[l]) + rope_tab(q_rope_norm_g[l])) * scale * gk_full

    dff = w_down.shape[1]
    ck = MXU_DIM
    nc = dff // ck
    half_val = jnp.concatenate([jnp.full((nc, 1, 1), 0.5, f32), jnp.ones((nc, 1, 1), f32)], axis=0)
    fcw = ffn_conv_w[l].reshape(-1, 2 * nc, ck).transpose(1, 0, 2).astype(f32) * half_val
    fcb = ffn_conv_b[l].reshape(2 * nc, 1, ck).astype(f32) * half_val
    return dict(
        n1g=row(norm1_g[l]), win=win_ext, cw=lru_conv_w[l].astype(f32), cb=row(lru_conv_b[l]),
        wa=_block_diag_groups(lru_w_a[l]), ba=row(lru_b_a[l]), wx=_block_diag_groups(lru_w_x[l]), bx=row(lru_b_x[l]),
        lam=row(lru_lambda[l]), qlg=row(q_lora_norm_g[l]), wuq=wuq, gq=gq.astype(f32), msel=msel,
        kvg=row(kv_lora_norm_g[l]), gkr=rope_tab(k_rope_norm_g[l]).astype(f32), wk=wk,
        wv=wv, vone=jnp.asarray(vone), wkt=wkt, log=row(lru_out_norm_g[l]),
        mog=row(mla_out_norm_g[l]), wo=w_o[l].astype(bf16), n2g=row(norm2_g[l]), wup=w_up[l].astype(bf16),
        fcw=fcw, fcb=fcb, wdn=w_down[l].astype(bf16))


def _rope_tables(pos):
    half = _ROPE // 2
    inv = ROPE_THETA ** (-np.arange(0, _ROPE, 2, dtype=np.float64) / _ROPE)
    ang = np.asarray(pos, np.float64)[:, None] * inv[None, :]
    cos, sin = np.cos(ang), np.sin(ang)
    n = ang.shape[0]
    cc = np.zeros((n, LANES), np.float32)
    cc[:, 0:_R1_LO] = 1.0
    cc[:, _R1_LO + half:_R1_LO + half + 16] = 1.0
    cc[:, _R1_LO:_R1_LO + half] = cos
    cc[:, _R2_LO:_R2_LO + half] = cos
    ss = np.zeros((n, LANES), np.float32)
    ss[:, _R1_LO:_R1_LO + half] = -sin
    ss[:, _R2_LO:_R2_LO + half] = sin
    return jnp.asarray(cc), jnp.asarray(ss)


def _gain_rope_tables(cc, ss, gq, gkr):
    swap = lambda g: jnp.roll(g, LANES // 2, axis=-1)
    return cc * gq, ss * swap(gq), cc * gkr, ss * swap(gkr)


def _tiles(s, npg):
    ts = 64 if s % 64 == 0 else 32
    tq = min(2048, s)
    tk = min(MXU_DIM, s)
    tm = min(512, s)
    ppsub = 16 if npg % 32 == 0 else 1
    assert s % ts == 0 and s % tq == 0 and tq % tk == 0 and s % tm == 0
    return ts, tq, tk, tm, ppsub


def kernel(x_prompt, x_sample, cache_kv_latent, cache_k_rope, state_lru_h, state_lru_conv, state_ffn_conv,
           page_table, c_prompt, c_sample, w_ada, b_ada, norm1_g, w_in, lru_conv_w, lru_conv_b, lru_w_a, lru_b_a,
           lru_w_x, lru_b_x, lru_lambda, q_lora_norm_g, w_uq, q_nope_norm_g, q_rope_norm_g, kv_lora_norm_g,
           k_rope_norm_g, w_ukv, k_nope_norm_g, lru_out_norm_g, mla_out_norm_g, w_o, norm2_g, w_up, ffn_conv_w,
           ffn_conv_b, w_down):
    b, s, d = x_prompt.shape
    nb, ds, _ = x_sample.shape
    depth = w_in.shape[0]
    assert ds == 1 and q_nope_norm_g.shape[-1] == _NOPE and q_rope_norm_g.shape[-1] == _ROPE
    n_heads = w_uq.shape[-1] // (_NOPE + _ROPE)
    scale = float(_NOPE + _ROPE) ** -0.5 * float(np.log2(np.e))
    npg = page_table.shape[1]
    n_past = npg * cache_kv_latent.shape[2]
    ts, tq, tk, tm, ppsub = _tiles(s, npg)
    hp = BF16_ROWS

    cache_krt = jnp.swapaxes(cache_k_rope, 2, 3)
    cc_p, ss_p = _rope_tables(np.arange(s))
    cc_s, ss_s = _rope_tables(n_past + np.arange(1))

    y_p = x_prompt
    y_s = x_sample.reshape(nb, d)
    c_all = jnp.concatenate([c_prompt, c_sample], axis=0)
    outs_p = [[] for _ in range(5)]
    outs_s = [[] for _ in range(5)]
    for l in range(depth):
        p = _prep_layer(l, n_heads, scale, w_in, lru_conv_w, lru_conv_b, lru_w_a, lru_b_a, lru_w_x, lru_b_x,
                        lru_lambda, norm1_g, q_lora_norm_g, w_uq, q_nope_norm_g, q_rope_norm_g, kv_lora_norm_g,
                        k_rope_norm_g, w_ukv, k_nope_norm_g, lru_out_norm_g, mla_out_norm_g, w_o, norm2_g, w_up,
                        ffn_conv_w, ffn_conv_b, w_down)
        mod = _mod_call(c_all, w_ada[l], b_ada[l])
        sh1, sc1, g1, sh2, sc2, g2 = jnp.split(mod, 6, axis=-1)
        mods = [sh1, p["n1g"] * (1.0 + sc1), g1, sh2, p["n2g"] * (1.0 + sc2), g2]
        mp = [m_[:b, None, :] for m_ in mods]
        ms = [m_[b:] for m_ in mods]

        lru_p, q_p, k_p, v_p, kvlat_p, krope_p, hlast_p, xtail_p = _inproj_call(
            y_p, mp[0], mp[1], _gain_rope_tables(cc_p, ss_p, p["gq"], p["gkr"]), p, n_heads, ts)
        attn_p = _attn_call(q_p, k_p, v_p, p["mog"], n_heads, tq, tk)
        y_p, ftail_p = _ffn_call(y_p, lru_p, attn_p, mp[2], mp[3], mp[4], mp[5], p, tm)
        kc = lru_conv_w.shape[1]
        fk = ffn_conv_w.shape[1]
        for j, o in enumerate((kvlat_p, krope_p, hlast_p, xtail_p[:, 8 - (kc - 1):], ftail_p[:, 8 - (fk - 1):])):
            outs_p[j].append(o)

        cbuf = jnp.swapaxes(state_lru_conv[l], 0, 1)
        fbuf = jnp.swapaxes(state_ffn_conv[l], 0, 1)
        lru_s, xlru_s, hnew_s, qabs, qr, kvn_s, kr_s = _spre_call(
            y_s, ms[0], ms[1], _gain_rope_tables(cc_s, ss_s, p["gq"], p["gkr"]), cbuf, state_lru_h[l], p, n_heads,
            n_past == 0)
        pad_heads = lambda t: jnp.pad(jnp.swapaxes(t, 0, 1), ((0, 0), (0, hp - n_heads), (0, 0)))
        olat = _sattn_call(page_table, cache_kv_latent, cache_krt, l, p["wkt"], pad_heads(qabs), pad_heads(qr),
                           kvn_s[:, None, :], kr_s[:, :, None], n_heads, ppsub)
        olat = jnp.swapaxes(olat[:, :n_heads], 0, 1)
        y_s, up_s = _spost_call(y_s, lru_s, olat, ms[2], ms[3], ms[4], ms[5], fbuf, p, n_heads)
        lru_conv_new = jnp.concatenate([state_lru_conv[l][:, 1:], xlru_s[:, None, :]], axis=1)
        ffn_conv_new = jnp.concatenate([state_ffn_conv[l][:, 1:], up_s[:, None, :]], axis=1)
        for j, o in enumerate((kvn_s[:, None, :], kr_s[:, None, :], hnew_s, lru_conv_new, ffn_conv_new)):
            outs_s[j].append(o)

    return (y_p, y_s.reshape(nb, 1, d), *[jnp.stack(o) for o in outs_p], *[jnp.stack(o) for o in outs_s])
```

```python
import functools

import numpy as np
import jax
import jax.numpy as jnp
from jax import lax
from jax.experimental import pallas as pl
from jax.experimental.pallas import tpu as pltpu

f32 = jnp.float32
bf16 = jnp.bfloat16

EPS = 1e-6
LRU_C = 8.0
ROPE_THETA = 10000.0
LANES = 128
MXU_DIM = 256
BF16_ROWS = 16
VMEM_LIMIT = 56 * 1024 * 1024

_NOPE, _ROPE = 64, 32
_R1_LO, _R2_LO = 48, 112


def _h128_src():
    src = np.full((LANES,), -1, np.int32)
    src[0:48] = np.arange(0, 48)
    src[48:64] = _NOPE + np.arange(0, 16)
    src[64:80] = np.arange(48, 64)
    src[112:128] = _NOPE + 16 + np.arange(0, 16)
    return src


def _dot(a, b):
    return jnp.dot(a, b, preferred_element_type=f32)


def _dot_nt(a, b):
    return lax.dot_general(a, b, (((1,), (1,)), ((), ())), preferred_element_type=f32)


def _rms(x, g):
    ms = jnp.mean(x * x, axis=-1, keepdims=True)
    return x * lax.rsqrt(ms + EPS) * g


_GELU_C0 = float(np.sqrt(2.0 / np.pi))
_GELU_C1 = _GELU_C0 * 0.044715


def _gelu_x2(x):
    return x * (1.0 + jnp.tanh(x * (_GELU_C0 + _GELU_C1 * (x * x))))


def _neg_expm1_2x(y):
    t = jnp.tanh(y)
    return -2.0 * t / (1.0 - t)


def _lru_gates(x_conv, wa_ref, ba, wx_ref, bx, lam):
    xb = x_conv.astype(bf16)
    ng = wa_ref.shape[0]
    ra = jnp.concatenate([_dot(xb[:, g * MXU_DIM:(g + 1) * MXU_DIM], wa_ref[g]) for g in range(ng)], axis=-1) + ba
    ia = jnp.concatenate([_dot(xb[:, g * MXU_DIM:(g + 1) * MXU_DIM], wx_ref[g]) for g in range(ng)], axis=-1) + bx
    r = jax.nn.sigmoid(ra)
    ig = jax.nn.sigmoid(ia)
    log_a = r * ((-LRU_C) * jax.nn.softplus(-lam))
    a = jnp.exp(log_a)
    mult = jnp.sqrt(_neg_expm1_2x(log_a))
    return a, mult, ig * x_conv


def _group_ms(x, msel):
    return _dot((x * x).astype(bf16), msel)


def _norm_rope(x, msel, ca, sb, to3d):
    inv = lax.rsqrt(_group_ms(x, msel) + EPS)
    return to3d(inv) * (to3d(x) * ca + to3d(pltpu.roll(x, LANES // 2, 1)) * sb)


def _q_heads(q, msel, ca, sb, n_heads, to3d):
    return [_norm_rope(q[:, h * LANES:(h + 1) * LANES], msel, ca, sb, to3d) for h in range(n_heads)]


def _mod_kernel(c_ref, w_ref, b_ref, o_ref):
    c = c_ref[...]
    sc = (c * jax.nn.sigmoid(c)).astype(bf16)
    o_ref[...] = _dot(sc, w_ref[...].astype(bf16)) + b_ref[...]


def _mod_call(c_all, w_ada, b_ada):
    m, d = c_all.shape
    n = w_ada.shape[1]
    tn = n // 4 if n % (4 * LANES) == 0 else n
    return pl.pallas_call(
        _mod_kernel,
        grid=(n // tn,),
        in_specs=[pl.BlockSpec((m, d), lambda j: (0, 0)),
                  pl.BlockSpec((d, tn), lambda j: (0, j)),
                  pl.BlockSpec((1, tn), lambda j: (0, j))],
        out_specs=pl.BlockSpec((m, tn), lambda j: (0, j)),
        out_shape=jax.ShapeDtypeStruct((m, n), f32),
        compiler_params=pltpu.CompilerParams(dimension_semantics=("arbitrary",), vmem_limit_bytes=VMEM_LIMIT),
        name="adaln_mod",
    )(c_all, w_ada, b_ada.reshape(1, n))


def _inproj_kernel(x_ref, sh_ref, sc_ref, qca_ref, qsb_ref, kca_ref, ksb_ref, win_ref, cw_ref, cb_ref,
                   wa_ref, ba_ref, wx_ref, bx_ref, lam_ref, qlg_ref, wuq_ref, msel_ref,
                   kvg_ref, wk_ref, wv_ref, vone_ref, log_ref,
                   lru_o, q_o, k_o, v_o, kvlat_o, krope_o, hlast_o, xtail_o,
                   xtail_s, a_s, u_s, hs_s, h_s, *, n_heads):
    i = pl.program_id(0)
    nb, ts, d = x_ref.shape
    m = nb * ts
    w = lam_ref.shape[-1]
    ql = qlg_ref.shape[-1]
    kl = kvg_ref.shape[-1]
    kconv = cw_ref.shape[0]
    nlc = w // LANES

    @pl.when(i == 0)
    def _():
        xtail_s[...] = jnp.zeros_like(xtail_s)
        h_s[...] = jnp.zeros_like(h_s)

    def to3d(v):
        return v.reshape(nb, ts, v.shape[-1])

    x = x_ref[...]
    h = _rms(x, sc_ref[...]) + sh_ref[...]
    z = _dot(h.reshape(m, d).astype(bf16), win_ref[...])
    x_lru = z[:, 0:w]
    g_lru = z[:, w:2 * w]
    q_lat = z[:, 2 * w:2 * w + ql]
    kv_lat = z[:, 2 * w + ql:2 * w + ql + kl]
    kr_pre = z[:, 2 * w + ql + kl:2 * w + ql + kl + LANES]

    t_idx = lax.broadcasted_iota(jnp.int32, (m, 1), 0) & (ts - 1)
    r8 = lax.broadcasted_iota(jnp.int32, (1, 8, 1), 1)
    tail2d = xtail_s[...].reshape(nb * 8, w)
    x_conv = to3d(jnp.broadcast_to(cb_ref[...], (m, w)))
    for j in range(kconv - 1):
        k = kconv - 1 - j
        rolled = to3d(pltpu.roll(x_lru, k, 0))
        prev = pltpu.roll(tail2d, nb * 8 + k - 8, 0).reshape(nb, 8, w)
        sh = jnp.concatenate([jnp.where(r8 < k, prev, rolled[:, 0:8]), rolled[:, 8:]], axis=1)
        x_conv = x_conv + sh * cw_ref[j:j + 1, :]
    x_conv = (x_conv + to3d(x_lru) * cw_ref[kconv - 1:kconv, :]).reshape(m, w)
    xtail_s[...] = to3d(x_lru)[:, ts - 8:, :]
    xtail_o[...] = to3d(x_lru)[:, ts - 8:, :]

    a, mult, ux = _lru_gates(x_conv, wa_ref, ba_ref[...], wx_ref, bx_ref[...], lam_ref[...])
    mult = jnp.where(jnp.logical_and(t_idx == 0, i == 0), 1.0, mult)
    u = mult * ux
    pitch = a_s.shape[1] // nb
    for j in range(nlc):
        for bi in range(nb):
            a_s[j, bi * pitch:bi * pitch + ts, :] = a[bi * ts:(bi + 1) * ts, j * LANES:(j + 1) * LANES]
            u_s[j, bi * pitch:bi * pitch + ts, :] = u[bi * ts:(bi + 1) * ts, j * LANES:(j + 1) * LANES]

    def scan_step(t, hc):
        out = []
        for j in range(nlc):
            hj = a_s[j, pl.ds(t, nb, stride=pitch), :] * hc[j] + u_s[j, pl.ds(t, nb, stride=pitch), :]
            hs_s[j, pl.ds(t, nb, stride=pitch), :] = hj
            out.append(hj)
        return tuple(out)

    hc = lax.fori_loop(0, ts, scan_step, tuple(h_s[j] for j in range(nlc)), unroll=8)
    for j in range(nlc):
        h_s[j] = hc[j]
    hlast_o[...] = jnp.concatenate(list(hc), axis=-1)
    hs = jnp.concatenate(
        [jnp.concatenate([hs_s[j, bi * pitch:bi * pitch + ts, :] for bi in range(nb)], axis=0) for j in range(nlc)],
        axis=-1)
    lru_out = hs * jax.nn.gelu(g_lru)
    lru_o[...] = to3d(_rms(lru_out, log_ref[...]).astype(bf16))

    msel = msel_ref[...]
    qn = _rms(q_lat, qlg_ref[...]).astype(bf16)
    q = _dot(qn, wuq_ref[...])
    for hd, o in enumerate(_q_heads(q, msel, qca_ref[...][None], qsb_ref[...][None], n_heads, to3d)):
        q_o[:, :, hd * LANES:(hd + 1) * LANES] = o.astype(bf16)

    kvn = _rms(kv_lat, kvg_ref[...])
    kvlat_o[...] = to3d(kvn)
    kr = _norm_rope(kr_pre, msel, kca_ref[...][None], ksb_ref[...][None], to3d)
    krope_o[...] = jnp.concatenate([kr[:, :, _R1_LO:_R1_LO + 16], kr[:, :, _R2_LO:_R2_LO + 16]], axis=-1)
    kvb = kvn.astype(bf16)
    kk = _dot(kvb, wk_ref[...])
    v_o[...] = to3d((_dot(kvb, wv_ref[...]) + vone_ref[...]).astype(bf16))
    for hd in range(n_heads):
        kh = kk[:, hd * LANES:(hd + 1) * LANES]
        khn = kh * lax.rsqrt(_group_ms(kh, msel) + EPS)
        k_o[:, :, hd * LANES:(hd + 1) * LANES] = (to3d(khn) + kr).astype(bf16)


def _const_spec(shape):
    nd = len(shape)
    return pl.BlockSpec(shape, lambda *_: (0,) * nd, pipeline_mode=pl.Buffered(1))


def _inproj_call(x, sh1, sc1, rope_tabs, p, n_heads, ts):
    b, s, d = x.shape
    w = p["lam"].shape[-1]
    kl = p["kvg"].shape[-1]
    hw = n_heads * LANES
    m = b * ts
    consts = [p["win"], p["cw"], p["cb"], p["wa"], p["ba"], p["wx"], p["bx"], p["lam"], p["qlg"],
              p["wuq"], p["msel"], p["kvg"], p["wk"], p["wv"], p["vone"], p["log"]]
    in_specs = ([pl.BlockSpec((b, ts, d), lambda i: (0, i, 0)), _const_spec(sh1.shape), _const_spec(sc1.shape)]
                + [pl.BlockSpec((ts, LANES), lambda i: (i, 0))] * len(rope_tabs)
                + [_const_spec(c.shape) for c in consts])

    def tile(n, dt):
        return pl.BlockSpec((b, ts, n), lambda i: (0, i, 0)), jax.ShapeDtypeStruct((b, s, n), dt)

    outs = [tile(w, bf16), tile(hw, bf16), tile(hw, bf16), tile(hw, bf16), tile(kl, f32), tile(_ROPE, f32),
            (pl.BlockSpec((b, w), lambda i: (0, 0)), jax.ShapeDtypeStruct((b, w), f32)),
            (pl.BlockSpec((b, 8, w), lambda i: (0, 0, 0)), jax.ShapeDtypeStruct((b, 8, w), f32))]
    nlc = w // LANES
    return pl.pallas_call(
        functools.partial(_inproj_kernel, n_heads=n_heads),
        grid=(s // ts,),
        in_specs=in_specs,
        out_specs=[o[0] for o in outs],
        out_shape=[o[1] for o in outs],
        scratch_shapes=[pltpu.VMEM((b, 8, w), f32)] + [pltpu.VMEM((nlc, b * (ts + 8), LANES), f32)] * 3
        + [pltpu.VMEM((nlc, b, LANES), f32)],
        compiler_params=pltpu.CompilerParams(dimension_semantics=("arbitrary",), vmem_limit_bytes=VMEM_LIMIT),
        name="prompt_inproj",
    )(x, sh1, sc1, *rope_tabs, *consts)


def _attn_kernel(q_ref, k_ref, v_ref, g_ref, o_ref, m_s, acc_s, *, n_heads, tk):
    tq = q_ref.shape[1]
    nmask = tq // tk

    def block(j, r0, first):
        start = j * tk
        vis = (lax.broadcasted_iota(jnp.int32, (tq - r0, tk), 1)
               <= lax.broadcasted_iota(jnp.int32, (tq - r0, tk), 0))
        for hd in range(n_heads):
            hs = slice(hd * LANES, (hd + 1) * LANES)
            s = _dot_nt(q_ref[0, r0:, hs], k_ref[0, pl.ds(start, tk), hs])
            s = jnp.where(vis, s, -1e30)
            smax = jnp.max(s, axis=-1, keepdims=True)
            m_new = jnp.broadcast_to(smax, (tq - r0, LANES)) if first else jnp.maximum(m_s[hd, r0:, :], smax)
            pm = jnp.exp2(s - jnp.concatenate([m_new] * (tk // LANES), axis=-1))
            pv = _dot(pm.astype(bf16), v_ref[0, pl.ds(start, tk), hs])
            if first:
                acc_s[hd, r0:, :] = pv
            else:
                acc_s[hd, r0:, :] = jnp.exp2(m_s[hd, r0:, :] - m_new) * acc_s[hd, r0:, :] + pv
            m_s[hd, r0:, :] = m_new

    for jm in range(nmask):
        block(jm, jm * tk, first=(jm == 0))
    lane = lax.broadcasted_iota(jnp.int32, (1, LANES), 1)
    half = LANES // 2
    heads = []
    for hd in range(n_heads):
        acc = acc_s[hd]
        lo = (hd % 2) * half
        one_lane = (half - lo)
        l = jnp.sum(jnp.where(lane == one_lane, acc, 0.0), axis=-1, keepdims=True)
        heads.append(jnp.where((lane >= lo) & (lane < lo + half), acc, 0.0) / l)
    o = jnp.concatenate([heads[2 * pp] + heads[2 * pp + 1] for pp in range(n_heads // 2)], axis=-1)
    o_ref[0] = _rms(o, g_ref[...]).astype(bf16)


def _attn_call(q, k, v, g, n_heads, tk):
    b, s, hw = q.shape
    wout = g.shape[-1]
    seq = pl.BlockSpec((1, s, hw), lambda bi: (bi, 0, 0))
    return pl.pallas_call(
        functools.partial(_attn_kernel, n_heads=n_heads, tk=tk),
        grid=(b,),
        in_specs=[seq, seq, seq, pl.BlockSpec((1, wout), lambda bi: (0, 0))],
        out_specs=pl.BlockSpec((1, s, wout), lambda bi: (bi, 0, 0)),
        out_shape=jax.ShapeDtypeStruct((b, s, wout), bf16),
        scratch_shapes=[pltpu.VMEM((n_heads, s, LANES), f32), pltpu.VMEM((n_heads, s, LANES), f32)],
        compiler_params=pltpu.CompilerParams(dimension_semantics=("arbitrary",), vmem_limit_bytes=VMEM_LIMIT),
        name="prompt_attn",
    )(q, k, v, g)


def _mix_and_norm2(x, lru, attn, wo_ref, g1, gs2, sh2):
    w = lru.shape[-1]
    mixed = _dot(lru, wo_ref[pl.ds(0, w), :]) + _dot(attn, wo_ref[pl.ds(w, attn.shape[-1]), :])
    x1 = x + g1 * mixed
    h2 = (_rms(x1, gs2) + sh2).astype(bf16)
    return x1, h2


def _up_cols(wup_ref, c, ck):
    return wup_ref[:, c * ck:(c + 1) * ck]


def _ffn_kernel(x_ref, lru_ref, attn_ref, g1_ref, sh2_ref, gs2_ref, g2_ref, wo_ref, wup_ref,
                fcw_ref, fcb_ref, wdn_ref, y_o, tail_o, tail_s, act_s):
    si = pl.program_id(1)
    tm = x_ref.shape[1]
    nc = fcw_ref.shape[0] // 2
    ck = fcw_ref.shape[2]
    kconv = fcw_ref.shape[1]

    @pl.when(si == 0)
    def _():
        tail_s[...] = jnp.zeros_like(tail_s)

    x1, h2 = _mix_and_norm2(x_ref[0], lru_ref[0], attn_ref[0], wo_ref, g1_ref[0], gs2_ref[0], sh2_ref[0])
    r8 = lax.broadcasted_iota(jnp.int32, (8, 1), 0)

    def conv(up, c):
        tail = tail_s[c]
        first = up[0:8]
        out = fcb_ref[c]
        cw = fcw_ref[c]
        for j in range(kconv - 1):
            k = kconv - 1 - j
            head = jnp.where(r8 < k, pltpu.roll(tail, k, 0), pltpu.roll(first, k, 0))
            sh = jnp.concatenate([head, pltpu.roll(up, k, 0)[8:]], axis=0)
            out = out + sh * cw[j:j + 1, :]
        out = out + up * cw[kconv - 1:kconv, :]
        tail_s[c] = up[tm - 8:]
        tail_o[0, :, c * ck:(c + 1) * ck] = up[tm - 8:]
        return out

    for c in range(nc):
        val = conv(_dot(h2, _up_cols(wup_ref, c, ck)), c)
        gt = conv(_dot(h2, _up_cols(wup_ref, nc + c, ck)), nc + c)
        act_s[:, c * ck:(c + 1) * ck] = (_gelu_x2(gt) * val).astype(bf16)
    acc = _dot(act_s[...], wdn_ref[...])
    y_o[0] = x1 + g2_ref[0] * acc


def _ffn_call(x, lru, attn, g1, sh2, gs2, g2, p, tm):
    b, s, d = x.shape
    w = lru.shape[-1]
    nc2, _, ck = p["fcw"].shape
    nc = nc2 // 2
    row = lambda n: pl.BlockSpec((1, tm, n), lambda bi, si: (bi, si, 0))
    modspec = pl.BlockSpec((1, 1, d), lambda bi, si: (bi, 0, 0))
    consts = [p["wo"], p["wup"], p["fcw"], p["fcb"], p["wdn"]]
    return pl.pallas_call(
        _ffn_kernel,
        grid=(b, s // tm),
        in_specs=[row(d), row(w), row(attn.shape[-1]), modspec, modspec, modspec, modspec]
        + [_const_spec(c.shape) for c in consts],
        out_specs=[row(d), pl.BlockSpec((1, 8, 2 * nc * ck), lambda bi, si: (bi, 0, 0))],
        out_shape=[jax.ShapeDtypeStruct((b, s, d), f32), jax.ShapeDtypeStruct((b, 8, 2 * nc * ck), f32)],
        scratch_shapes=[pltpu.VMEM((2 * nc, 8, ck), f32), pltpu.VMEM((tm, nc * ck), bf16)],
        compiler_params=pltpu.CompilerParams(dimension_semantics=("arbitrary", "arbitrary"),
                                             vmem_limit_bytes=VMEM_LIMIT),
        name="prompt_ffn",
    )(x, lru, attn, g1, sh2, gs2, g2, *consts)


def _spre_kernel(x_ref, sh_ref, sc_ref, qca_ref, qsb_ref, kca_ref, ksb_ref, win_ref, cbuf_ref, cw_ref,
                 cb_ref, wa_ref, ba_ref, wx_ref, bx_ref, lam_ref, h0_ref, qlg_ref, wuq_ref, msel_ref,
                 kvg_ref, wk_ref, log_ref,
                 lru_o, xlru_o, hnew_o, qabs_o, qr_o, kvn_o, kr_o, *, n_heads, first_pos):
    w = lam_ref.shape[-1]
    ql = qlg_ref.shape[-1]
    kl = kvg_ref.shape[-1]
    kconv = cw_ref.shape[0]
    ident = lambda v: v

    h = _rms(x_ref[...], sc_ref[...]) + sh_ref[...]
    z = _dot(h.astype(bf16), win_ref[...])
    x_lru = z[:, 0:w]
    g_lru = z[:, w:2 * w]
    q_lat = z[:, 2 * w:2 * w + ql]
    kv_lat = z[:, 2 * w + ql:2 * w + ql + kl]
    kr_pre = z[:, 2 * w + ql + kl:2 * w + ql + kl + LANES]

    x_conv = cb_ref[...]
    for j in range(kconv - 1):
        x_conv = x_conv + cbuf_ref[j] * cw_ref[j:j + 1, :]
    x_conv = x_conv + x_lru * cw_ref[kconv - 1:kconv, :]
    xlru_o[...] = x_lru
    a, mult, ux = _lru_gates(x_conv, wa_ref, ba_ref[...], wx_ref, bx_ref[...], lam_ref[...])
    if first_pos:
        mult = jnp.ones_like(mult)
    hn = a * h0_ref[...] + mult * ux
    hnew_o[...] = hn
    lru_o[...] = _rms(hn * jax.nn.gelu(g_lru), log_ref[...]).astype(bf16)

    msel = msel_ref[...]
    qn = _rms(q_lat, qlg_ref[...]).astype(bf16)
    q = _dot(qn, wuq_ref[...])
    for hd, o in enumerate(_q_heads(q, msel, qca_ref[...], qsb_ref[...], n_heads, ident)):
        wkh = wk_ref[:, hd * LANES:(hd + 1) * LANES]
        qabs_o[hd] = _dot_nt(o.astype(bf16), wkh).astype(bf16)
        qr_o[hd] = jnp.concatenate([o[:, _R1_LO:_R1_LO + 16], o[:, _R2_LO:_R2_LO + 16]], axis=-1).astype(bf16)

    kvn_o[...] = _rms(kv_lat, kvg_ref[...])
    kr = _norm_rope(kr_pre, msel, kca_ref[...], ksb_ref[...], ident)
    kr_o[...] = jnp.concatenate([kr[:, _R1_LO:_R1_LO + 16], kr[:, _R2_LO:_R2_LO + 16]], axis=-1)


def _spre_call(x, sh1, sc1, rope_tabs, cbuf, h0, p, n_heads, first_pos):
    nb, d = x.shape
    w = p["lam"].shape[-1]
    kl = p["kvg"].shape[-1]
    out_shape = [jax.ShapeDtypeStruct((nb, w), bf16), jax.ShapeDtypeStruct((nb, w), f32),
                 jax.ShapeDtypeStruct((nb, w), f32), jax.ShapeDtypeStruct((n_heads, nb, kl), bf16),
                 jax.ShapeDtypeStruct((n_heads, nb, _ROPE), bf16), jax.ShapeDtypeStruct((nb, kl), f32),
                 jax.ShapeDtypeStruct((nb, _ROPE), f32)]
    return pl.pallas_call(
        functools.partial(_spre_kernel, n_heads=n_heads, first_pos=first_pos),
        out_shape=out_shape,
        compiler_params=pltpu.CompilerParams(vmem_limit_bytes=VMEM_LIMIT),
        name="sample_inproj",
    )(x, sh1, sc1, *rope_tabs, p["win"], cbuf, p["cw"], p["cb"], p["wa"], p["ba"], p["wx"], p["bx"],
      p["lam"], h0, p["qlg"], p["wuq"], p["msel"], p["kvg"], p["wk"], p["log"])


def _sattn_kernel(pt_ref, lat_hbm, krt_hbm, wkt_ref, qabs_ref, qr_ref, latn_ref, krn_ref, o_ref,
                  lat_buf, kr_buf, lhs_s, latb_s, s_s, car_s, sems, *, layer, n_heads, ppsub):
    b = pl.program_id(0)
    nseq = pl.num_programs(0) - 1
    _, npg, page, kl = lat_buf.shape
    nk = wkt_ref.shape[0]
    hp = qabs_ref.shape[0]
    tk = ppsub * page
    slot = lax.rem(b, 2)

    def page_copies(src_page, slot_, pg):
        return (pltpu.make_async_copy(lat_hbm.at[layer, src_page], lat_buf.at[slot_, pg], sems.at[0, slot_]),
                pltpu.make_async_copy(krt_hbm.at[layer, src_page], kr_buf.at[slot_, pg], sems.at[1, slot_]))

    def start_pages(seq, slot_, pg0, n):
        for i in range(n):
            lat_cp, kr_cp = page_copies(pt_ref[seq * npg + pg0 + i], slot_, pg0 + i)
            kr_cp.start(priority=0)
            lat_cp.start(priority=1)

    @pl.when(b == 0)
    def _():
        def body(pg, c):
            start_pages(0, 0, pg, 1)
            return c
        lax.fori_loop(0, npg, body, 0)
        lhs_s[0:nk, :] = wkt_ref[...]
        latb_s[1] = jnp.zeros(latb_s.shape[1:], bf16)
        s_s[1] = jnp.zeros(s_s.shape[1:], f32)
        car_s[...] = jnp.zeros_like(car_s)

    lhs_s[nk:nk + hp, :] = qabs_ref[...]

    def wait_slot(slot_):
        pltpu.make_async_copy(lat_hbm.at[layer, pl.ds(0, npg)], lat_buf.at[slot_], sems.at[0, slot_]).wait()
        pltpu.make_async_copy(krt_hbm.at[layer, pl.ds(0, npg)], kr_buf.at[slot_], sems.at[1, slot_]).wait()

    wait_slot(slot)

    def scores(latb, krt):
        n = latb.shape[0]
        big = _dot_nt(lhs_s[...], latb)
        knt = big[0:nk]
        ssq = jnp.sum((knt * knt).reshape(n_heads, _NOPE, n), axis=1)
        rs = lax.rsqrt(ssq * (1.0 / _NOPE) + EPS)
        sr = _dot(qr_ref[...], krt)
        s8 = big[nk:nk + n_heads] * rs + sr[0:n_heads]
        return jnp.concatenate([s8, jnp.zeros((hp - n_heads, n), f32)], axis=0)

    def softmax_update(carry, parts):
        m_run, l_run, acc = carry
        m_new = m_run
        for _, s in parts:
            m_new = jnp.maximum(m_new, jnp.max(s, axis=-1, keepdims=True))
        alpha = jnp.exp2(m_run - m_new)
        l_new = alpha * l_run
        acc = alpha * acc
        for latb, s in parts:
            pm = jnp.exp2(s - jnp.concatenate([m_new] * (s.shape[-1] // LANES), axis=-1))
            l_new = l_new + jnp.sum(pm, axis=-1, keepdims=True)
            acc = acc + _dot(pm.astype(bf16), latb)
        return m_new, l_new, acc

    nxt = jnp.minimum(b + 1, nseq - 1)
    nsub = npg // ppsub
    spi = 6 if (nsub - 2) % 6 == 0 else 2
    trips = (nsub - 2) // spi

    def request(j):
        if trips != 1:
            start_pages(nxt, 1 - slot, j * ppsub, ppsub)
        elif 2 * j < nsub:
            start_pages(nxt, 1 - slot, 2 * j * ppsub, 2 * ppsub)

    def stage(j, st, with_new_token=False):
        request(j)
        p0 = j * ppsub if isinstance(j, int) else pl.multiple_of(j * ppsub, ppsub)
        lat = [lat_buf[slot, pl.ds(p0, ppsub)].reshape(tk, kl).astype(bf16)]
        krt = [kr_buf[slot, p0 + i].astype(bf16) for i in range(ppsub)]
        if with_new_token:
            lat.append(jnp.broadcast_to(latn_ref[...], (page, kl)).astype(bf16))
            krt.append(jnp.broadcast_to(krn_ref[...], (krn_ref.shape[0], page)).astype(bf16))
        latb = jnp.concatenate(lat, axis=0)
        s = scores(latb, jnp.concatenate(krt, axis=-1))
        n = latb.shape[0]
        if with_new_token:
            s = jnp.where(lax.broadcasted_iota(jnp.int32, s.shape, 1) <= tk, s, -1e30)
        latb_s[st, 0:n, :] = latb
        s_s[st, :, 0:n] = s

    def stashed(st, n=tk):
        return [(latb_s[st, 0:n, :], s_s[st, :, 0:n])]

    stage(0, 0)
    _, l_prev, acc_prev = softmax_update((car_s[0], car_s[1], car_s[2]), stashed(1, tk + page))
    o_ref[...] = acc_prev / l_prev

    def body(k, carry):
        for u in range(0, spi, 2):
            stage(spi * k + u + 1, 1)
            carry = softmax_update(carry, stashed(0))
            stage(spi * k + u + 2, 0)
            carry = softmax_update(carry, stashed(1))
        return carry

    carry = (jnp.full((hp, LANES), -1e30, f32), jnp.zeros((hp, LANES), f32), jnp.zeros((hp, kl), f32))
    carry = body(0, carry) if trips == 1 else lax.fori_loop(0, trips, body, carry)
    stage(nsub - 1, 1, with_new_token=True)
    carry = softmax_update(carry, stashed(0))
    for i in range(3):
        car_s[i] = carry[i]

    @pl.when(b == nseq)
    def _():
        wait_slot(1 - slot)


def _sattn_call(page_table, cache_lat, cache_krt, layer, wkt, qabs, qr, latn, krn, n_heads, ppsub):
    nb, npg = page_table.shape
    _, _, page, kl = cache_lat.shape
    rd = cache_krt.shape[2]
    hp = qabs.shape[1]
    nk = wkt.shape[0]
    assert npg % (2 * ppsub) == 0 and kl == LANES
    cur = lambda bi, pt: (jnp.minimum(bi, nb - 1), 0, 0)
    grid_spec = pltpu.PrefetchScalarGridSpec(
        num_scalar_prefetch=1,
        grid=(nb + 1,),
        in_specs=[pl.BlockSpec(memory_space=pl.ANY),
                  pl.BlockSpec(memory_space=pl.ANY),
                  pl.BlockSpec((nk, kl), lambda bi, pt: (0, 0)),
                  pl.BlockSpec((None, hp, kl), cur),
                  pl.BlockSpec((None, hp, rd), cur),
                  pl.BlockSpec((None, 1, kl), cur),
                  pl.BlockSpec((None, rd, 1), cur)],
        out_specs=pl.BlockSpec((None, hp, kl), lambda bi, pt: (jnp.maximum(bi - 1, 0), 0, 0)),
        scratch_shapes=[pltpu.VMEM((2, npg, page, kl), f32), pltpu.VMEM((2, npg, rd, page), f32),
                        pltpu.VMEM((nk + hp, kl), bf16), pltpu.VMEM((2, (ppsub + 1) * page, kl), bf16),
                        pltpu.VMEM((2, hp, (ppsub + 1) * page), f32), pltpu.VMEM((3, hp, LANES), f32),
                        pltpu.SemaphoreType.DMA((2, 2))])
    return pl.pallas_call(
        functools.partial(_sattn_kernel, layer=layer, n_heads=n_heads, ppsub=ppsub),
        grid_spec=grid_spec,
        out_shape=jax.ShapeDtypeStruct((nb, hp, kl), f32),
        compiler_params=pltpu.CompilerParams(dimension_semantics=("arbitrary",), vmem_limit_bytes=VMEM_LIMIT),
        name="sample_attn",
    )(page_table.reshape(-1), cache_lat, cache_krt, wkt, qabs, qr, latn, krn)


def _spost_kernel(x_ref, lru_ref, olat_ref, g1_ref, sh2_ref, gs2_ref, g2_ref, mog_ref, wv_ref, wo_ref,
                  wup_ref, fcw_ref, fcb_ref, wdn_ref, fbuf_ref, y_o, up_o, *, n_heads):
    nc = fcw_ref.shape[0] // 2
    ck = fcw_ref.shape[2]
    kconv = fcw_ref.shape[1]
    heads = [_dot(olat_ref[hd].astype(bf16), wv_ref[:, hd * LANES:(hd + 1) * LANES]) for hd in range(n_heads)]
    attn = jnp.concatenate([heads[2 * pp] + heads[2 * pp + 1] for pp in range(n_heads // 2)], axis=-1)
    attn = _rms(attn, mog_ref[...]).astype(bf16)
    x1, h2 = _mix_and_norm2(x_ref[...], lru_ref[...], attn, wo_ref, g1_ref[...], gs2_ref[...], sh2_ref[...])

    def conv(up, c):
        out = fcb_ref[c]
        cw = fcw_ref[c]
        for j in range(kconv - 1):
            out = out + fbuf_ref[j, :, c * ck:(c + 1) * ck] * cw[j:j + 1, :]
        up_o[:, c * ck:(c + 1) * ck] = up
        return out + up * cw[kconv - 1:kconv, :]

    acc = jnp.zeros(x1.shape, f32)
    for c in range(nc):
        val = conv(_dot(h2, _up_cols(wup_ref, c, ck)), c)
        gt = conv(_dot(h2, _up_cols(wup_ref, nc + c, ck)), nc + c)
        acc = acc + _dot((_gelu_x2(gt) * val).astype(bf16), wdn_ref[c * ck:(c + 1) * ck, :])
    y_o[...] = x1 + g2_ref[...] * acc


def _spost_call(x, lru, olat, g1, sh2, gs2, g2, fbuf, p, n_heads):
    nb, d = x.shape
    return pl.pallas_call(
        functools.partial(_spost_kernel, n_heads=n_heads),
        out_shape=[jax.ShapeDtypeStruct((nb, d), f32), jax.ShapeDtypeStruct((nb, p["wup"].shape[1]), f32)],
        compiler_params=pltpu.CompilerParams(vmem_limit_bytes=VMEM_LIMIT),
        name="sample_ffn",
    )(x, lru, olat, g1, sh2, gs2, g2, p["mog"], p["wv"], p["wo"], p["wup"], p["fcw"], p["fcb"], p["wdn"], fbuf)


def _take_cols(wmat, idx):
    padded = jnp.concatenate([wmat, jnp.zeros(wmat.shape[:-1] + (1,), wmat.dtype)], axis=-1)
    return jnp.take(padded, jnp.asarray(np.where(idx < 0, wmat.shape[-1], idx)), axis=-1)


def _block_diag_groups(wh):
    nh, hd, _ = wh.shape
    per = MXU_DIM // hd
    grouped = wh.reshape(nh // per, per, hd, 1, hd)
    same = jnp.eye(per, dtype=wh.dtype).reshape(1, per, 1, per, 1)
    return (grouped * same).reshape(nh // per, MXU_DIM, MXU_DIM).astype(bf16)


def _prep_layer(l, n_heads, scale, w_in, lru_conv_w, lru_conv_b, lru_w_a, lru_b_a, lru_w_x, lru_b_x, lru_lambda,
                norm1_g, q_lora_norm_g, w_uq, q_nope_norm_g, q_rope_norm_g, kv_lora_norm_g, k_rope_norm_g, w_ukv,
                k_nope_norm_g, lru_out_norm_g, mla_out_norm_g, w_o, norm2_g, w_up, ffn_conv_w, ffn_conv_b, w_down):
    src = _h128_src()
    w = lru_lambda.shape[-1]
    ql = q_lora_norm_g.shape[-1]
    kl = kv_lora_norm_g.shape[-1]
    qk = _NOPE + _ROPE
    vd = w_ukv.shape[-1] // n_heads - _NOPE
    row = lambda v: v.reshape(1, -1).astype(f32)

    kr_src = np.where(src >= _NOPE, src - _NOPE, -1)
    win = w_in[l]
    base = 2 * w + ql + kl
    win_ext = jnp.concatenate([win[:, :base], _take_cols(win[:, base:], kr_src)], axis=-1).astype(bf16)

    q_idx = np.concatenate([np.where(src >= 0, src + h * qk, -1) for h in range(n_heads)])
    wuq = _take_cols(w_uq[l], q_idx).astype(bf16)
    k_src = np.where((src >= 0) & (src < _NOPE), src, -1)
    k_idx = np.concatenate([np.where(k_src >= 0, k_src + h * (_NOPE + vd), -1) for h in range(n_heads)])
    wk = _take_cols(w_ukv[l], k_idx).astype(bf16)
    v_idx = []
    vone = np.zeros((1, n_heads * LANES), np.float32)
    for h in range(n_heads):
        slab = np.full((LANES,), -1, np.int64)
        off = (h % 2) * vd
        slab[off:off + vd] = h * (_NOPE + vd) + _NOPE + np.arange(vd)
        v_idx.append(slab)
        vone[0, h * LANES + (vd - off)] = 1.0
    wv = _take_cols(w_ukv[l], np.concatenate(v_idx)).astype(bf16)
    wkt_idx = np.concatenate([h * (_NOPE + vd) + np.arange(_NOPE) for h in range(n_heads)])
    wkt = jnp.take(w_ukv[l], jnp.asarray(wkt_idx), axis=-1).T.astype(bf16)

    nope_tab = lambda g: _take_cols(g.reshape(1, -1), k_src)
    rope_tab = lambda g: _take_cols(g.reshape(1, -1), kr_src)
    is_n = (k_src >= 0).astype(np.float32)
    is_r = (kr_src >= 0).astype(np.float32)
    msel = jnp.asarray(np.outer(is_n, is_n) / _NOPE + np.outer(is_r, is_r) / _ROPE).astype(bf16)
    gk_full = nope_tab(k_nope_norm_g[l]) + jnp.asarray(1.0 - is_n).reshape(1, LANES)
    gq = (nope_tab(q_nope_norm_g[l]) + rope_tab(q_rope_norm_g[l])) * scale * gk_full

    dff = w_down.shape[1]
    ck = MXU_DIM
    nc = dff // ck
    half_val = jnp.concatenate([jnp.full((nc, 1, 1), 0.5, f32), jnp.ones((nc, 1, 1), f32)], axis=0)
    fcw = ffn_conv_w[l].reshape(-1, 2 * nc, ck).transpose(1, 0, 2).astype(f32) * half_val
    fcb = ffn_conv_b[l].reshape(2 * nc, 1, ck).astype(f32) * half_val
    return dict(
        n1g=row(norm1_g[l]), win=win_ext, cw=lru_conv_w[l].astype(f32), cb=row(lru_conv_b[l]),
        wa=_block_diag_groups(lru_w_a[l]), ba=row(lru_b_a[l]), wx=_block_diag_groups(lru_w_x[l]), bx=row(lru_b_x[l]),
        lam=row(lru_lambda[l]), qlg=row(q_lora_norm_g[l]), wuq=wuq, gq=gq.astype(f32), msel=msel,
        kvg=row(kv_lora_norm_g[l]), gkr=rope_tab(k_rope_norm_g[l]).astype(f32), wk=wk,
        wv=wv, vone=jnp.asarray(vone), wkt=wkt, log=row(lru_out_norm_g[l]),
        mog=row(mla_out_norm_g[l]), wo=w_o[l].astype(bf16), n2g=row(norm2_g[l]), wup=w_up[l].astype(bf16),
        fcw=fcw, fcb=fcb, wdn=w_down[l].astype(bf16))


def _rope_tables(pos):
    half = _ROPE // 2
    inv = ROPE_THETA ** (-np.arange(0, _ROPE, 2, dtype=np.float64) / _ROPE)
    ang = np.asarray(pos, np.float64)[:, None] * inv[None, :]
    cos, sin = np.cos(ang), np.sin(ang)
    n = ang.shape[0]
    cc = np.zeros((n, LANES), np.float32)
    cc[:, 0:_R1_LO] = 1.0
    cc[:, _R1_LO + half:_R1_LO + half + 16] = 1.0
    cc[:, _R1_LO:_R1_LO + half] = cos
    cc[:, _R2_LO:_R2_LO + half] = cos
    ss = np.zeros((n, LANES), np.float32)
    ss[:, _R1_LO:_R1_LO + half] = -sin
    ss[:, _R2_LO:_R2_LO + half] = sin
    return jnp.asarray(cc), jnp.asarray(ss)


def _gain_rope_tables(cc, ss, gq, gkr):
    swap = lambda g: jnp.roll(g, LANES // 2, axis=-1)
    return cc * gq, ss * swap(gq), cc * gkr, ss * swap(gkr)


def _tiles(s, npg):
    ts = 64 if s % 64 == 0 else 32
    tk = min(MXU_DIM, s)
    tm = min(512, s)
    ppsub = 16 if npg % 32 == 0 else 1
    assert s % ts == 0 and s % tk == 0 and s % tm == 0 and s <= 2048
    return ts, tk, tm, ppsub


def kernel(x_prompt, x_sample, cache_kv_latent, cache_k_rope, state_lru_h, state_lru_conv, state_ffn_conv,
           page_table, c_prompt, c_sample, w_ada, b_ada, norm1_g, w_in, lru_conv_w, lru_conv_b, lru_w_a, lru_b_a,
           lru_w_x, lru_b_x, lru_lambda, q_lora_norm_g, w_uq, q_nope_norm_g, q_rope_norm_g, kv_lora_norm_g,
           k_rope_norm_g, w_ukv, k_nope_norm_g, lru_out_norm_g, mla_out_norm_g, w_o, norm2_g, w_up, ffn_conv_w,
           ffn_conv_b, w_down):
    b, s, d = x_prompt.shape
    nb, ds, _ = x_sample.shape
    depth = w_in.shape[0]
    assert ds == 1 and q_nope_norm_g.shape[-1] == _NOPE and q_rope_norm_g.shape[-1] == _ROPE
    n_heads = w_uq.shape[-1] // (_NOPE + _ROPE)
    scale = float(_NOPE + _ROPE) ** -0.5 * float(np.log2(np.e))
    npg = page_table.shape[1]
    n_past = npg * cache_kv_latent.shape[2]
    ts, tk, tm, ppsub = _tiles(s, npg)
    hp = BF16_ROWS

    cache_krt = jnp.swapaxes(cache_k_rope, 2, 3)
    cc_p, ss_p = _rope_tables(np.arange(s))
    cc_s, ss_s = _rope_tables(n_past + np.arange(1))

    y_p = x_prompt
    y_s = x_sample.reshape(nb, d)
    c_all = jnp.concatenate([c_prompt, c_sample], axis=0)
    outs_p = [[] for _ in range(5)]
    outs_s = [[] for _ in range(5)]
    for l in range(depth):
        p = _prep_layer(l, n_heads, scale, w_in, lru_conv_w, lru_conv_b, lru_w_a, lru_b_a, lru_w_x, lru_b_x,
                        lru_lambda, norm1_g, q_lora_norm_g, w_uq, q_nope_norm_g, q_rope_norm_g, kv_lora_norm_g,
                        k_rope_norm_g, w_ukv, k_nope_norm_g, lru_out_norm_g, mla_out_norm_g, w_o, norm2_g, w_up,
                        ffn_conv_w, ffn_conv_b, w_down)
        mod = _mod_call(c_all, w_ada[l], b_ada[l])
        sh1, sc1, g1, sh2, sc2, g2 = jnp.split(mod, 6, axis=-1)
        mods = [sh1, p["n1g"] * (1.0 + sc1), g1, sh2, p["n2g"] * (1.0 + sc2), g2]
        mp = [m_[:b, None, :] for m_ in mods]
        ms = [m_[b:] for m_ in mods]

        lru_p, q_p, k_p, v_p, kvlat_p, krope_p, hlast_p, xtail_p = _inproj_call(
            y_p, mp[0], mp[1], _gain_rope_tables(cc_p, ss_p, p["gq"], p["gkr"]), p, n_heads, ts)
        attn_p = _attn_call(q_p, k_p, v_p, p["mog"], n_heads, tk)
        y_p, ftail_p = _ffn_call(y_p, lru_p, attn_p, mp[2], mp[3], mp[4], mp[5], p, tm)
        kc = lru_conv_w.shape[1]
        fk = ffn_conv_w.shape[1]
        for j, o in enumerate((kvlat_p, krope_p, hlast_p, xtail_p[:, 8 - (kc - 1):], ftail_p[:, 8 - (fk - 1):])):
            outs_p[j].append(o)

        cbuf = jnp.swapaxes(state_lru_conv[l], 0, 1)
        fbuf = jnp.swapaxes(state_ffn_conv[l], 0, 1)
        lru_s, xlru_s, hnew_s, qabs, qr, kvn_s, kr_s = _spre_call(
            y_s, ms[0], ms[1], _gain_rope_tables(cc_s, ss_s, p["gq"], p["gkr"]), cbuf, state_lru_h[l], p, n_heads,
            n_past == 0)
        pad_heads = lambda t: jnp.pad(jnp.swapaxes(t, 0, 1), ((0, 0), (0, hp - n_heads), (0, 0)))
        olat = _sattn_call(page_table, cache_kv_latent, cache_krt, l, p["wkt"], pad_heads(qabs), pad_heads(qr),
                           kvn_s[:, None, :], kr_s[:, :, None], n_heads, ppsub)
        olat = jnp.swapaxes(olat[:, :n_heads], 0, 1)
        y_s, up_s = _spost_call(y_s, lru_s, olat, ms[2], ms[3], ms[4], ms[5], fbuf, p, n_heads)
        lru_conv_new = jnp.concatenate([state_lru_conv[l][:, 1:], xlru_s[:, None, :]], axis=1)
        ffn_conv_new = jnp.concatenate([state_ffn_conv[l][:, 1:], up_s[:, None, :]], axis=1)
        for j, o in enumerate((kvn_s[:, None, :], kr_s[:, None, :], hnew_s, lru_conv_new, ffn_conv_new)):
            outs_s[j].append(o)

    return (y_p, y_s.reshape(nb, 1, d), *[jnp.stack(o) for o in outs_p], *[jnp.stack(o) for o in outs_s])
```

```python
import functools

import numpy as np
import jax
import jax.numpy as jnp
from jax import lax
from jax.experimental import pallas as pl
from jax.experimental.pallas import tpu as pltpu

f32 = jnp.float32
bf16 = jnp.bfloat16

EPS = 1e-6
LRU_C = 8.0
ROPE_THETA = 10000.0
LANES = 128
MXU_DIM = 256
BF16_ROWS = 16
VMEM_LIMIT = 56 * 1024 * 1024

_NOPE, _ROPE = 64, 32
_R1_LO, _R2_LO = 48, 112


def _h128_src():
    src = np.full((LANES,), -1, np.int32)
    src[0:48] = np.arange(0, 48)
    src[48:64] = _NOPE + np.arange(0, 16)
    src[64:80] = np.arange(48, 64)
    src[112:128] = _NOPE + 16 + np.arange(0, 16)
    return src


def _dot(a, b):
    return jnp.dot(a, b, preferred_element_type=f32)


def _dot_nt(a, b):
    return lax.dot_general(a, b, (((1,), (1,)), ((), ())), preferred_element_type=f32)


def _rms(x, g):
    ms = jnp.mean(x * x, axis=-1, keepdims=True)
    return x * lax.rsqrt(ms + EPS) * g


_GELU_C0 = float(np.sqrt(2.0 / np.pi))
_GELU_C1 = _GELU_C0 * 0.044715


def _gelu_x2(x):
    return x * (1.0 + jnp.tanh(x * (_GELU_C0 + _GELU_C1 * (x * x))))


def _neg_expm1_2x(y):
    t = jnp.tanh(y)
    return -2.0 * t / (1.0 - t)


def _lru_gates(x_conv, wa_ref, ba, wx_ref, bx, lam):
    xb = x_conv.astype(bf16)
    ng = wa_ref.shape[0]
    ra = jnp.concatenate([_dot(xb[:, g * MXU_DIM:(g + 1) * MXU_DIM], wa_ref[g]) for g in range(ng)], axis=-1) + ba
    ia = jnp.concatenate([_dot(xb[:, g * MXU_DIM:(g + 1) * MXU_DIM], wx_ref[g]) for g in range(ng)], axis=-1) + bx
    r = jax.nn.sigmoid(ra)
    ig = jax.nn.sigmoid(ia)
    log_a = r * ((-LRU_C) * jax.nn.softplus(-lam))
    a = jnp.exp(log_a)
    mult = jnp.sqrt(_neg_expm1_2x(log_a))
    return a, mult, ig * x_conv


def _group_ms(x, msel):
    return _dot((x * x).astype(bf16), msel)


def _norm_rope(x, msel, ca, sb, to3d):
    inv = lax.rsqrt(_group_ms(x, msel) + EPS)
    return to3d(inv) * (to3d(x) * ca + to3d(pltpu.roll(x, LANES // 2, 1)) * sb)


def _q_heads(q, msel, ca, sb, n_heads, to3d):
    return [_norm_rope(q[:, h * LANES:(h + 1) * LANES], msel, ca, sb, to3d) for h in range(n_heads)]


def _mod_kernel(c_ref, w_ref, b_ref, o_ref):
    c = c_ref[...]
    sc = (c * jax.nn.sigmoid(c)).astype(bf16)
    o_ref[...] = _dot(sc, w_ref[...].astype(bf16)) + b_ref[...]


def _mod_call(c_all, w_ada, b_ada):
    m, d = c_all.shape
    n = w_ada.shape[1]
    tn = n // 4 if n % (4 * LANES) == 0 else n
    return pl.pallas_call(
        _mod_kernel,
        grid=(n // tn,),
        in_specs=[pl.BlockSpec((m, d), lambda j: (0, 0)),
                  pl.BlockSpec((d, tn), lambda j: (0, j)),
                  pl.BlockSpec((1, tn), lambda j: (0, j))],
        out_specs=pl.BlockSpec((m, tn), lambda j: (0, j)),
        out_shape=jax.ShapeDtypeStruct((m, n), f32),
        compiler_params=pltpu.CompilerParams(dimension_semantics=("arbitrary",), vmem_limit_bytes=VMEM_LIMIT),
        name="adaln_mod",
    )(c_all, w_ada, b_ada.reshape(1, n))


def _inproj_kernel(x_ref, sh_ref, sc_ref, qca_ref, qsb_ref, kca_ref, ksb_ref, win_ref, cw_ref, cb_ref,
                   wa_ref, ba_ref, wx_ref, bx_ref, lam_ref, qlg_ref, wuq_ref, msel_ref,
                   kvg_ref, wk_ref, wv_ref, vone_ref, log_ref,
                   lru_o, q_o, k_o, v_o, kvlat_o, krope_o, hlast_o, xtail_o,
                   xtail_s, a_s, u_s, hs_s, h_s, *, n_heads):
    i = pl.program_id(0)
    nb, ts, d = x_ref.shape
    m = nb * ts
    w = lam_ref.shape[-1]
    ql = qlg_ref.shape[-1]
    kl = kvg_ref.shape[-1]
    kconv = cw_ref.shape[0]
    nlc = w // LANES

    @pl.when(i == 0)
    def _():
        xtail_s[...] = jnp.zeros_like(xtail_s)
        h_s[...] = jnp.zeros_like(h_s)

    def to3d(v):
        return v.reshape(nb, ts, v.shape[-1])

    x = x_ref[...]
    h = _rms(x, sc_ref[...]) + sh_ref[...]
    z = _dot(h.reshape(m, d).astype(bf16), win_ref[...])
    x_lru = z[:, 0:w]
    g_lru = z[:, w:2 * w]
    q_lat = z[:, 2 * w:2 * w + ql]
    kv_lat = z[:, 2 * w + ql:2 * w + ql + kl]
    kr_pre = z[:, 2 * w + ql + kl:2 * w + ql + kl + LANES]

    t_idx = lax.broadcasted_iota(jnp.int32, (m, 1), 0) & (ts - 1)
    r8 = lax.broadcasted_iota(jnp.int32, (1, 8, 1), 1)
    tail2d = xtail_s[...].reshape(nb * 8, w)
    x_conv = to3d(jnp.broadcast_to(cb_ref[...], (m, w)))
    for j in range(kconv - 1):
        k = kconv - 1 - j
        rolled = to3d(pltpu.roll(x_lru, k, 0))
        prev = pltpu.roll(tail2d, nb * 8 + k - 8, 0).reshape(nb, 8, w)
        sh = jnp.concatenate([jnp.where(r8 < k, prev, rolled[:, 0:8]), rolled[:, 8:]], axis=1)
        x_conv = x_conv + sh * cw_ref[j:j + 1, :]
    x_conv = (x_conv + to3d(x_lru) * cw_ref[kconv - 1:kconv, :]).reshape(m, w)
    xtail_s[...] = to3d(x_lru)[:, ts - 8:, :]
    xtail_o[...] = to3d(x_lru)[:, ts - 8:, :]

    a, mult, ux = _lru_gates(x_conv, wa_ref, ba_ref[...], wx_ref, bx_ref[...], lam_ref[...])
    mult = jnp.where(jnp.logical_and(t_idx == 0, i == 0), 1.0, mult)
    u = mult * ux
    pitch = a_s.shape[1] // nb
    for j in range(nlc):
        for bi in range(nb):
            a_s[j, bi * pitch:bi * pitch + ts, :] = a[bi * ts:(bi + 1) * ts, j * LANES:(j + 1) * LANES]
            u_s[j, bi * pitch:bi * pitch + ts, :] = u[bi * ts:(bi + 1) * ts, j * LANES:(j + 1) * LANES]

    def scan_step(t, hc):
        out = []
        for j in range(nlc):
            hj = a_s[j, pl.ds(t, nb, stride=pitch), :] * hc[j] + u_s[j, pl.ds(t, nb, stride=pitch), :]
            hs_s[j, pl.ds(t, nb, stride=pitch), :] = hj
            out.append(hj)
        return tuple(out)

    hc = lax.fori_loop(0, ts, scan_step, tuple(h_s[j] for j in range(nlc)), unroll=8)
    for j in range(nlc):
        h_s[j] = hc[j]
    hlast_o[...] = jnp.concatenate(list(hc), axis=-1)
    hs = jnp.concatenate(
        [jnp.concatenate([hs_s[j, bi * pitch:bi * pitch + ts, :] for bi in range(nb)], axis=0) for j in range(nlc)],
        axis=-1)
    lru_out = hs * jax.nn.gelu(g_lru)
    lru_o[...] = to3d(_rms(lru_out, log_ref[...]).astype(bf16))

    msel = msel_ref[...]
    qn = _rms(q_lat, qlg_ref[...]).astype(bf16)
    q = _dot(qn, wuq_ref[...])
    for hd, o in enumerate(_q_heads(q, msel, qca_ref[...][None], qsb_ref[...][None], n_heads, to3d)):
        q_o[:, :, hd * LANES:(hd + 1) * LANES] = o.astype(bf16)

    kvn = _rms(kv_lat, kvg_ref[...])
    kvlat_o[...] = to3d(kvn)
    kr = _norm_rope(kr_pre, msel, kca_ref[...][None], ksb_ref[...][None], to3d)
    krope_o[...] = jnp.concatenate([kr[:, :, _R1_LO:_R1_LO + 16], kr[:, :, _R2_LO:_R2_LO + 16]], axis=-1)
    kvb = kvn.astype(bf16)
    kk = _dot(kvb, wk_ref[...])
    v_o[...] = to3d((_dot(kvb, wv_ref[...]) + vone_ref[...]).astype(bf16))
    for hd in range(n_heads):
        kh = kk[:, hd * LANES:(hd + 1) * LANES]
        khn = kh * lax.rsqrt(_group_ms(kh, msel) + EPS)
        k_o[:, :, hd * LANES:(hd + 1) * LANES] = (to3d(khn) + kr).astype(bf16)


def _const_spec(shape):
    nd = len(shape)
    return pl.BlockSpec(shape, lambda *_: (0,) * nd, pipeline_mode=pl.Buffered(1))


def _inproj_call(x, sh1, sc1, rope_tabs, p, n_heads, ts):
    b, s, d = x.shape
    w = p["lam"].shape[-1]
    kl = p["kvg"].shape[-1]
    hw = n_heads * LANES
    m = b * ts
    consts = [p["win"], p["cw"], p["cb"], p["wa"], p["ba"], p["wx"], p["bx"], p["lam"], p["qlg"],
              p["wuq"], p["msel"], p["kvg"], p["wk"], p["wv"], p["vone"], p["log"]]
    in_specs = ([pl.BlockSpec((b, ts, d), lambda i: (0, i, 0)), _const_spec(sh1.shape), _const_spec(sc1.shape)]
                + [pl.BlockSpec((ts, LANES), lambda i: (i, 0))] * len(rope_tabs)
                + [_const_spec(c.shape) for c in consts])

    def tile(n, dt):
        return pl.BlockSpec((b, ts, n), lambda i: (0, i, 0)), jax.ShapeDtypeStruct((b, s, n), dt)

    outs = [tile(w, bf16), tile(hw, bf16), tile(hw, bf16), tile(hw, bf16), tile(kl, f32), tile(_ROPE, f32),
            (pl.BlockSpec((b, w), lambda i: (0, 0)), jax.ShapeDtypeStruct((b, w), f32)),
            (pl.BlockSpec((b, 8, w), lambda i: (0, 0, 0)), jax.ShapeDtypeStruct((b, 8, w), f32))]
    nlc = w // LANES
    return pl.pallas_call(
        functools.partial(_inproj_kernel, n_heads=n_heads),
        grid=(s // ts,),
        in_specs=in_specs,
        out_specs=[o[0] for o in outs],
        out_shape=[o[1] for o in outs],
        scratch_shapes=[pltpu.VMEM((b, 8, w), f32)] + [pltpu.VMEM((nlc, b * (ts + 8), LANES), f32)] * 3
        + [pltpu.VMEM((nlc, b, LANES), f32)],
        compiler_params=pltpu.CompilerParams(dimension_semantics=("arbitrary",), vmem_limit_bytes=VMEM_LIMIT),
        name="prompt_inproj",
    )(x, sh1, sc1, *rope_tabs, *consts)


def _attn_kernel(q_ref, k_ref, v_ref, g_ref, o_ref, m_s, acc_s, *, n_heads, tk):
    tq = q_ref.shape[1]
    nmask = tq // tk

    def block(j, r0, first):
        start = j * tk
        vis = (lax.broadcasted_iota(jnp.int32, (tq - r0, tk), 1)
               <= lax.broadcasted_iota(jnp.int32, (tq - r0, tk), 0))
        for hd in range(n_heads):
            hs = slice(hd * LANES, (hd + 1) * LANES)
            s = _dot_nt(q_ref[0, r0:, hs], k_ref[0, pl.ds(start, tk), hs])
            s = jnp.where(vis, s, -1e30)
            smax = jnp.max(s, axis=-1, keepdims=True)
            m_new = jnp.broadcast_to(smax, (tq - r0, LANES)) if first else jnp.maximum(m_s[hd, r0:, :], smax)
            pm = jnp.exp2(s - jnp.concatenate([m_new] * (tk // LANES), axis=-1))
            pv = _dot(pm.astype(bf16), v_ref[0, pl.ds(start, tk), hs])
            if first:
                acc_s[hd, r0:, :] = pv
            else:
                acc_s[hd, r0:, :] = jnp.exp2(m_s[hd, r0:, :] - m_new) * acc_s[hd, r0:, :] + pv
            m_s[hd, r0:, :] = m_new

    for jm in range(nmask):
        block(jm, jm * tk, first=(jm == 0))
    lane = lax.broadcasted_iota(jnp.int32, (1, LANES), 1)
    half = LANES // 2
    heads = []
    for hd in range(n_heads):
        acc = acc_s[hd]
        lo = (hd % 2) * half
        one_lane = (half - lo)
        l = jnp.sum(jnp.where(lane == one_lane, acc, 0.0), axis=-1, keepdims=True)
        heads.append(jnp.where((lane >= lo) & (lane < lo + half), acc, 0.0) / l)
    o = jnp.concatenate([heads[2 * pp] + heads[2 * pp + 1] for pp in range(n_heads // 2)], axis=-1)
    o_ref[0] = _rms(o, g_ref[...]).astype(bf16)


def _attn_call(q, k, v, g, n_heads, tk):
    b, s, hw = q.shape
    wout = g.shape[-1]
    seq = pl.BlockSpec((1, s, hw), lambda bi: (bi, 0, 0))
    return pl.pallas_call(
        functools.partial(_attn_kernel, n_heads=n_heads, tk=tk),
        grid=(b,),
        in_specs=[seq, seq, seq, pl.BlockSpec((1, wout), lambda bi: (0, 0))],
        out_specs=pl.BlockSpec((1, s, wout), lambda bi: (bi, 0, 0)),
        out_shape=jax.ShapeDtypeStruct((b, s, wout), bf16),
        scratch_shapes=[pltpu.VMEM((n_heads, s, LANES), f32), pltpu.VMEM((n_heads, s, LANES), f32)],
        compiler_params=pltpu.CompilerParams(dimension_semantics=("arbitrary",), vmem_limit_bytes=VMEM_LIMIT),
        name="prompt_attn",
    )(q, k, v, g)


def _mix_and_norm2(x, lru, attn, wo_ref, g1, gs2, sh2):
    w = lru.shape[-1]
    mixed = _dot(lru, wo_ref[pl.ds(0, w), :]) + _dot(attn, wo_ref[pl.ds(w, attn.shape[-1]), :])
    x1 = x + g1 * mixed
    h2 = (_rms(x1, gs2) + sh2).astype(bf16)
    return x1, h2


def _up_cols(wup_ref, c, ck):
    return wup_ref[:, c * ck:(c + 1) * ck]


def _ffn_kernel(x_ref, lru_ref, attn_ref, g1_ref, sh2_ref, gs2_ref, g2_ref, wo_ref, wup_ref,
                fcw_ref, fcb_ref, wdn_ref, y_o, tail_o, tail_s, act_s):
    si = pl.program_id(1)
    tm = x_ref.shape[1]
    nc = fcw_ref.shape[0] // 2
    ck = fcw_ref.shape[2]
    kconv = fcw_ref.shape[1]

    @pl.when(si == 0)
    def _():
        tail_s[...] = jnp.zeros_like(tail_s)

    x1, h2 = _mix_and_norm2(x_ref[0], lru_ref[0], attn_ref[0], wo_ref, g1_ref[0], gs2_ref[0], sh2_ref[0])
    r8 = lax.broadcasted_iota(jnp.int32, (8, 1), 0)

    def conv(up, c):
        tail = tail_s[c]
        first = up[0:8]
        out = fcb_ref[c]
        cw = fcw_ref[c]
        for j in range(kconv - 1):
            k = kconv - 1 - j
            head = jnp.where(r8 < k, pltpu.roll(tail, k, 0), pltpu.roll(first, k, 0))
            sh = jnp.concatenate([head, pltpu.roll(up, k, 0)[8:]], axis=0)
            out = out + sh * cw[j:j + 1, :]
        out = out + up * cw[kconv - 1:kconv, :]
        tail_s[c] = up[tm - 8:]
        tail_o[0, :, c * ck:(c + 1) * ck] = up[tm - 8:]
        return out

    for c in range(nc):
        val = conv(_dot(h2, _up_cols(wup_ref, c, ck)), c)
        gt = conv(_dot(h2, _up_cols(wup_ref, nc + c, ck)), nc + c)
        act_s[:, c * ck:(c + 1) * ck] = (_gelu_x2(gt) * val).astype(bf16)
    acc = _dot(act_s[...], wdn_ref[...])
    y_o[0] = x1 + g2_ref[0] * acc


def _ffn_call(x, lru, attn, g1, sh2, gs2, g2, p, tm):
    b, s, d = x.shape
    w = lru.shape[-1]
    nc2, _, ck = p["fcw"].shape
    nc = nc2 // 2
    row = lambda n: pl.BlockSpec((1, tm, n), lambda bi, si: (bi, si, 0))
    modspec = pl.BlockSpec((1, 1, d), lambda bi, si: (bi, 0, 0))
    consts = [p["wo"], p["wup"], p["fcw"], p["fcb"], p["wdn"]]
    return pl.pallas_call(
        _ffn_kernel,
        grid=(b, s // tm),
        in_specs=[row(d), row(w), row(attn.shape[-1]), modspec, modspec, modspec, modspec]
        + [_const_spec(c.shape) for c in consts],
        out_specs=[row(d), pl.BlockSpec((1, 8, 2 * nc * ck), lambda bi, si: (bi, 0, 0))],
        out_shape=[jax.ShapeDtypeStruct((b, s, d), f32), jax.ShapeDtypeStruct((b, 8, 2 * nc * ck), f32)],
        scratch_shapes=[pltpu.VMEM((2 * nc, 8, ck), f32), pltpu.VMEM((tm, nc * ck), bf16)],
        compiler_params=pltpu.CompilerParams(dimension_semantics=("arbitrary", "arbitrary"),
                                             vmem_limit_bytes=VMEM_LIMIT),
        name="prompt_ffn",
    )(x, lru, attn, g1, sh2, gs2, g2, *consts)


def _spre_kernel(x_ref, sh_ref, sc_ref, qca_ref, qsb_ref, kca_ref, ksb_ref, win_ref, cbuf_ref, cw_ref,
                 cb_ref, wa_ref, ba_ref, wx_ref, bx_ref, lam_ref, h0_ref, qlg_ref, wuq_ref, msel_ref,
                 kvg_ref, wk_ref, log_ref,
                 lru_o, xlru_o, hnew_o, qabs_o, qr_o, kvn_o, kr_o, *, n_heads, first_pos):
    w = lam_ref.shape[-1]
    ql = qlg_ref.shape[-1]
    kl = kvg_ref.shape[-1]
    kconv = cw_ref.shape[0]
    ident = lambda v: v

    h = _rms(x_ref[...], sc_ref[...]) + sh_ref[...]
    z = _dot(h.astype(bf16), win_ref[...])
    x_lru = z[:, 0:w]
    g_lru = z[:, w:2 * w]
    q_lat = z[:, 2 * w:2 * w + ql]
    kv_lat = z[:, 2 * w + ql:2 * w + ql + kl]
    kr_pre = z[:, 2 * w + ql + kl:2 * w + ql + kl + LANES]

    x_conv = cb_ref[...]
    for j in range(kconv - 1):
        x_conv = x_conv + cbuf_ref[j] * cw_ref[j:j + 1, :]
    x_conv = x_conv + x_lru * cw_ref[kconv - 1:kconv, :]
    xlru_o[...] = x_lru
    a, mult, ux = _lru_gates(x_conv, wa_ref, ba_ref[...], wx_ref, bx_ref[...], lam_ref[...])
    if first_pos:
        mult = jnp.ones_like(mult)
    hn = a * h0_ref[...] + mult * ux
    hnew_o[...] = hn
    lru_o[...] = _rms(hn * jax.nn.gelu(g_lru), log_ref[...]).astype(bf16)

    msel = msel_ref[...]
    qn = _rms(q_lat, qlg_ref[...]).astype(bf16)
    q = _dot(qn, wuq_ref[...])
    for hd, o in enumerate(_q_heads(q, msel, qca_ref[...], qsb_ref[...], n_heads, ident)):
        wkh = wk_ref[:, hd * LANES:(hd + 1) * LANES]
        qabs_o[hd] = _dot_nt(o.astype(bf16), wkh).astype(bf16)
        qr_o[hd] = jnp.concatenate([o[:, _R1_LO:_R1_LO + 16], o[:, _R2_LO:_R2_LO + 16]], axis=-1).astype(bf16)

    kvn_o[...] = _rms(kv_lat, kvg_ref[...])
    kr = _norm_rope(kr_pre, msel, kca_ref[...], ksb_ref[...], ident)
    kr_o[...] = jnp.concatenate([kr[:, _R1_LO:_R1_LO + 16], kr[:, _R2_LO:_R2_LO + 16]], axis=-1)


def _spre_call(x, sh1, sc1, rope_tabs, cbuf, h0, p, n_heads, first_pos):
    nb, d = x.shape
    w = p["lam"].shape[-1]
    kl = p["kvg"].shape[-1]
    out_shape = [jax.ShapeDtypeStruct((nb, w), bf16), jax.ShapeDtypeStruct((nb, w), f32),
                 jax.ShapeDtypeStruct((nb, w), f32), jax.ShapeDtypeStruct((n_heads, nb, kl), bf16),
                 jax.ShapeDtypeStruct((n_heads, nb, _ROPE), bf16), jax.ShapeDtypeStruct((nb, kl), f32),
                 jax.ShapeDtypeStruct((nb, _ROPE), f32)]
    return pl.pallas_call(
        functools.partial(_spre_kernel, n_heads=n_heads, first_pos=first_pos),
        out_shape=out_shape,
        compiler_params=pltpu.CompilerParams(vmem_limit_bytes=VMEM_LIMIT),
        name="sample_inproj",
    )(x, sh1, sc1, *rope_tabs, p["win"], cbuf, p["cw"], p["cb"], p["wa"], p["ba"], p["wx"], p["bx"],
      p["lam"], h0, p["qlg"], p["wuq"], p["msel"], p["kvg"], p["wk"], p["log"])


def _sattn_kernel(pt_ref, lat_hbm, krt_hbm, wkt_ref, qabs_ref, qr_ref, latn_ref, krn_ref, o_ref,
                  lat_buf, kr_buf, lhs_s, latb_s, s_s, car_s, sems, *, layer, n_heads, ppsub):
    b = pl.program_id(0)
    nseq = pl.num_programs(0) - 1
    _, npg, page, kl = lat_buf.shape
    nk = wkt_ref.shape[0]
    hp = qabs_ref.shape[0]
    tk = ppsub * page
    slot = lax.rem(b, 2)

    def page_copies(src_page, slot_, pg):
        return (pltpu.make_async_copy(lat_hbm.at[layer, src_page], lat_buf.at[slot_, pg], sems.at[0, slot_]),
                pltpu.make_async_copy(krt_hbm.at[layer, src_page], kr_buf.at[slot_, pg], sems.at[1, slot_]))

    def start_pages(seq, slot_, pg0, n):
        for i in range(n):
            lat_cp, kr_cp = page_copies(pt_ref[seq * npg + pg0 + i], slot_, pg0 + i)
            kr_cp.start(priority=i % 2)
            lat_cp.start(priority=i % 2)

    @pl.when(b == 0)
    def _():
        def body(pg, c):
            start_pages(0, 0, pg, 1)
            return c
        lax.fori_loop(0, npg, body, 0)
        lhs_s[0:nk, :] = wkt_ref[...]
        latb_s[1] = jnp.zeros(latb_s.shape[1:], bf16)
        s_s[1] = jnp.zeros(s_s.shape[1:], f32)
        car_s[...] = jnp.zeros_like(car_s)

    lhs_s[nk:nk + hp, :] = qabs_ref[...]

    def wait_slot(slot_):
        pltpu.make_async_copy(lat_hbm.at[layer, pl.ds(0, npg)], lat_buf.at[slot_], sems.at[0, slot_]).wait()
        pltpu.make_async_copy(krt_hbm.at[layer, pl.ds(0, npg)], kr_buf.at[slot_], sems.at[1, slot_]).wait()

    wait_slot(slot)

    def scores(latb, krt):
        n = latb.shape[0]
        big = _dot_nt(lhs_s[...], latb)
        knt = big[0:nk]
        ssq = jnp.sum((knt * knt).reshape(n_heads, _NOPE, n), axis=1)
        rs = lax.rsqrt(ssq * (1.0 / _NOPE) + EPS)
        sr = _dot(qr_ref[...], krt)
        s8 = big[nk:nk + n_heads] * rs + sr[0:n_heads]
        return jnp.concatenate([s8, jnp.zeros((hp - n_heads, n), f32)], axis=0)

    def softmax_update(carry, parts):
        m_run, l_run, acc = carry
        m_new = m_run
        for _, s in parts:
            m_new = jnp.maximum(m_new, jnp.max(s, axis=-1, keepdims=True))
        alpha = jnp.exp2(m_run - m_new)
        l_new = alpha * l_run
        acc = alpha * acc
        for latb, s in parts:
            pm = jnp.exp2(s - jnp.concatenate([m_new] * (s.shape[-1] // LANES), axis=-1))
            l_new = l_new + jnp.sum(pm, axis=-1, keepdims=True)
            acc = acc + _dot(pm.astype(bf16), latb)
        return m_new, l_new, acc

    nxt = jnp.minimum(b + 1, nseq - 1)
    nsub = npg // ppsub
    spi = 6 if (nsub - 2) % 6 == 0 else 2
    trips = (nsub - 2) // spi

    def request(j):
        if trips != 1:
            start_pages(nxt, 1 - slot, j * ppsub, ppsub)
        elif 2 * j < nsub:
            start_pages(nxt, 1 - slot, 2 * j * ppsub, 2 * ppsub)

    def stage(j, st, with_new_token=False):
        request(j)
        p0 = j * ppsub if isinstance(j, int) else pl.multiple_of(j * ppsub, ppsub)
        lat = [lat_buf[slot, pl.ds(p0, ppsub)].reshape(tk, kl).astype(bf16)]
        krt = [kr_buf[slot, p0 + i].astype(bf16) for i in range(ppsub)]
        if with_new_token:
            lat.append(jnp.broadcast_to(latn_ref[...], (page, kl)).astype(bf16))
            krt.append(jnp.broadcast_to(krn_ref[...], (krn_ref.shape[0], page)).astype(bf16))
        latb = jnp.concatenate(lat, axis=0)
        s = scores(latb, jnp.concatenate(krt, axis=-1))
        n = latb.shape[0]
        if with_new_token:
            s = jnp.where(lax.broadcasted_iota(jnp.int32, s.shape, 1) <= tk, s, -1e30)
        latb_s[st, 0:n, :] = latb
        s_s[st, :, 0:n] = s

    def stashed(st, n=tk):
        return [(latb_s[st, 0:n, :], s_s[st, :, 0:n])]

    stage(0, 0)
    _, l_prev, acc_prev = softmax_update((car_s[0], car_s[1], car_s[2]), stashed(1, tk + page))
    o_ref[...] = acc_prev / l_prev

    def body(k, carry):
        for u in range(0, spi, 2):
            stage(spi * k + u + 1, 1)
            carry = softmax_update(carry, stashed(0))
            stage(spi * k + u + 2, 0)
            carry = softmax_update(carry, stashed(1))
        return carry

    carry = (jnp.full((hp, LANES), -1e30, f32), jnp.zeros((hp, LANES), f32), jnp.zeros((hp, kl), f32))
    carry = body(0, carry) if trips == 1 else lax.fori_loop(0, trips, body, carry)
    stage(nsub - 1, 1, with_new_token=True)
    carry = softmax_update(carry, stashed(0))
    for i in range(3):
        car_s[i] = carry[i]

    @pl.when(b == nseq)
    def _():
        wait_slot(1 - slot)


def _sattn_call(page_table, cache_lat, cache_krt, layer, wkt, qabs, qr, latn, krn, n_heads, ppsub):
    nb, npg = page_table.shape
    _, _, page, kl = cache_lat.shape
    rd = cache_krt.shape[2]
    hp = qabs.shape[1]
    nk = wkt.shape[0]
    assert npg % (2 * ppsub) == 0 and kl == LANES
    cur = lambda bi, pt: (jnp.minimum(bi, nb - 1), 0, 0)
    grid_spec = pltpu.PrefetchScalarGridSpec(
        num_scalar_prefetch=1,
        grid=(nb + 1,),
        in_specs=[pl.BlockSpec(memory_space=pl.ANY),
                  pl.BlockSpec(memory_space=pl.ANY),
                  pl.BlockSpec((nk, kl), lambda bi, pt: (0, 0)),
                  pl.BlockSpec((None, hp, kl), cur),
                  pl.BlockSpec((None, hp, rd), cur),
                  pl.BlockSpec((None, 1, kl), cur),
                  pl.BlockSpec((None, rd, 1), cur)],
        out_specs=pl.BlockSpec((None, hp, kl), lambda bi, pt: (jnp.maximum(bi - 1, 0), 0, 0)),
        scratch_shapes=[pltpu.VMEM((2, npg, page, kl), f32), pltpu.VMEM((2, npg, rd, page), f32),
                        pltpu.VMEM((nk + hp, kl), bf16), pltpu.VMEM((2, (ppsub + 1) * page, kl), bf16),
                        pltpu.VMEM((2, hp, (ppsub + 1) * page), f32), pltpu.VMEM((3, hp, LANES), f32),
                        pltpu.SemaphoreType.DMA((2, 2))])
    return pl.pallas_call(
        functools.partial(_sattn_kernel, layer=layer, n_heads=n_heads, ppsub=ppsub),
        grid_spec=grid_spec,
        out_shape=jax.ShapeDtypeStruct((nb, hp, kl), f32),
        compiler_params=pltpu.CompilerParams(dimension_semantics=("arbitrary",), vmem_limit_bytes=VMEM_LIMIT),
        name="sample_attn",
    )(page_table.reshape(-1), cache_lat, cache_krt, wkt, qabs, qr, latn, krn)


def _spost_kernel(x_ref, lru_ref, olat_ref, g1_ref, sh2_ref, gs2_ref, g2_ref, mog_ref, wv_ref, wo_ref,
                  wup_ref, fcw_ref, fcb_ref, wdn_ref, fbuf_ref, y_o, up_o, *, n_heads):
    nc = fcw_ref.shape[0] // 2
    ck = fcw_ref.shape[2]
    kconv = fcw_ref.shape[1]
    heads = [_dot(olat_ref[hd].astype(bf16), wv_ref[:, hd * LANES:(hd + 1) * LANES]) for hd in range(n_heads)]
    attn = jnp.concatenate([heads[2 * pp] + heads[2 * pp + 1] for pp in range(n_heads // 2)], axis=-1)
    attn = _rms(attn, mog_ref[...]).astype(bf16)
    x1, h2 = _mix_and_norm2(x_ref[...], lru_ref[...], attn, wo_ref, g1_ref[...], gs2_ref[...], sh2_ref[...])

    def conv(up, c):
        out = fcb_ref[c]
        cw = fcw_ref[c]
        for j in range(kconv - 1):
            out = out + fbuf_ref[j, :, c * ck:(c + 1) * ck] * cw[j:j + 1, :]
        up_o[:, c * ck:(c + 1) * ck] = up
        return out + up * cw[kconv - 1:kconv, :]

    acc = jnp.zeros(x1.shape, f32)
    for c in range(nc):
        val = conv(_dot(h2, _up_cols(wup_ref, c, ck)), c)
        gt = conv(_dot(h2, _up_cols(wup_ref, nc + c, ck)), nc + c)
        acc = acc + _dot((_gelu_x2(gt) * val).astype(bf16), wdn_ref[c * ck:(c + 1) * ck, :])
    y_o[...] = x1 + g2_ref[...] * acc


def _spost_call(x, lru, olat, g1, sh2, gs2, g2, fbuf, p, n_heads):
    nb, d = x.shape
    return pl.pallas_call(
        functools.partial(_spost_kernel, n_heads=n_heads),
        out_shape=[jax.ShapeDtypeStruct((nb, d), f32), jax.ShapeDtypeStruct((nb, p["wup"].shape[1]), f32)],
        compiler_params=pltpu.CompilerParams(vmem_limit_bytes=VMEM_LIMIT),
        name="sample_ffn",
    )(x, lru, olat, g1, sh2, gs2, g2, p["mog"], p["wv"], p["wo"], p["wup"], p["fcw"], p["fcb"], p["wdn"], fbuf)


def _take_cols(wmat, idx):
    padded = jnp.concatenate([wmat, jnp.zeros(wmat.shape[:-1] + (1,), wmat.dtype)], axis=-1)
    return jnp.take(padded, jnp.asarray(np.where(idx < 0, wmat.shape[-1], idx)), axis=-1)


def _block_diag_groups(wh):
    nh, hd, _ = wh.shape
    per = MXU_DIM // hd
    grouped = wh.reshape(nh // per, per, hd, 1, hd)
    same = jnp.eye(per, dtype=wh.dtype).reshape(1, per, 1, per, 1)
    return (grouped * same).reshape(nh // per, MXU_DIM, MXU_DIM).astype(bf16)


def _prep_layer(l, n_heads, scale, w_in, lru_conv_w, lru_conv_b, lru_w_a, lru_b_a, lru_w_x, lru_b_x, lru_lambda,
                norm1_g, q_lora_norm_g, w_uq, q_nope_norm_g, q_rope_norm_g, kv_lora_norm_g, k_rope_norm_g, w_ukv,
                k_nope_norm_g, lru_out_norm_g, mla_out_norm_g, w_o, norm2_g, w_up, ffn_conv_w, ffn_conv_b, w_down):
    src = _h128_src()
    w = lru_lambda.shape[-1]
    ql = q_lora_norm_g.shape[-1]
    kl = kv_lora_norm_g.shape[-1]
    qk = _NOPE + _ROPE
    vd = w_ukv.shape[-1] // n_heads - _NOPE
    row = lambda v: v.reshape(1, -1).astype(f32)

    kr_src = np.where(src >= _NOPE, src - _NOPE, -1)
    win = w_in[l]
    base = 2 * w + ql + kl
    win_ext = jnp.concatenate([win[:, :base], _take_cols(win[:, base:], kr_src)], axis=-1).astype(bf16)

    q_idx = np.concatenate([np.where(src >= 0, src + h * qk, -1) for h in range(n_heads)])
    wuq = _take_cols(w_uq[l], q_idx).astype(bf16)
    k_src = np.where((src >= 0) & (src < _NOPE), src, -1)
    k_idx = np.concatenate([np.where(k_src >= 0, k_src + h * (_NOPE + vd), -1) for h in range(n_heads)])
    wk = _take_cols(w_ukv[l], k_idx).astype(bf16)
    v_idx = []
    vone = np.zeros((1, n_heads * LANES), np.float32)
    for h in range(n_heads):
        slab = np.full((LANES,), -1, np.int64)
        off = (h % 2) * vd
        slab[off:off + vd] = h * (_NOPE + vd) + _NOPE + np.arange(vd)
        v_idx.append(slab)
        vone[0, h * LANES + (vd - off)] = 1.0
    wv = _take_cols(w_ukv[l], np.concatenate(v_idx)).astype(bf16)
    wkt_idx = np.concatenate([h * (_NOPE + vd) + np.arange(_NOPE) for h in range(n_heads)])
    wkt = jnp.take(w_ukv[l], jnp.asarray(wkt_idx), axis=-1).T.astype(bf16)

    nope_tab = lambda g: _take_cols(g.reshape(1, -1), k_src)
    rope_tab = lambda g: _take_cols(g.reshape(1, -1), kr_src)
    is_n = (k_src >= 0).astype(np.float32)
    is_r = (kr_src >= 0).astype(np.float32)
    msel = jnp.asarray(np.outer(is_n, is_n) / _NOPE + np.outer(is_r, is_r) / _ROPE).astype(bf16)
    gk_full = nope_tab(k_nope_norm_g[l]) + jnp.asarray(1.0 - is_n).reshape(1, LANES)
    gq = (nope_tab(q_nope_norm_g[l]) + rope_tab(q_rope_norm_g[l])) * scale * gk_full

    dff = w_down.shape[1]
    ck = MXU_DIM
    nc = dff // ck
    half_val = jnp.concatenate([jnp.full((nc, 1, 1), 0.5, f32), jnp.ones((nc, 1, 1), f32)], axis=0)
    fcw = ffn_conv_w[l].reshape(-1, 2 * nc, ck).transpose(1, 0, 2).astype(f32) * half_val
    fcb = ffn_conv_b[l].reshape(2 * nc, 1, ck).astype(f32) * half_val
    return dict(
        n1g=row(norm1_g[l]), win=win_ext, cw=lru_conv_w[l].astype(f32), cb=row(lru_conv_b[l]),
        wa=_block_diag_groups(lru_w_a[l]), ba=row(lru_b_a[l]), wx=_block_diag_groups(lru_w_x[l]), bx=row(lru_b_x[l]),
        lam=row(lru_lambda[l]), qlg=row(q_lora_norm_g[l]), wuq=wuq, gq=gq.astype(f32), msel=msel,
        kvg=row(kv_lora_norm_g[l]), gkr=rope_tab(k_rope_norm_g[l]).astype(f32), wk=wk,
        wv=wv, vone=jnp.asarray(vone), wkt=wkt, log=row(lru_out_norm_g[l]),
        mog=row(mla_out_norm_g[l]), wo=w_o[l].astype(bf16), n2g=row(norm2_g[l]), wup=w_up[l].astype(bf16),
        fcw=fcw, fcb=fcb, wdn=w_down[l].astype(bf16))


def _rope_tables(pos):
    half = _ROPE // 2
    inv = ROPE_THETA ** (-np.arange(0, _ROPE, 2, dtype=np.float64) / _ROPE)
    ang = np.asarray(pos, np.float64)[:, None] * inv[None, :]
    cos, sin = np.cos(ang), np.sin(ang)
    n = ang.shape[0]
    cc = np.zeros((n, LANES), np.float32)
    cc[:, 0:_R1_LO] = 1.0
    cc[:, _R1_LO + half:_R1_LO + half + 16] = 1.0
    cc[:, _R1_LO:_R1_LO + half] = cos
    cc[:, _R2_LO:_R2_LO + half] = cos
    ss = np.zeros((n, LANES), np.float32)
    ss[:, _R1_LO:_R1_LO + half] = -sin
    ss[:, _R2_LO:_R2_LO + half] = sin
    return jnp.asarray(cc), jnp.asarray(ss)


def _gain_rope_tables(cc, ss, gq, gkr):
    swap = lambda g: jnp.roll(g, LANES // 2, axis=-1)
    return cc * gq, ss * swap(gq), cc * gkr, ss * swap(gkr)


def _tiles(s, npg):
    ts = 64 if s % 64 == 0 else 32
    tk = min(MXU_DIM, s)
    tm = min(512, s)
    ppsub = 16 if npg % 32 == 0 else 1
    assert s % ts == 0 and s % tk == 0 and s % tm == 0 and s <= 2048
    return ts, tk, tm, ppsub


def kernel(x_prompt, x_sample, cache_kv_latent, cache_k_rope, state_lru_h, state_lru_conv, state_ffn_conv,
           page_table, c_prompt, c_sample, w_ada, b_ada, norm1_g, w_in, lru_conv_w, lru_conv_b, lru_w_a, lru_b_a,
           lru_w_x, lru_b_x, lru_lambda, q_lora_norm_g, w_uq, q_nope_norm_g, q_rope_norm_g, kv_lora_norm_g,
           k_rope_norm_g, w_ukv, k_nope_norm_g, lru_out_norm_g, mla_out_norm_g, w_o, norm2_g, w_up, ffn_conv_w,
           ffn_conv_b, w_down):
    b, s, d = x_prompt.shape
    nb, ds, _ = x_sample.shape
    depth = w_in.shape[0]
    assert ds == 1 and q_nope_norm_g.shape[-1] == _NOPE and q_rope_norm_g.shape[-1] == _ROPE
    n_heads = w_uq.shape[-1] // (_NOPE + _ROPE)
    scale = float(_NOPE + _ROPE) ** -0.5 * float(np.log2(np.e))
    npg = page_table.shape[1]
    n_past = npg * cache_kv_latent.shape[2]
    ts, tk, tm, ppsub = _tiles(s, npg)
    hp = BF16_ROWS

    cache_krt = jnp.swapaxes(cache_k_rope, 2, 3)
    cc_p, ss_p = _rope_tables(np.arange(s))
    cc_s, ss_s = _rope_tables(n_past + np.arange(1))

    y_p = x_prompt
    y_s = x_sample.reshape(nb, d)
    c_all = jnp.concatenate([c_prompt, c_sample], axis=0)
    outs_p = [[] for _ in range(5)]
    outs_s = [[] for _ in range(5)]
    for l in range(depth):
        p = _prep_layer(l, n_heads, scale, w_in, lru_conv_w, lru_conv_b, lru_w_a, lru_b_a, lru_w_x, lru_b_x,
                        lru_lambda, norm1_g, q_lora_norm_g, w_uq, q_nope_norm_g, q_rope_norm_g, kv_lora_norm_g,
                        k_rope_norm_g, w_ukv, k_nope_norm_g, lru_out_norm_g, mla_out_norm_g, w_o, norm2_g, w_up,
                        ffn_conv_w, ffn_conv_b, w_down)
        mod = _mod_call(c_all, w_ada[l], b_ada[l])
        sh1, sc1, g1, sh2, sc2, g2 = jnp.split(mod, 6, axis=-1)
        mods = [sh1, p["n1g"] * (1.0 + sc1), g1, sh2, p["n2g"] * (1.0 + sc2), g2]
        mp = [m_[:b, None, :] for m_ in mods]
        ms = [m_[b:] for m_ in mods]

        lru_p, q_p, k_p, v_p, kvlat_p, krope_p, hlast_p, xtail_p = _inproj_call(
            y_p, mp[0], mp[1], _gain_rope_tables(cc_p, ss_p, p["gq"], p["gkr"]), p, n_heads, ts)
        attn_p = _attn_call(q_p, k_p, v_p, p["mog"], n_heads, tk)
        y_p, ftail_p = _ffn_call(y_p, lru_p, attn_p, mp[2], mp[3], mp[4], mp[5], p, tm)
        kc = lru_conv_w.shape[1]
        fk = ffn_conv_w.shape[1]
        for j, o in enumerate((kvlat_p, krope_p, hlast_p, xtail_p[:, 8 - (kc - 1):], ftail_p[:, 8 - (fk - 1):])):
            outs_p[j].append(o)

        cbuf = jnp.swapaxes(state_lru_conv[l], 0, 1)
        fbuf = jnp.swapaxes(state_ffn_conv[l], 0, 1)
        lru_s, xlru_s, hnew_s, qabs, qr, kvn_s, kr_s = _spre_call(
            y_s, ms[0], ms[1], _gain_rope_tables(cc_s, ss_s, p["gq"], p["gkr"]), cbuf, state_lru_h[l], p, n_heads,
            n_past == 0)
        pad_heads = lambda t: jnp.pad(jnp.swapaxes(t, 0, 1), ((0, 0), (0, hp - n_heads), (0, 0)))
        olat = _sattn_call(page_table, cache_kv_latent, cache_krt, l, p["wkt"], pad_heads(qabs), pad_heads(qr),
                           kvn_s[:, None, :], kr_s[:, :, None], n_heads, ppsub)
        olat = jnp.swapaxes(olat[:, :n_heads], 0, 1)
        y_s, up_s = _spost_call(y_s, lru_s, olat, ms[2], ms[3], ms[4], ms[5], fbuf, p, n_heads)
        lru_conv_new = jnp.concatenate([state_lru_conv[l][:, 1:], xlru_s[:, None, :]], axis=1)
        ffn_conv_new = jnp.concatenate([state_ffn_conv[l][:, 1:], up_s[:, None, :]], axis=1)
        for j, o in enumerate((kvn_s[:, None, :], kr_s[:, None, :], hnew_s, lru_conv_new, ffn_conv_new)):
            outs_s[j].append(o)

    return (y_p, y_s.reshape(nb, 1, d), *[jnp.stack(o) for o in outs_p], *[jnp.stack(o) for o in outs_s])
```
